```python
import jax, jax.numpy as jnp
from jax import lax
import numpy as np

D_MODEL = 1024
BATCH = 16
SEQ = 2048
DEPTH = 4

GRID_W = 64
CTX_LEN = 256
N_MIXERS = 2
N_FNET_LAYERS = (DEPTH + 1) // 2
N_RWKV_LAYERS = DEPTH // 2
N_VRES_LAYERS = N_RWKV_LAYERS - 1
FNET_GROUPS = 4
FNET_GROUP_DIM = D_MODEL // FNET_GROUPS
HEAD_SIZE = 64
N_HEADS = D_MODEL // HEAD_SIZE
D_DECAY_LORA = 64
D_AAA_LORA = 64
D_MV_LORA = 32
D_GATE_LORA = 128
N_DIRS = 2
N_EXPERTS = 16
D_EXPERT = 1024
CAPACITY_FACTOR = 2
NORM_EPS = 1e-6
LNX_EPS = 64e-5

kernel_name = 'hybrid_fnet_rwkv7_ecmoe_dit'


def rmsnorm(h, g):
    hf = h.astype(jnp.float32)
    hf = hf * lax.rsqrt(jnp.mean(hf * hf, axis=-1, keepdims=True) + NORM_EPS)
    return (hf * g.astype(jnp.float32)).astype(h.dtype)


def adaln_params(cvec, w, b):
    return jnp.split(jax.nn.silu(cvec) @ w + b, 6, axis=-1)


def fourier_mix(h, w_o, b_o):
    bsz, t, _ = h.shape
    hg = h.astype(jnp.float32).reshape(bsz, t, FNET_GROUPS, FNET_GROUP_DIM)
    f = jnp.fft.fft2(hg, axes=(1, 3), norm='ortho').real
    return f.reshape(bsz, t, D_MODEL).astype(h.dtype) @ w_o + b_o


def qshift_grid(h):
    bsz, t, d = h.shape
    rows = t // GRID_W
    g = h.reshape(bsz, rows, GRID_W, d)
    q = d // 4
    left = jnp.pad(g[:, :, :-1, :q], ((0, 0), (0, 0), (1, 0), (0, 0)))
    right = jnp.pad(g[:, :, 1:, q:2 * q], ((0, 0), (0, 0), (0, 1), (0, 0)))
    up = jnp.pad(g[:, :-1, :, 2 * q:3 * q], ((0, 0), (1, 0), (0, 0), (0, 0)))
    down = jnp.pad(g[:, 1:, :, 3 * q:], ((0, 0), (0, 1), (0, 0), (0, 0)))
    return jnp.concatenate([left, right, up, down], axis=-1).reshape(bsz, t, d)


def shift_seq(h):
    half = h.shape[-1] // 2
    prev = jnp.pad(h[:, :-1, :half], ((0, 0), (1, 0), (0, 0)))
    nxt = jnp.pad(h[:, 1:, half:], ((0, 0), (0, 1), (0, 0)))
    return jnp.concatenate([prev, nxt], axis=-1)


def to_heads(t):
    return t.reshape(t.shape[0], t.shape[1], N_HEADS, HEAD_SIZE)


def rwkv_project(h, h_shift, mix, wr, wk, wv, w0, w1, w2, a0, a1, a2, g1, g2, k_k, k_a, v_first, vres):
    xx = h_shift - h
    xr, xw, xk, xv, xa, xg = [h + xx * mix[j] for j in range(6)]
    r = xr @ wr
    k = xk @ wk
    v = xv @ wv
    if vres is not None:
        v0, v1, v2 = vres
        v = v + (v_first - v) * jax.nn.sigmoid(v0 + (xv @ v1) @ v2)
    g = jax.nn.sigmoid(xg @ g1) @ g2
    kk = to_heads((k * k_k).astype(jnp.float32))
    kk = kk * lax.rsqrt(jnp.sum(kk * kk, axis=-1, keepdims=True) + 1e-12)
    kf = k.astype(jnp.float32)
    dirs = []
    for d in range(N_DIRS):
        wlog = (w0[d] + jnp.tanh(xw @ w1[d]) @ w2[d]).astype(jnp.float32)
        wlog = -jax.nn.softplus(-wlog) - 0.5
        decay = jnp.exp(-jnp.exp(wlog))
        a = jax.nn.sigmoid((a0[d] + (xa @ a1[d]) @ a2[d]).astype(jnp.float32))
        k_d = kf * (1.0 + (a - 1.0) * k_a.astype(jnp.float32))
        dirs.append((to_heads(decay), to_heads(k_d), to_heads(a)))
    return to_heads(r.astype(jnp.float32)), v, g, kk, dirs


def wkv_scan(r, decay, k, v, a_vec, b_vec, s0, reverse):
    def step(s, inp):
        r_t, w_t, k_t, v_t, a_t, b_t = inp
        sa = jnp.einsum('bhvk,bhk->bhv', s, a_t)
        s = s * w_t[:, :, None, :] + sa[..., None] * b_t[:, :, None, :] + v_t[..., None] * k_t[:, :, None, :]
        return s, jnp.einsum('bhvk,bhk->bhv', s, r_t)
    xs = tuple(jnp.swapaxes(t, 0, 1) for t in (r, decay, k, v, a_vec, b_vec))
    s_final, ys = lax.scan(step, s0, xs, reverse=reverse)
    return jnp.swapaxes(ys, 0, 1), s_final


def rwkv_output(y, r, v, g, k_dirs, r_k, lnx_w, lnx_b, wo, dtype):
    mu = jnp.mean(y, axis=-1, keepdims=True)
    var = jnp.mean(jnp.square(y - mu), axis=-1, keepdims=True)
    yn = (y - mu) * lax.rsqrt(var + LNX_EPS)
    bsz, t = y.shape[:2]
    yn = yn.reshape(bsz, t, D_MODEL) * lnx_w.astype(jnp.float32) + lnx_b.astype(jnp.float32)
    rk = r_k.astype(jnp.float32)
    bonus = sum(jnp.sum(r * k_d * rk, axis=-1, keepdims=True) for k_d in k_dirs) * v
    out = (yn + bonus.reshape(bsz, t, D_MODEL)).astype(dtype) * g
    return out @ wo


def ec_moe(h, router, wg, wu, wd):
    bsz, t, d = h.shape
    cap = CAPACITY_FACTOR * t // N_EXPERTS
    aff = jax.nn.softmax((h @ router).astype(jnp.float32), axis=-1)
    gates, idx = lax.top_k(jnp.swapaxes(aff, 1, 2), cap)
    xs = jax.vmap(lambda hb, ib: hb[ib])(h, idx)
    hid = jax.nn.silu(jnp.einsum('becd,edf->becf', xs, wg)) * jnp.einsum('becd,edf->becf', xs, wu)
    ys = jnp.einsum('becf,efd->becd', hid, wd) * gates[..., None].astype(h.dtype)
    return jax.vmap(lambda ib, yb: jnp.zeros((t, d), h.dtype).at[ib.reshape(-1)].add(yb.reshape(-1, d)))(idx, ys)


def setup_inputs(seed: int = 0) -> dict:
    key = jax.random.key(seed)
    ks = iter(jax.random.split(key, 40))

    def nrm(shape, scale):
        return jax.random.normal(next(ks), shape, jnp.float32) * scale

    D = D_MODEL
    inv = D ** -0.5
    NR = N_RWKV_LAYERS
    NV = N_VRES_LAYERS
    return {
        'x': nrm((BATCH, SEQ, D), 1.0),
        'c': nrm((BATCH, D), 1.0),
        'ctx': nrm((BATCH, CTX_LEN, D), 1.0),
        'c_ctx': nrm((D,), 1.0),
        'ada_w': nrm((DEPTH, D, 6 * D), 0.5 * inv),
        'ada_b': nrm((DEPTH, 6 * D), 0.01),
        'norm_g': 1.0 + nrm((DEPTH, 2, D), 0.1),
        'fnet_wo': nrm((N_FNET_LAYERS, D, D), inv),
        'fnet_bo': nrm((N_FNET_LAYERS, D), 0.01),
        'rw_mix': jax.random.uniform(next(ks), (NR, 6, D), jnp.float32),
        'rw_wr': nrm((NR, D, D), inv),
        'rw_wk': nrm((NR, D, D), inv),
        'rw_wv': nrm((NR, D, D), inv),
        'rw_wo': nrm((NR, D, D), inv),
        'rw_w0': jax.random.uniform(next(ks), (NR, N_DIRS, D), jnp.float32, -6.0, 1.0),
        'rw_w1': nrm((NR, N_DIRS, D, D_DECAY_LORA), inv),
        'rw_w2': nrm((NR, N_DIRS, D_DECAY_LORA, D), 0.1),
        'rw_a0': nrm((NR, N_DIRS, D), 0.5),
        'rw_a1': nrm((NR, N_DIRS, D, D_AAA_LORA), inv),
        'rw_a2': nrm((NR, N_DIRS, D_AAA_LORA, D), 0.1),
        'rw_v0': nrm((NV, D), 0.5),
        'rw_v1': nrm((NV, D, D_MV_LORA), inv),
        'rw_v2': nrm((NV, D_MV_LORA, D), 0.1),
        'rw_g1': nrm((NR, D, D_GATE_LORA), inv),
        'rw_g2': nrm((NR, D_GATE_LORA, D), D_GATE_LORA ** -0.5),
        'rw_kk': 0.85 + nrm((NR, D), 0.1),
        'rw_ka': 1.0 + nrm((NR, D), 0.1),
        'rw_rk': nrm((NR, N_HEADS, HEAD_SIZE), 0.1),
        'rw_lnx_w': 1.0 + nrm((NR, D), 0.1),
        'rw_lnx_b': nrm((NR, D), 0.01),
        'moe_router': nrm((DEPTH, D, N_EXPERTS), inv),
        'moe_wg': nrm((DEPTH, N_EXPERTS, D, D_EXPERT), inv),
        'moe_wu': nrm((DEPTH, N_EXPERTS, D, D_EXPERT), inv),
        'moe_wd': nrm((DEPTH, N_EXPERTS, D_EXPERT, D), D_EXPERT ** -0.5),
        'final_g': 1.0 + nrm((D,), 0.1),
    }


def reference(x, c, ctx, c_ctx, ada_w, ada_b, norm_g, fnet_wo, fnet_bo, rw_mix, rw_wr, rw_wk, rw_wv, rw_wo,
              rw_w0, rw_w1, rw_w2, rw_a0, rw_a1, rw_a2, rw_v0, rw_v1, rw_v2, rw_g1, rw_g2, rw_kk, rw_ka, rw_rk,
              rw_lnx_w, rw_lnx_b, moe_router, moe_wg, moe_wu, moe_wd, final_g):
    dtype = x.dtype
    v_first_x = None
    v_first_c = None
    for i in range(DEPTH):
        need_ctx = i < DEPTH - 1
        sh1x, sc1x, g1x, sh2x, sc2x, g2x = [m[:, None, :] for m in adaln_params(c, ada_w[i], ada_b[i])]
        sh1c, sc1c, g1c, sh2c, sc2c, g2c = adaln_params(c_ctx, ada_w[i], ada_b[i])
        hx = rmsnorm(x, norm_g[i, 0]) * (1.0 + sc1x) + sh1x
        if i % N_MIXERS == 0:
            fi = i // N_MIXERS
            x = x + g1x * fourier_mix(hx, fnet_wo[fi], fnet_bo[fi])
            if need_ctx:
                hc = rmsnorm(ctx, norm_g[i, 0]) * (1.0 + sc1c) + sh1c
                ctx = ctx + g1c * fourier_mix(hc, fnet_wo[fi], fnet_bo[fi])
        else:
            ri = i // N_MIXERS
            hc = rmsnorm(ctx, norm_g[i, 0]) * (1.0 + sc1c) + sh1c
            vres = None if ri == 0 else (rw_v0[ri - 1], rw_v1[ri - 1], rw_v2[ri - 1])
            prm = (rw_mix[ri], rw_wr[ri], rw_wk[ri], rw_wv[ri], rw_w0[ri], rw_w1[ri], rw_w2[ri],
                   rw_a0[ri], rw_a1[ri], rw_a2[ri], rw_g1[ri], rw_g2[ri], rw_kk[ri], rw_ka[ri])
            r_x, v_x, gt_x, kk_x, dirs_x = rwkv_project(hx, qshift_grid(hx), *prm, v_first_x, vres)
            r_c, v_c, gt_c, kk_c, dirs_c = rwkv_project(hc, shift_seq(hc), *prm, v_first_c, vres)
            if ri == 0:
                v_first_x, v_first_c = v_x, v_c
            vh_x = to_heads(v_x.astype(jnp.float32))
            vh_c = to_heads(v_c.astype(jnp.float32))
            s0 = jnp.zeros((x.shape[0], N_HEADS, HEAD_SIZE, HEAD_SIZE), jnp.float32)
            ys_x = []
            ys_c = []
            for d in range(N_DIRS):
                reverse = d == 1
                dec_c, k_c, a_c = dirs_c[d]
                yc_d, s_ctx = wkv_scan(r_c, dec_c, k_c, vh_c, -kk_c, kk_c * a_c, s0, reverse)
                dec_x, k_x, a_x = dirs_x[d]
                yx_d, _ = wkv_scan(r_x, dec_x, k_x, vh_x, -kk_x, kk_x * a_x, s_ctx, reverse)
                ys_x.append(yx_d)
                ys_c.append(yc_d)
            x = x + g1x * rwkv_output(ys_x[0] + ys_x[1], r_x, vh_x, gt_x, [dd[1] for dd in dirs_x],
                                      rw_rk[ri], rw_lnx_w[ri], rw_lnx_b[ri], rw_wo[ri], dtype)
            if need_ctx:
                ctx = ctx + g1c * rwkv_output(ys_c[0] + ys_c[1], r_c, vh_c, gt_c, [dd[1] for dd in dirs_c],
                                              rw_rk[ri], rw_lnx_w[ri], rw_lnx_b[ri], rw_wo[ri], dtype)
        hx = rmsnorm(x, norm_g[i, 1]) * (1.0 + sc2x) + sh2x
        x = x + g2x * ec_moe(hx, moe_router[i], moe_wg[i], moe_wu[i], moe_wd[i])
        if need_ctx:
            hc = rmsnorm(ctx, norm_g[i, 1]) * (1.0 + sc2c) + sh2c
            ctx = ctx + g2c * ec_moe(hc, moe_router[i], moe_wg[i], moe_wu[i], moe_wd[i])
    return rmsnorm(x, final_g)
```

```python
import functools

import jax
import jax.numpy as jnp
from jax import lax
from jax.experimental import pallas as pl
from jax.experimental.pallas import tpu as pltpu

F32 = jnp.float32
BF16 = jnp.bfloat16

HEAD_SIZE = 64
LANE_GROUP = 256
GRID_W = 64
FNET_GROUPS = 4
CAPACITY_FACTOR = 2
NORM_EPS = 1e-6
LNX_EPS = 64e-5
WKV_CHUNK = 64
VMEM_LIMIT_BYTES = 56 * 1024 * 1024


def _params(*semantics):
    return pltpu.CompilerParams(dimension_semantics=semantics, vmem_limit_bytes=VMEM_LIMIT_BYTES)


def _tile(pref, *extents):
    tm = pref
    while any(n % tm for n in extents if n):
        tm //= 2
    return tm


def _dot(a, b):
    return jnp.dot(a, b, preferred_element_type=F32)


def _dot_nt(a, b):
    return lax.dot_general(a, b, (((1,), (1,)), ((), ())), preferred_element_type=F32)


def _dot_tn(a, b):
    return lax.dot_general(a, b, (((0,), (0,)), ((), ())), preferred_element_type=F32)


def _split2(x):
    hi = x.astype(BF16)
    lo = (x - hi.astype(F32)).astype(BF16)
    return hi, lo


def _split3(x):
    hi = x.astype(BF16)
    r1 = x - hi.astype(F32)
    mid = r1.astype(BF16)
    lo = (r1 - mid.astype(F32)).astype(BF16)
    return hi, mid, lo


def _norm_mod(x, g, sc, sh):
    ms = jnp.mean(x * x, axis=-1, keepdims=True)
    return x * lax.rsqrt(ms + NORM_EPS) * g * (1.0 + sc) + sh


def _sigmoid(x):
    return 1.0 / (1.0 + jnp.exp(-x))


def _seg_ones(n):
    r = lax.broadcasted_iota(jnp.int32, (n, n), 0) // HEAD_SIZE
    c = lax.broadcasted_iota(jnp.int32, (n, n), 1) // HEAD_SIZE
    return jnp.where(r == c, 1.0, 0.0).astype(BF16)


def _head_sum(x, ones):
    d = x.shape[-1]
    w = ones.shape[0]
    parts = []
    for j in range(d // w):
        hi, lo = _split2(x[:, j * w:(j + 1) * w])
        parts.append(_dot(hi, ones) + _dot(lo, ones))
    return parts[0] if len(parts) == 1 else jnp.concatenate(parts, axis=1)


def _adaln_kernel(c_ref, w_ref, b_ref, o_ref):
    c = c_ref[...]
    s = c * _sigmoid(c)
    s_hi, s_lo = _split2(s)
    w_hi, w_lo = _split2(w_ref[...])
    o_ref[...] = _dot(s_hi, w_hi) + _dot(s_lo, w_hi) + _dot(s_hi, w_lo) + b_ref[...]


def _adaln(cc, ada_w, ada_b):
    depth, d, n = ada_w.shape
    rows = cc.shape[0]
    tn = min(n, 1536)
    return pl.pallas_call(
        _adaln_kernel,
        out_shape=jax.ShapeDtypeStruct((depth, rows, n), F32),
        grid=(depth, n // tn),
        in_specs=[pl.BlockSpec((rows, d), lambda l, j: (0, 0)),
                  pl.BlockSpec((None, d, tn), lambda l, j: (l, 0, j)),
                  pl.BlockSpec((None, 1, tn), lambda l, j: (l, 0, j))],
        out_specs=pl.BlockSpec((None, rows, tn), lambda l, j: (l, 0, j)),
        compiler_params=_params("parallel", "parallel"),
        name="adaln",
    )(cc, ada_w, ada_b.reshape(depth, 1, n))


def _fnet_chan_kernel(x_ref, g_ref, sc_ref, sh_ref, cs_ref, o_ref):
    h = _norm_mod(x_ref[...], g_ref[...], sc_ref[...], sh_ref[...])
    gd = cs_ref.shape[0]
    cs = cs_ref[...]
    for j in range(h.shape[1] // gd):
        z = _dot(h[:, j * gd:(j + 1) * gd].astype(BF16), cs)
        o_ref[0, :, j * gd:(j + 1) * gd] = z[:, :gd].astype(BF16)
        o_ref[1, :, j * gd:(j + 1) * gd] = z[:, gd:].astype(BF16)


def _fnet_time_kernel(f_ref, hcs_ref, wo_ref, bo_ref, x_ref, gate_ref, o_ref):
    f = _dot(f_ref[...], hcs_ref[...])
    y = _dot(f.astype(BF16), wo_ref[...]) + bo_ref[...]
    o_ref[...] = x_ref[...] + gate_ref[...] * y


def _dft_mats(t, gd):
    def cs(n):
        i = jnp.arange(n, dtype=jnp.int32)
        ang = ((i[:, None] * i[None, :]) % n).astype(F32) * (2.0 * jnp.pi / n)
        return jnp.cos(ang), jnp.sin(ang)
    ct, st = cs(t)
    cc, sc = cs(gd)
    scale = 1.0 / jnp.sqrt(jnp.asarray(t * gd, F32))
    return (jnp.concatenate([ct, -st], axis=1).astype(BF16),
            (jnp.concatenate([cc, sc], axis=1) * scale).astype(BF16))


def _fnet_layer(x, norm_g, sc, sh, gate, wo_bf, bo, mats):
    bsz, t, d = x.shape
    f_mat, cs_mat = mats
    gd = d // FNET_GROUPS
    tm = min(t, 512)
    vec = lambda b, i: (b, 0, 0)
    hcs = pl.pallas_call(
        _fnet_chan_kernel,
        out_shape=jax.ShapeDtypeStruct((bsz, 2, t, d), BF16),
        grid=(bsz, t // tm),
        in_specs=[pl.BlockSpec((None, tm, d), lambda b, i: (b, i, 0)),
                  pl.BlockSpec((1, d), lambda b, i: (0, 0)),
                  pl.BlockSpec((None, 1, d), vec),
                  pl.BlockSpec((None, 1, d), vec),
                  pl.BlockSpec((gd, 2 * gd), lambda b, i: (0, 0))],
        out_specs=pl.BlockSpec((None, 2, tm, d), lambda b, i: (b, 0, i, 0)),
        compiler_params=_params("parallel", "parallel"),
        name="fnet_chan",
    )(x, norm_g.reshape(1, d), sc, sh, cs_mat)
    hcs = hcs.reshape(bsz, 2 * t, d)
    tm2 = min(t, 256)
    return pl.pallas_call(
        _fnet_time_kernel,
        out_shape=jax.ShapeDtypeStruct((bsz, t, d), F32),
        grid=(bsz, t // tm2),
        in_specs=[pl.BlockSpec((tm2, 2 * t), lambda b, i: (i, 0)),
                  pl.BlockSpec((None, 2 * t, d), lambda b, i: (b, 0, 0)),
                  pl.BlockSpec((d, d), lambda b, i: (0, 0)),
                  pl.BlockSpec((1, d), lambda b, i: (0, 0)),
                  pl.BlockSpec((None, tm2, d), lambda b, i: (b, i, 0)),
                  pl.BlockSpec((None, 1, d), vec)],
        out_specs=pl.BlockSpec((None, tm2, d), lambda b, i: (b, i, 0)),
        compiler_params=_params("parallel", "parallel"),
        name="fnet_time",
    )(f_mat, hcs, wo_bf, bo.reshape(1, d), x, gate)


def _normmod_kernel(x_ref, g_ref, sc_ref, sh_ref, o_ref):
    o_ref[...] = _norm_mod(x_ref[...], g_ref[...], sc_ref[...], sh_ref[...])


def _normmod(x, norm_g, sc, sh):
    bsz, t, d = x.shape
    tm = min(t, 512)
    vec = lambda b, i: (b, 0, 0)
    return pl.pallas_call(
        _normmod_kernel,
        out_shape=jax.ShapeDtypeStruct((bsz, t, d), F32),
        grid=(bsz, t // tm),
        in_specs=[pl.BlockSpec((None, tm, d), lambda b, i: (b, i, 0)),
                  pl.BlockSpec((1, d), lambda b, i: (0, 0)),
                  pl.BlockSpec((None, 1, d), vec),
                  pl.BlockSpec((None, 1, d), vec)],
        out_specs=pl.BlockSpec((None, tm, d), lambda b, i: (b, i, 0)),
        compiler_params=_params("parallel", "parallel"),
        name="normmod",
    )(x, norm_g.reshape(1, d), sc, sh)


def _qshift_grid(h):
    bsz, t, d = h.shape
    g = h.reshape(bsz, t // GRID_W, GRID_W, d)
    q = d // 4
    left = jnp.pad(g[:, :, :-1, :q], ((0, 0), (0, 0), (1, 0), (0, 0)))
    right = jnp.pad(g[:, :, 1:, q:2 * q], ((0, 0), (0, 0), (0, 1), (0, 0)))
    up = jnp.pad(g[:, :-1, :, 2 * q:3 * q], ((0, 0), (1, 0), (0, 0), (0, 0)))
    down = jnp.pad(g[:, 1:, :, 3 * q:], ((0, 0), (0, 1), (0, 0), (0, 0)))
    return jnp.concatenate([left, right, up, down], axis=-1).reshape(bsz, t, d)


def _shift_seq(h):
    half = h.shape[-1] // 2
    prev = jnp.pad(h[:, :-1, :half], ((0, 0), (1, 0), (0, 0)))
    nxt = jnp.pad(h[:, 1:, half:], ((0, 0), (0, 1), (0, 0)))
    return jnp.concatenate([prev, nxt], axis=-1)


def _rwkv_proj_kernel(*refs, has_vres):
    it = iter(refs)
    h_ref, hs_ref = next(it), next(it)
    vf_ref = next(it) if has_vres else None
    mix_ref, wr_ref, wk_ref, wv_ref = next(it), next(it), next(it), next(it)
    w0_ref, w1_ref, w2_ref = next(it), next(it), next(it)
    a0_ref, a1_ref, a2_ref = next(it), next(it), next(it)
    g1_ref, g2_ref, kk_ref, ka_ref = next(it), next(it), next(it), next(it)
    if has_vres:
        v0_ref, v1_ref, v2_ref = next(it), next(it), next(it)
    r_out, v_out, g_out, a_out = next(it), next(it), next(it), next(it)
    lw_out, kd_out, bd_out = next(it), next(it), next(it)

    h = h_ref[...]
    xx = hs_ref[...] - h
    xr, xw, xk, xv, xa, xg = [(h + xx * mix_ref[j:j + 1, :]).astype(BF16) for j in range(6)]
    r = _dot(xr, wr_ref[...])
    k = _dot(xk, wk_ref[...])
    v = _dot(xv, wv_ref[...])
    if has_vres:
        lora = _dot(_dot(xv, v1_ref[...]).astype(BF16), v2_ref[...])
        v = v + (vf_ref[...] - v) * _sigmoid(v0_ref[...] + lora)
    g = _dot(_sigmoid(_dot(xg, g1_ref[...])).astype(BF16), g2_ref[...])
    ones = _seg_ones(min(LANE_GROUP, h.shape[1]))
    kk = k * kk_ref[...]
    kk = kk * lax.rsqrt(_head_sum(kk * kk, ones) + 1e-12)
    r_out[...] = r.astype(BF16)
    v_out[...] = v
    g_out[...] = g.astype(BF16)
    a_out[...] = (-kk).astype(BF16)
    for d in range(2):
        wlog = w0_ref[d:d + 1, :] + _dot(jnp.tanh(_dot(xw, w1_ref[d])).astype(BF16), w2_ref[d])
        z = -wlog
        softplus = jnp.maximum(z, 0.0) + jnp.log(1.0 + jnp.exp(-jnp.abs(z)))
        lw_out[d] = -jnp.exp(-softplus - 0.5)
        a = _sigmoid(a0_ref[d:d + 1, :] + _dot(_dot(xa, a1_ref[d]).astype(BF16), a2_ref[d]))
        kd_out[d] = (k * (1.0 + (a - 1.0) * ka_ref[...])).astype(BF16)
        bd_out[d] = (kk * a).astype(BF16)


def _rwkv_project(h, hs, vfirst, p):
    bsz, t, d = h.shape
    tm = _tile(256, t)
    has_vres = vfirst is not None
    tok = pl.BlockSpec((None, tm, d), lambda b, i: (b, i, 0))
    tok2 = pl.BlockSpec((2, None, tm, d), lambda b, i: (0, b, i, 0))

    def full(a):
        nd = a.ndim
        return pl.BlockSpec(a.shape, lambda b, i: (0,) * nd)

    bf = lambda a: a.astype(BF16)
    weights = [p["mix"], bf(p["wr"]), bf(p["wk"]), bf(p["wv"]),
               p["w0"], bf(p["w1"]), bf(p["w2"]), p["a0"], bf(p["a1"]), bf(p["a2"]),
               bf(p["g1"]), bf(p["g2"]), p["kk"].reshape(1, d), p["ka"].reshape(1, d)]
    if has_vres:
        weights += [p["v0"].reshape(1, d), bf(p["v1"]), bf(p["v2"])]
    acts = [h, hs] + ([vfirst] if has_vres else [])
    out_shape = [jax.ShapeDtypeStruct((bsz, t, d), BF16),
                 jax.ShapeDtypeStruct((bsz, t, d), F32),
                 jax.ShapeDtypeStruct((bsz, t, d), BF16),
                 jax.ShapeDtypeStruct((bsz, t, d), BF16),
                 jax.ShapeDtypeStruct((2, bsz, t, d), F32),
                 jax.ShapeDtypeStruct((2, bsz, t, d), BF16),
                 jax.ShapeDtypeStruct((2, bsz, t, d), BF16)]
    return pl.pallas_call(
        functools.partial(_rwkv_proj_kernel, has_vres=has_vres),
        out_shape=out_shape,
        grid=(bsz, t // tm),
        in_specs=[tok] * len(acts) + [full(w) for w in weights],
        out_specs=[tok, tok, tok, tok, tok2, tok2, tok2],
        compiler_params=_params("parallel", "parallel"),
        name="rwkv_proj",
    )(*acts, *weights)


def _wkv_kernel(r_ref, lw_ref, k_ref, v_ref, a_ref, b_ref, y_ref, s_scr, *, reverse):
    L = WKV_CHUNK
    d = r_ref.shape[-1]
    gw = s_scr.shape[-1]
    hpg = gw // HEAD_SIZE

    @pl.when(pl.program_id(1) == 0)
    def _():
        s_scr[...] = jnp.zeros_like(s_scr)

    ti = lax.broadcasted_iota(jnp.int32, (L, L), 0)
    si = lax.broadcasted_iota(jnp.int32, (L, L), 1)
    tri = jnp.where((si >= ti) if reverse else (si <= ti), 1.0, 0.0).astype(BF16)
    lw = lw_ref[...]
    p1, p2, p3 = _split3(lw)
    cw = _dot(tri, p1) + _dot(tri, p2) + _dot(tri, p3)
    last, mid = (0, L // 2) if reverse else (L - 1, L // 2 - 1)
    cw_end = cw[last:last + 1, :]
    cw_mid = cw[mid:mid + 1, :]

    r = r_ref[...].astype(F32)
    k = k_ref[...].astype(F32)
    a = a_ref[...].astype(F32)
    b = b_ref[...].astype(F32)
    v = v_ref[...]
    e_in = jnp.exp(cw)
    e_ex = jnp.exp(cw - lw)
    e_inv = jnp.exp(cw_mid - cw)
    e_mid = jnp.exp(-cw_mid)
    e_end = e_inv * jnp.exp(cw_end - cw_mid)
    r0 = r * e_in
    a0 = a * e_ex
    ra = r0 * e_mid
    aa = a0 * e_mid
    kb = k * e_inv
    bb = b * e_inv
    ke = k * e_end
    be = b * e_end

    trow = lax.broadcasted_iota(jnp.int32, (L, gw), 0)
    scol = lax.broadcasted_iota(jnp.int32, (L, gw), 1) % HEAD_SIZE
    strict = (scol > trow) if reverse else (scol < trow)
    incl = (scol >= trow) if reverse else (scol <= trow)
    eye = jnp.where(scol == trow, 1.0, 0.0)
    bdm = (lax.broadcasted_iota(jnp.int32, (gw, gw), 0) // HEAD_SIZE
           == lax.broadcasted_iota(jnp.int32, (gw, gw), 1) // HEAD_SIZE)

    def bdiag(z):
        return jnp.where(bdm, jnp.concatenate([z] * hpg, axis=0), 0.0).astype(BF16)

    for g in range(d // gw):
        sl = slice(g * gw, (g + 1) * gw)
        x = jnp.concatenate([aa[:, sl], ra[:, sl]], axis=0).astype(BF16)
        gb = _dot_nt(x, bdiag(bb[:, sl]))
        gk = _dot_nt(x, bdiag(kb[:, sl]))
        a_ab = jnp.where(strict, gb[:L], 0.0)
        tm = eye + a_ab
        pw = _dot(a_ab.astype(BF16), bdiag(a_ab))
        n_lvl = L.bit_length() - 2
        for lvl in range(n_lvl):
            if lvl + 1 < n_lvl:
                z = _dot(jnp.concatenate([tm, pw], axis=0).astype(BF16), bdiag(pw))
                tm = tm + z[:L]
                pw = z[L:]
            else:
                tm = tm + _dot(tm.astype(BF16), bdiag(pw))
        s_old = s_scr[g]
        x0 = jnp.concatenate([a0[:, sl], r0[:, sl]], axis=0).astype(BF16)
        xs = _dot(x0, s_old.astype(BF16))
        vg = v[:, sl]
        vbd = bdiag(vg)
        a_ak = jnp.where(strict, gk[:L], 0.0)
        rhs = xs[:L] + _dot(a_ak.astype(BF16), vbd)
        u = _dot(tm.astype(BF16), bdiag(rhs))
        r_b = jnp.where(incl, gb[L:], 0.0)
        r_k = jnp.where(incl, gk[L:], 0.0)
        y = xs[L:] + _dot(r_b.astype(BF16), bdiag(u)) + _dot(r_k.astype(BF16), vbd)
        y_ref[:, sl] = y
        ek = jnp.concatenate([be[:, sl], ke[:, sl]], axis=0)
        uv = jnp.concatenate([u, vg], axis=0).astype(BF16)
        upd = _dot(ek.T.astype(BF16), uv)
        lwt = jnp.concatenate([lw[:, sl], jnp.zeros_like(lw[:, sl])], axis=0).T
        wcol = jnp.exp(jnp.sum(lwt, axis=1, keepdims=True))
        s_scr[g] = jnp.where(bdm, wcol * s_old + upd, 0.0)


def _wkv_scan(r, lw, k, v, a, b, d_idx, n_ctx_chunks, reverse):
    bsz, t, d = r.shape
    L = WKV_CHUNK
    nc = t // L
    gw = min(LANE_GROUP, d)
    if reverse:
        cidx = lambda c: jnp.where(c < n_ctx_chunks, n_ctx_chunks - 1 - c, nc + n_ctx_chunks - 1 - c)
    else:
        cidx = lambda c: c
    tok = pl.BlockSpec((None, L, d), lambda bb, c: (bb, cidx(c), 0))
    tokd = pl.BlockSpec((None, None, L, d), lambda bb, c: (d_idx, bb, cidx(c), 0))
    return pl.pallas_call(
        functools.partial(_wkv_kernel, reverse=reverse),
        out_shape=jax.ShapeDtypeStruct((bsz, t, d), F32),
        grid=(bsz, nc),
        in_specs=[tok, tokd, tokd, tok, tok, tokd],
        out_specs=tok,
        scratch_shapes=[pltpu.VMEM((d // gw, gw, gw), F32)],
        compiler_params=_params("parallel", "arbitrary"),
        name="wkv_rev" if reverse else "wkv_fwd",
    )(r, lw, k, v, a, b)


def _rwkv_out_kernel(y0_ref, y1_ref, r_ref, k_ref, v_ref, g_ref, rk_ref, lnw_ref, lnb_ref,
                     wo_ref, x_ref, gate_ref, o_ref):
    y = y0_ref[...] + y1_ref[...]
    ones = _seg_ones(min(LANE_GROUP, y.shape[1]))
    inv_n = 1.0 / HEAD_SIZE
    mu = _head_sum(y, ones) * inv_n
    yc = y - mu
    var = _head_sum(yc * yc, ones) * inv_n
    yn = yc * lax.rsqrt(var + LNX_EPS) * lnw_ref[...] + lnb_ref[...]
    ksum = k_ref[0].astype(F32) + k_ref[1].astype(F32)
    bonus = _head_sum(r_ref[...].astype(F32) * ksum * rk_ref[...], ones) * v_ref[...]
    out = ((yn + bonus) * g_ref[...].astype(F32)).astype(BF16)
    o_ref[...] = x_ref[...] + gate_ref[...] * _dot(out, wo_ref[...])


def _rwkv_output(y0, y1, r, kd, v, g, rk, lnw, lnb, wo_bf, x, gate, t_off):
    bsz, t, d = x.shape
    tm = _tile(256, t, t_off)
    off = t_off // tm
    tok = pl.BlockSpec((None, tm, d), lambda b, i: (b, i + off, 0))
    tok2 = pl.BlockSpec((2, None, tm, d), lambda b, i: (0, b, i + off, 0))
    row = pl.BlockSpec((1, d), lambda b, i: (0, 0))
    return pl.pallas_call(
        _rwkv_out_kernel,
        out_shape=jax.ShapeDtypeStruct((bsz, t, d), F32),
        grid=(bsz, t // tm),
        in_specs=[tok, tok, tok, tok2, tok, tok, row, row, row,
                  pl.BlockSpec((d, d), lambda b, i: (0, 0)),
                  pl.BlockSpec((None, tm, d), lambda b, i: (b, i, 0)),
                  pl.BlockSpec((None, 1, d), lambda b, i: (b, 0, 0))],
        out_specs=pl.BlockSpec((None, tm, d), lambda b, i: (b, i, 0)),
        compiler_params=_params("parallel", "parallel"),
        name="rwkv_out",
    )(y0, y1, r, kd, v, g, rk.reshape(1, d), lnw.reshape(1, d), lnb.reshape(1, d), wo_bf, x, gate)


def _moe_route_kernel(x_ref, g_ref, sc_ref, sh_ref, rt_ref, xs_ref, gs_ref, code_ref,
                      h_scr, aff_scr, *, cap):
    t, d = x_ref.shape
    n_e = rt_ref.shape[0]

    @pl.when(pl.program_id(1) == 0)
    def _():
        g = g_ref[...]
        sc = sc_ref[...]
        sh = sh_ref[...]
        rt_hi, rt_lo = _split2(rt_ref[...])
        tc = min(t, 256)
        for j in range(t // tc):
            h = _norm_mod(x_ref[j * tc:(j + 1) * tc, :], g, sc, sh)
            h_hi, h_lo = _split2(h)
            h_scr[j * tc:(j + 1) * tc, :] = h_hi
            aff_scr[:, j * tc:(j + 1) * tc] = (_dot_nt(rt_hi, h_hi) + _dot_nt(rt_hi, h_lo)
                                               + _dot_nt(rt_lo, h_hi))
        logits = aff_scr[...]
        m = jnp.max(logits, axis=0, keepdims=True)
        ex = jnp.exp(logits - m)
        aff = ex / jnp.sum(ex, axis=0, keepdims=True)
        aff_scr[...] = aff
        bits = pltpu.bitcast(aff, jnp.int32)

        def search(i, thr):
            cand = thr | (jnp.int32(1) << (30 - i))
            cnt = jnp.sum(jnp.where(bits >= cand, 1, 0), axis=1, keepdims=True)
            return jnp.where(cnt >= cap, cand, thr)

        thr = lax.fori_loop(0, 31, search, jnp.zeros((n_e, 1), jnp.int32))
        gt = bits > thr
        eq = bits == thr
        key = jnp.where(gt, 1, 0) + jnp.where(eq, 4096, 0)
        lane = lax.broadcasted_iota(jnp.int32, (n_e, t), 1)
        csum = key
        sh_amt = 1
        while sh_amt < t:
            csum = csum + jnp.where(lane >= sh_amt, pltpu.roll(csum, sh_amt, 1), 0)
            sh_amt *= 2
        before = csum - key
        n_gt = before & 4095
        n_eq = before >> 12
        need = cap - jnp.sum(jnp.where(gt, 1, 0), axis=1, keepdims=True)
        sel = gt | (eq & (n_eq < need))
        code_ref[...] = jnp.where(sel, n_gt + jnp.minimum(n_eq, need), -1)

    e = pl.program_id(1)
    code = code_ref[pl.ds(e, 1), :]
    aff_e = aff_scr[pl.ds(e, 1), :]
    slot = lax.broadcasted_iota(jnp.int32, (cap, t), 0)
    hit = code == slot
    xs_ref[...] = _dot(jnp.where(hit, 1.0, 0.0).astype(BF16), h_scr[...]).astype(BF16)
    gs_ref[...] = jnp.sum(jnp.where(hit, aff_e, 0.0), axis=1, keepdims=True)


def _moe_route(x, norm_g, sc, sh, router):
    bsz, t, d = x.shape
    n_e = router.shape[1]
    cap = CAPACITY_FACTOR * t // n_e
    vec = lambda b, e: (b, 0, 0)
    return pl.pallas_call(
        functools.partial(_moe_route_kernel, cap=cap),
        out_shape=[jax.ShapeDtypeStruct((bsz, n_e, cap, d), BF16),
                   jax.ShapeDtypeStruct((bsz, n_e, cap, 1), F32),
                   jax.ShapeDtypeStruct((bsz, n_e, t), jnp.int32)],
        grid=(bsz, n_e),
        in_specs=[pl.BlockSpec((None, t, d), lambda b, e: (b, 0, 0)),
                  pl.BlockSpec((1, d), lambda b, e: (0, 0)),
                  pl.BlockSpec((None, 1, d), vec),
                  pl.BlockSpec((None, 1, d), vec),
                  pl.BlockSpec((n_e, d), lambda b, e: (0, 0))],
        out_specs=[pl.BlockSpec((None, None, cap, d), lambda b, e: (b, e, 0, 0)),
                   pl.BlockSpec((None, None, cap, 1), lambda b, e: (b, e, 0, 0)),
                   pl.BlockSpec((None, n_e, t), lambda b, e: (b, 0, 0))],
        scratch_shapes=[pltpu.VMEM((t, d), BF16), pltpu.VMEM((n_e, t), F32)],
        compiler_params=_params("parallel", "arbitrary"),
        name="moe_route",
    )(x, norm_g.reshape(1, d), sc, sh, router.T)


def _moe_ffn_kernel(xs_ref, gs_ref, wg_ref, wu_ref, wd_ref, ys_ref, wg_scr, wu_scr, wd_scr):
    @pl.when(pl.program_id(1) == 0)
    def _():
        wg_scr[...] = wg_ref[...].astype(BF16)
        wu_scr[...] = wu_ref[...].astype(BF16)
        wd_scr[...] = wd_ref[...].astype(BF16)

    tb, cap, d = xs_ref.shape
    xs = xs_ref[...].reshape(tb * cap, d)
    hg = _dot(xs, wg_scr[...])
    hu = _dot(xs, wu_scr[...])
    hid = (hg * _sigmoid(hg) * hu).astype(BF16)
    ys = _dot(hid, wd_scr[...]) * gs_ref[...].reshape(tb * cap, 1)
    ys_ref[...] = ys.reshape(tb, cap, d).astype(BF16)


def _moe_ffn(xs, gs, wg, wu, wd):
    bsz, n_e, cap, d = xs.shape
    f = wg.shape[-1]
    tb = max(1, min(bsz, 512 // cap))
    while bsz % tb:
        tb -= 1
    return pl.pallas_call(
        _moe_ffn_kernel,
        out_shape=jax.ShapeDtypeStruct((bsz, n_e, cap, d), BF16),
        grid=(n_e, bsz // tb),
        in_specs=[pl.BlockSpec((tb, None, cap, d), lambda e, j: (j, e, 0, 0)),
                  pl.BlockSpec((tb, None, cap, 1), lambda e, j: (j, e, 0, 0)),
                  pl.BlockSpec((None, d, f), lambda e, j: (e, 0, 0)),
                  pl.BlockSpec((None, d, f), lambda e, j: (e, 0, 0)),
                  pl.BlockSpec((None, f, d), lambda e, j: (e, 0, 0))],
        out_specs=pl.BlockSpec((tb, None, cap, d), lambda e, j: (j, e, 0, 0)),
        scratch_shapes=[pltpu.VMEM((d, f), BF16), pltpu.VMEM((d, f), BF16), pltpu.VMEM((f, d), BF16)],
        compiler_params=_params("parallel", "arbitrary"),
        name="moe_ffn",
    )(xs, gs, wg, wu, wd)


def _moe_combine_kernel(code_ref, ys_ref, x_ref, gate_ref, fg_ref, o_ref, *, cap, final_norm):
    n_e = code_ref.shape[1]
    slot = lax.broadcasted_iota(jnp.int32, (1, cap), 1)
    pieces = [jnp.where(code_ref[:, e:e + 1] == slot, 1.0, 0.0).astype(BF16) for e in range(n_e)]
    scat = jnp.concatenate(pieces, axis=1)
    x = x_ref[...] + gate_ref[...] * _dot(scat, ys_ref[...])
    if final_norm:
        x = x * lax.rsqrt(jnp.mean(x * x, axis=-1, keepdims=True) + NORM_EPS) * fg_ref[...]
    o_ref[...] = x


def _moe_combine(code_t, ys, x, gate, final_g, final_norm):
    bsz, t, d = x.shape
    n_e = code_t.shape[-1]
    cap = ys.shape[1] // n_e
    tm = min(t, 512)
    return pl.pallas_call(
        functools.partial(_moe_combine_kernel, cap=cap, final_norm=final_norm),
        out_shape=jax.ShapeDtypeStruct((bsz, t, d), F32),
        grid=(bsz, t // tm),
        in_specs=[pl.BlockSpec((None, tm, n_e), lambda b, i: (b, i, 0)),
                  pl.BlockSpec((None, n_e * cap, d), lambda b, i: (b, 0, 0)),
                  pl.BlockSpec((None, tm, d), lambda b, i: (b, i, 0)),
                  pl.BlockSpec((None, 1, d), lambda b, i: (b, 0, 0)),
                  pl.BlockSpec((1, d), lambda b, i: (0, 0))],
        out_specs=pl.BlockSpec((None, tm, d), lambda b, i: (b, i, 0)),
        compiler_params=_params("parallel", "parallel"),
        name="moe_combine",
    )(code_t, ys, x, gate, final_g.reshape(1, d))


def _ec_moe_layer(x, norm_g, sc, sh, gate, router, wg, wu, wd, final_g, final_norm):
    bsz, t, d = x.shape
    xs, gs, code = _moe_route(x, norm_g, sc, sh, router)
    ys = _moe_ffn(xs, gs, wg, wu, wd)
    n_e, cap = xs.shape[1], xs.shape[2]
    return _moe_combine(jnp.swapaxes(code, 1, 2), ys.reshape(bsz, n_e * cap, d), x, gate,
                        final_g, final_norm)


def kernel(x, c, ctx, c_ctx, ada_w, ada_b, norm_g, fnet_wo, fnet_bo, rw_mix, rw_wr, rw_wk, rw_wv, rw_wo,
           rw_w0, rw_w1, rw_w2, rw_a0, rw_a1, rw_a2, rw_v0, rw_v1, rw_v2, rw_g1, rw_g2, rw_kk, rw_ka, rw_rk,
           rw_lnx_w, rw_lnx_b, moe_router, moe_wg, moe_wu, moe_wd, final_g):
    bsz, t, d = x.shape
    t_ctx = ctx.shape[1]
    depth = ada_w.shape[0]
    n_mixers = 2

    rows = -(-(bsz + 1) // 8) * 8
    cc = jnp.concatenate([c, c_ctx[None, :], jnp.zeros((rows - bsz - 1, d), F32)], axis=0)
    mods = _adaln(cc, ada_w, ada_b)

    def mod_x(i, j):
        return mods[i, :bsz, j * d:(j + 1) * d].reshape(bsz, 1, d)

    def mod_c(i, j):
        return jnp.broadcast_to(mods[i, bsz, j * d:(j + 1) * d].reshape(1, 1, d), (bsz, 1, d))

    mats_x = _dft_mats(t, d // FNET_GROUPS)
    mats_c = _dft_mats(t_ctx, d // FNET_GROUPS)
    vfirst = None
    for i in range(depth):
        need_ctx = i < depth - 1
        if i % n_mixers == 0:
            fi = i // n_mixers
            wo_bf = fnet_wo[fi].astype(BF16)
            x = _fnet_layer(x, norm_g[i, 0], mod_x(i, 1), mod_x(i, 0), mod_x(i, 2), wo_bf, fnet_bo[fi], mats_x)
            if need_ctx:
                ctx = _fnet_layer(ctx, norm_g[i, 0], mod_c(i, 1), mod_c(i, 0), mod_c(i, 2), wo_bf,
                                  fnet_bo[fi], mats_c)
        else:
            ri = i // n_mixers
            hx = _normmod(x, norm_g[i, 0], mod_x(i, 1), mod_x(i, 0))
            hc = _normmod(ctx, norm_g[i, 0], mod_c(i, 1), mod_c(i, 0))
            h = jnp.concatenate([hc, hx], axis=1)
            hs = jnp.concatenate([_shift_seq(hc), _qshift_grid(hx)], axis=1)
            p = dict(mix=rw_mix[ri], wr=rw_wr[ri], wk=rw_wk[ri], wv=rw_wv[ri], w0=rw_w0[ri], w1=rw_w1[ri],
                     w2=rw_w2[ri], a0=rw_a0[ri], a1=rw_a1[ri], a2=rw_a2[ri], g1=rw_g1[ri], g2=rw_g2[ri],
                     kk=rw_kk[ri], ka=rw_ka[ri])
            if ri > 0:
                p.update(v0=rw_v0[ri - 1], v1=rw_v1[ri - 1], v2=rw_v2[ri - 1])
            r, v, g, a, lw, kd, bd = _rwkv_project(h, hs, vfirst if ri > 0 else None, p)
            if ri == 0:
                vfirst = v
            ncc = t_ctx // WKV_CHUNK
            y0 = _wkv_scan(r, lw, kd, v, a, bd, 0, ncc, False)
            y1 = _wkv_scan(r, lw, kd, v, a, bd, 1, ncc, True)
            wo_bf = rw_wo[ri].astype(BF16)
            args = (y0, y1, r, kd, v, g, rw_rk[ri], rw_lnx_w[ri], rw_lnx_b[ri], wo_bf)
            x = _rwkv_output(*args, x, mod_x(i, 2), t_ctx)
            if need_ctx:
                ctx = _rwkv_output(*args, ctx, mod_c(i, 2), 0)
        last = i == depth - 1
        x = _ec_moe_layer(x, norm_g[i, 1], mod_x(i, 4), mod_x(i, 3), mod_x(i, 5), moe_router[i],
                          moe_wg[i], moe_wu[i], moe_wd[i], final_g, last)
        if need_ctx:
            ctx = _ec_moe_layer(ctx, norm_g[i, 1], mod_c(i, 4), mod_c(i, 3), mod_c(i, 5), moe_router[i],
                                moe_wg[i], moe_wu[i], moe_wd[i], final_g, False)
    return x
```

```python
import functools

import jax
import jax.numpy as jnp
from jax import lax
from jax.experimental import pallas as pl
from jax.experimental.pallas import tpu as pltpu

F32 = jnp.float32
BF16 = jnp.bfloat16

HEAD_SIZE = 64
LANE_GROUP = 256
GRID_W = 64
FNET_GROUPS = 4
CAPACITY_FACTOR = 2
NORM_EPS = 1e-6
LNX_EPS = 64e-5
WKV_CHUNK = 64
WKV_BLOCK = 256
VMEM_LIMIT_BYTES = 56 * 1024 * 1024


def _params(*semantics):
    return pltpu.CompilerParams(dimension_semantics=semantics, vmem_limit_bytes=VMEM_LIMIT_BYTES)


def _tile(pref, *extents):
    tm = pref
    while any(n % tm for n in extents if n):
        tm //= 2
    return tm


def _dot(a, b):
    return jnp.dot(a, b, preferred_element_type=F32)


def _dot_nt(a, b):
    return lax.dot_general(a, b, (((1,), (1,)), ((), ())), preferred_element_type=F32)


def _dot_tn(a, b):
    return lax.dot_general(a, b, (((0,), (0,)), ((), ())), preferred_element_type=F32)


def _split2(x):
    hi = x.astype(BF16)
    lo = (x - hi.astype(F32)).astype(BF16)
    return hi, lo


def _split3(x):
    hi = x.astype(BF16)
    r1 = x - hi.astype(F32)
    mid = r1.astype(BF16)
    lo = (r1 - mid.astype(F32)).astype(BF16)
    return hi, mid, lo


def _norm_mod(x, g, sc, sh):
    ms = jnp.mean(x * x, axis=-1, keepdims=True)
    return x * lax.rsqrt(ms + NORM_EPS) * g * (1.0 + sc) + sh


def _sigmoid(x):
    return 1.0 / (1.0 + jnp.exp(-x))


def _seg_ones(n):
    r = lax.broadcasted_iota(jnp.int32, (n, n), 0) // HEAD_SIZE
    c = lax.broadcasted_iota(jnp.int32, (n, n), 1) // HEAD_SIZE
    return jnp.where(r == c, 1.0, 0.0).astype(BF16)


def _head_sum(x, ones):
    d = x.shape[-1]
    w = ones.shape[0]
    parts = []
    for j in range(d // w):
        hi, lo = _split2(x[:, j * w:(j + 1) * w])
        parts.append(_dot(hi, ones) + _dot(lo, ones))
    return parts[0] if len(parts) == 1 else jnp.concatenate(parts, axis=1)


def _adaln_kernel(c_ref, w_ref, b_ref, o_ref):
    c = c_ref[...]
    s = c * _sigmoid(c)
    s_hi, s_lo = _split2(s)
    w_hi, w_lo = _split2(w_ref[...])
    o_ref[...] = _dot(s_hi, w_hi) + _dot(s_lo, w_hi) + _dot(s_hi, w_lo) + b_ref[...]


def _adaln(cc, ada_w, ada_b):
    depth, d, n = ada_w.shape
    rows = cc.shape[0]
    tn = min(n, 1536)
    return pl.pallas_call(
        _adaln_kernel,
        out_shape=jax.ShapeDtypeStruct((depth, rows, n), F32),
        grid=(depth, n // tn),
        in_specs=[pl.BlockSpec((rows, d), lambda l, j: (0, 0)),
                  pl.BlockSpec((None, d, tn), lambda l, j: (l, 0, j)),
                  pl.BlockSpec((None, 1, tn), lambda l, j: (l, 0, j))],
        out_specs=pl.BlockSpec((None, rows, tn), lambda l, j: (l, 0, j)),
        compiler_params=_params("parallel", "parallel"),
        name="adaln",
    )(cc, ada_w, ada_b.reshape(depth, 1, n))


def _fnet_chan_kernel(x_ref, g_ref, sc_ref, sh_ref, cs_ref, o_ref):
    h = _norm_mod(x_ref[...], g_ref[...], sc_ref[...], sh_ref[...])
    gd = cs_ref.shape[0]
    cs = cs_ref[...]
    for j in range(h.shape[1] // gd):
        z = _dot(h[:, j * gd:(j + 1) * gd].astype(BF16), cs)
        o_ref[0, :, j * gd:(j + 1) * gd] = z[:, :gd].astype(BF16)
        o_ref[1, :, j * gd:(j + 1) * gd] = z[:, gd:].astype(BF16)


def _fnet_time_kernel(f_ref, hcs_ref, wo_ref, bo_ref, x_ref, gate_ref, o_ref):
    f = _dot(f_ref[...], hcs_ref[...])
    y = _dot(f.astype(BF16), wo_ref[...]) + bo_ref[...]
    o_ref[...] = x_ref[...] + gate_ref[...] * y


def _dft_mats(t, gd):
    def cs(n):
        i = jnp.arange(n, dtype=jnp.int32)
        ang = ((i[:, None] * i[None, :]) % n).astype(F32) * (2.0 * jnp.pi / n)
        return jnp.cos(ang), jnp.sin(ang)
    ct, st = cs(t)
    cc, sc = cs(gd)
    scale = 1.0 / jnp.sqrt(jnp.asarray(t * gd, F32))
    return (jnp.concatenate([ct, -st], axis=1).astype(BF16),
            (jnp.concatenate([cc, sc], axis=1) * scale).astype(BF16))


def _fnet_layer(x, norm_g, sc, sh, gate, wo_bf, bo, mats):
    bsz, t, d = x.shape
    f_mat, cs_mat = mats
    gd = d // FNET_GROUPS
    tm = min(t, 512)
    vec = lambda b, i: (b, 0, 0)
    hcs = pl.pallas_call(
        _fnet_chan_kernel,
        out_shape=jax.ShapeDtypeStruct((bsz, 2, t, d), BF16),
        grid=(bsz, t // tm),
        in_specs=[pl.BlockSpec((None, tm, d), lambda b, i: (b, i, 0)),
                  pl.BlockSpec((1, d), lambda b, i: (0, 0)),
                  pl.BlockSpec((None, 1, d), vec),
                  pl.BlockSpec((None, 1, d), vec),
                  pl.BlockSpec((gd, 2 * gd), lambda b, i: (0, 0))],
        out_specs=pl.BlockSpec((None, 2, tm, d), lambda b, i: (b, 0, i, 0)),
        compiler_params=_params("parallel", "parallel"),
        name="fnet_chan",
    )(x, norm_g.reshape(1, d), sc, sh, cs_mat)
    hcs = hcs.reshape(bsz, 2 * t, d)
    tm2 = min(t, 256)
    return pl.pallas_call(
        _fnet_time_kernel,
        out_shape=jax.ShapeDtypeStruct((bsz, t, d), F32),
        grid=(bsz, t // tm2),
        in_specs=[pl.BlockSpec((tm2, 2 * t), lambda b, i: (i, 0)),
                  pl.BlockSpec((None, 2 * t, d), lambda b, i: (b, 0, 0)),
                  pl.BlockSpec((d, d), lambda b, i: (0, 0)),
                  pl.BlockSpec((1, d), lambda b, i: (0, 0)),
                  pl.BlockSpec((None, tm2, d), lambda b, i: (b, i, 0)),
                  pl.BlockSpec((None, 1, d), vec)],
        out_specs=pl.BlockSpec((None, tm2, d), lambda b, i: (b, i, 0)),
        compiler_params=_params("parallel", "parallel"),
        name="fnet_time",
    )(f_mat, hcs, wo_bf, bo.reshape(1, d), x, gate)


def _normmod_kernel(x_ref, g_ref, sc_ref, sh_ref, o_ref):
    o_ref[...] = _norm_mod(x_ref[...], g_ref[...], sc_ref[...], sh_ref[...])


def _normmod(x, norm_g, sc, sh):
    bsz, t, d = x.shape
    tm = min(t, 512)
    vec = lambda b, i: (b, 0, 0)
    return pl.pallas_call(
        _normmod_kernel,
        out_shape=jax.ShapeDtypeStruct((bsz, t, d), F32),
        grid=(bsz, t // tm),
        in_specs=[pl.BlockSpec((None, tm, d), lambda b, i: (b, i, 0)),
                  pl.BlockSpec((1, d), lambda b, i: (0, 0)),
                  pl.BlockSpec((None, 1, d), vec),
                  pl.BlockSpec((None, 1, d), vec)],
        out_specs=pl.BlockSpec((None, tm, d), lambda b, i: (b, i, 0)),
        compiler_params=_params("parallel", "parallel"),
        name="normmod",
    )(x, norm_g.reshape(1, d), sc, sh)


def _qshift_grid(h):
    bsz, t, d = h.shape
    g = h.reshape(bsz, t // GRID_W, GRID_W, d)
    q = d // 4
    left = jnp.pad(g[:, :, :-1, :q], ((0, 0), (0, 0), (1, 0), (0, 0)))
    right = jnp.pad(g[:, :, 1:, q:2 * q], ((0, 0), (0, 0), (0, 1), (0, 0)))
    up = jnp.pad(g[:, :-1, :, 2 * q:3 * q], ((0, 0), (1, 0), (0, 0), (0, 0)))
    down = jnp.pad(g[:, 1:, :, 3 * q:], ((0, 0), (0, 1), (0, 0), (0, 0)))
    return jnp.concatenate([left, right, up, down], axis=-1).reshape(bsz, t, d)


def _shift_seq(h):
    half = h.shape[-1] // 2
    prev = jnp.pad(h[:, :-1, :half], ((0, 0), (1, 0), (0, 0)))
    nxt = jnp.pad(h[:, 1:, half:], ((0, 0), (0, 1), (0, 0)))
    return jnp.concatenate([prev, nxt], axis=-1)


def _rwkv_proj_kernel(*refs, has_vres):
    it = iter(refs)
    h_ref, hs_ref = next(it), next(it)
    vf_ref = next(it) if has_vres else None
    mix_ref, wr_ref, wk_ref, wv_ref = next(it), next(it), next(it), next(it)
    w0_ref, w1_ref, w2_ref = next(it), next(it), next(it)
    a0_ref, a1_ref, a2_ref = next(it), next(it), next(it)
    g1_ref, g2_ref, kk_ref, ka_ref = next(it), next(it), next(it), next(it)
    if has_vres:
        v0_ref, v1_ref, v2_ref = next(it), next(it), next(it)
    r_out, v_out, g_out, a_out = next(it), next(it), next(it), next(it)
    lw_out, kd_out, bd_out = next(it), next(it), next(it)

    h = h_ref[...]
    xx = hs_ref[...] - h
    xr, xw, xk, xv, xa, xg = [(h + xx * mix_ref[j:j + 1, :]).astype(BF16) for j in range(6)]
    r = _dot(xr, wr_ref[...])
    k = _dot(xk, wk_ref[...])
    v = _dot(xv, wv_ref[...])
    if has_vres:
        lora = _dot(_dot(xv, v1_ref[...]).astype(BF16), v2_ref[...])
        v = v + (vf_ref[...] - v) * _sigmoid(v0_ref[...] + lora)
    g = _dot(_sigmoid(_dot(xg, g1_ref[...])).astype(BF16), g2_ref[...])
    ones = _seg_ones(min(LANE_GROUP, h.shape[1]))
    kk = k * kk_ref[...]
    kk = kk * lax.rsqrt(_head_sum(kk * kk, ones) + 1e-12)
    r_out[...] = r.astype(BF16)
    v_out[...] = v
    g_out[...] = g.astype(BF16)
    a_out[...] = (-kk).astype(BF16)
    for d in range(2):
        wlog = w0_ref[d:d + 1, :] + _dot(jnp.tanh(_dot(xw, w1_ref[d])).astype(BF16), w2_ref[d])
        z = -wlog
        softplus = jnp.maximum(z, 0.0) + jnp.log(1.0 + jnp.exp(-jnp.abs(z)))
        lw_out[d] = -jnp.exp(-softplus - 0.5)
        a = _sigmoid(a0_ref[d:d + 1, :] + _dot(_dot(xa, a1_ref[d]).astype(BF16), a2_ref[d]))
        kd_out[d] = (k * (1.0 + (a - 1.0) * ka_ref[...])).astype(BF16)
        bd_out[d] = (kk * a).astype(BF16)


def _rwkv_project(h, hs, vfirst, p):
    bsz, t, d = h.shape
    tm = _tile(256, t)
    has_vres = vfirst is not None
    tok = pl.BlockSpec((None, tm, d), lambda b, i: (b, i, 0))
    tok2 = pl.BlockSpec((2, None, tm, d), lambda b, i: (0, b, i, 0))

    def full(a):
        nd = a.ndim
        return pl.BlockSpec(a.shape, lambda b, i: (0,) * nd)

    bf = lambda a: a.astype(BF16)
    weights = [p["mix"], bf(p["wr"]), bf(p["wk"]), bf(p["wv"]),
               p["w0"], bf(p["w1"]), bf(p["w2"]), p["a0"], bf(p["a1"]), bf(p["a2"]),
               bf(p["g1"]), bf(p["g2"]), p["kk"].reshape(1, d), p["ka"].reshape(1, d)]
    if has_vres:
        weights += [p["v0"].reshape(1, d), bf(p["v1"]), bf(p["v2"])]
    acts = [h, hs] + ([vfirst] if has_vres else [])
    out_shape = [jax.ShapeDtypeStruct((bsz, t, d), BF16),
                 jax.ShapeDtypeStruct((bsz, t, d), F32),
                 jax.ShapeDtypeStruct((bsz, t, d), BF16),
                 jax.ShapeDtypeStruct((bsz, t, d), BF16),
                 jax.ShapeDtypeStruct((2, bsz, t, d), F32),
                 jax.ShapeDtypeStruct((2, bsz, t, d), BF16),
                 jax.ShapeDtypeStruct((2, bsz, t, d), BF16)]
    return pl.pallas_call(
        functools.partial(_rwkv_proj_kernel, has_vres=has_vres),
        out_shape=out_shape,
        grid=(bsz, t // tm),
        in_specs=[tok] * len(acts) + [full(w) for w in weights],
        out_specs=[tok, tok, tok, tok, tok2, tok2, tok2],
        compiler_params=_params("parallel", "parallel"),
        name="rwkv_proj",
    )(*acts, *weights)


def _wkv_kernel(r_ref, lw_ref, k_ref, v_ref, a_ref, b_ref, y_ref, s_scr, lhs_scr, n_scr, wl_scr, *, reverse):
    L = WKV_CHUNK
    tb, d = r_ref.shape
    nch = tb // L
    gw = s_scr.shape[-1]
    hpg = gw // HEAD_SIZE
    ng = d // gw

    @pl.when(pl.program_id(1) == 0)
    def _():
        s_scr[...] = jnp.zeros_like(s_scr)

    ti = lax.broadcasted_iota(jnp.int32, (L, L), 0)
    si = lax.broadcasted_iota(jnp.int32, (L, L), 1)
    tri = jnp.where((si >= ti) if reverse else (si <= ti), 1.0, 0.0).astype(BF16)
    last, mid = (0, L // 2) if reverse else (L - 1, L // 2 - 1)
    trow = lax.broadcasted_iota(jnp.int32, (L, gw), 0)
    scol = lax.broadcasted_iota(jnp.int32, (L, gw), 1) % HEAD_SIZE
    strict = (scol > trow) if reverse else (scol < trow)
    incl = (scol >= trow) if reverse else (scol <= trow)
    eye = jnp.where(scol == trow, 1.0, 0.0)
    bdm = (lax.broadcasted_iota(jnp.int32, (gw, gw), 0) // HEAD_SIZE
           == lax.broadcasted_iota(jnp.int32, (gw, gw), 1) // HEAD_SIZE)

    def bdiag(z):
        return jnp.where(bdm, jnp.concatenate([z] * hpg, axis=0), 0.0).astype(BF16)

    def compact(f):
        fm = jnp.where(bdm, f, 0.0)
        out = fm[:HEAD_SIZE]
        for h in range(1, hpg):
            out = out + fm[h * HEAD_SIZE:(h + 1) * HEAD_SIZE]
        return out

    for j in range(nch):
        rows = slice(j * L, (j + 1) * L)
        lw = lw_ref[rows, :]
        p1, p2, p3 = _split3(lw)
        cw = _dot(tri, p1) + _dot(tri, p2) + _dot(tri, p3)
        cw_end = cw[last:last + 1, :]
        cw_mid = cw[mid:mid + 1, :]
        r = r_ref[rows, :].astype(F32)
        k = k_ref[rows, :].astype(F32)
        a = a_ref[rows, :].astype(F32)
        b = b_ref[rows, :].astype(F32)
        e_in = jnp.exp(cw)
        e_ex = jnp.exp(cw - lw)
        e_inv = jnp.exp(cw_mid - cw)
        e_mid = jnp.exp(-cw_mid)
        e_end = e_inv * jnp.exp(cw_end - cw_mid)
        r0 = r * e_in
        a0 = a * e_ex
        ra = r0 * e_mid
        aa = a0 * e_mid
        kb = k * e_inv
        bb = b * e_inv
        ke = k * e_end
        be = b * e_end
        gs = range(ng)
        sls = [slice(g * gw, (g + 1) * gw) for g in gs]
        xq = [jnp.concatenate([aa[:, sl], ra[:, sl]], axis=0).astype(BF16) for sl in sls]
        gb = [_dot_nt(xq[g], bdiag(bb[:, sls[g]])) for g in gs]
        gk = [_dot_nt(xq[g], bdiag(kb[:, sls[g]])) for g in gs]
        a_ab = [jnp.where(strict, gb[g][:L], 0.0) for g in gs]
        vg = [v_ref[rows, sl] for sl in sls]
        akrk = [jnp.concatenate([jnp.where(strict, gk[g][:L], 0.0), jnp.where(incl, gk[g][L:], 0.0)],
                                axis=0).astype(BF16) for g in gs]
        tmat = [eye + a_ab[g] for g in gs]
        pw = [_dot(a_ab[g].astype(BF16), bdiag(a_ab[g])) for g in gs]
        avyv = [_dot(akrk[g], bdiag(vg[g])) for g in gs]
        n_lvl = L.bit_length() - 2
        for lvl in range(n_lvl):
            if lvl + 1 < n_lvl:
                z = [_dot(jnp.concatenate([tmat[g], pw[g]], axis=0).astype(BF16), bdiag(pw[g])) for g in gs]
                tmat = [tmat[g] + z[g][:L] for g in gs]
                pw = [z[g][L:] for g in gs]
            else:
                z = [_dot(tmat[g].astype(BF16), bdiag(pw[g])) for g in gs]
                tmat = [tmat[g] + z[g] for g in gs]
        t16 = [tmat[g].astype(BF16) for g in gs]
        ta = [_dot(t16[g], bdiag(a0[:, sls[g]])) for g in gs]
        tav = [_dot(t16[g], bdiag(avyv[g][:L])) for g in gs]
        rb = [jnp.where(incl, gb[g][L:], 0.0).astype(BF16) for g in gs]
        ekt = [jnp.concatenate([be[:, sl], ke[:, sl]], axis=0).T.astype(BF16) for sl in sls]
        rbta = [_dot(rb[g], bdiag(ta[g])) for g in gs]
        rbtav = [_dot(rb[g], bdiag(tav[g])) for g in gs]
        mfull = [_dot(ekt[g][:, :L], ta[g].astype(BF16)) for g in gs]
        nfull = [_dot(ekt[g], jnp.concatenate([tav[g], vg[g]], axis=0).astype(BF16)) for g in gs]
        for g in gs:
            sl = sls[g]
            y_ref[rows, sl] = rbtav[g] + avyv[g][L:]
            lhs_scr[j, g, :L, :] = (r0[:, sl] + rbta[g]).astype(BF16)
            lhs_scr[j, g, L:, :] = compact(mfull[g]).astype(BF16)
            n_scr[j, g] = compact(nfull[g])
            lwt = jnp.concatenate([lw[:, sl], jnp.zeros_like(lw[:, sl])], axis=0).T
            wcol = jnp.exp(jnp.sum(lwt, axis=1, keepdims=True))
            wl_scr[j, g] = compact(jnp.broadcast_to(wcol, (gw, gw)))

    for j in (reversed(range(nch)) if reverse else range(nch)):
        rows = slice(j * L, (j + 1) * L)
        s_old = [s_scr[g] for g in range(ng)]
        z = [_dot(lhs_scr[j, g], bdiag(s_old[g])) for g in range(ng)]
        for g in range(ng):
            y_ref[rows, g * gw:(g + 1) * gw] += z[g][:L]
            s_scr[g] = wl_scr[j, g] * s_old[g] + z[g][L:] + n_scr[j, g]


def _wkv_scan(r, lw, k, v, a, b, d_idx, n_ctx_blocks, reverse):
    bsz, t, d = r.shape
    L = WKV_CHUNK
    tb = _tile(WKV_BLOCK, t)
    nb = t // tb
    nch = tb // L
    gw = min(LANE_GROUP, d)
    ng = d // gw
    if reverse:
        cidx = lambda c: jnp.where(c < n_ctx_blocks, n_ctx_blocks - 1 - c, nb + n_ctx_blocks - 1 - c)
    else:
        cidx = lambda c: c
    tok = pl.BlockSpec((None, tb, d), lambda bb, c: (bb, cidx(c), 0))
    tokd = pl.BlockSpec((None, None, tb, d), lambda bb, c: (d_idx, bb, cidx(c), 0))
    return pl.pallas_call(
        functools.partial(_wkv_kernel, reverse=reverse),
        out_shape=jax.ShapeDtypeStruct((bsz, t, d), F32),
        grid=(bsz, nb),
        in_specs=[tok, tokd, tokd, tok, tok, tokd],
        out_specs=tok,
        scratch_shapes=[pltpu.VMEM((ng, HEAD_SIZE, gw), F32),
                        pltpu.VMEM((nch, ng, 2 * L, gw), BF16),
                        pltpu.VMEM((nch, ng, HEAD_SIZE, gw), F32),
                        pltpu.VMEM((nch, ng, HEAD_SIZE, gw), F32)],
        compiler_params=_params("parallel", "arbitrary"),
        name="wkv_rev" if reverse else "wkv_fwd",
    )(r, lw, k, v, a, b)


def _rwkv_out_kernel(y0_ref, y1_ref, r_ref, k_ref, v_ref, g_ref, rk_ref, lnw_ref, lnb_ref,
                     wo_ref, x_ref, gate_ref, o_ref):
    y = y0_ref[...] + y1_ref[...]
    ones = _seg_ones(min(LANE_GROUP, y.shape[1]))
    inv_n = 1.0 / HEAD_SIZE
    mu = _head_sum(y, ones) * inv_n
    yc = y - mu
    var = _head_sum(yc * yc, ones) * inv_n
    yn = yc * lax.rsqrt(var + LNX_EPS) * lnw_ref[...] + lnb_ref[...]
    ksum = k_ref[0].astype(F32) + k_ref[1].astype(F32)
    bonus = _head_sum(r_ref[...].astype(F32) * ksum * rk_ref[...], ones) * v_ref[...]
    out = ((yn + bonus) * g_ref[...].astype(F32)).astype(BF16)
    o_ref[...] = x_ref[...] + gate_ref[...] * _dot(out, wo_ref[...])


def _rwkv_output(y0, y1, r, kd, v, g, rk, lnw, lnb, wo_bf, x, gate, t_off):
    bsz, t, d = x.shape
    tm = _tile(256, t, t_off)
    off = t_off // tm
    tok = pl.BlockSpec((None, tm, d), lambda b, i: (b, i + off, 0))
    tok2 = pl.BlockSpec((2, None, tm, d), lambda b, i: (0, b, i + off, 0))
    row = pl.BlockSpec((1, d), lambda b, i: (0, 0))
    return pl.pallas_call(
        _rwkv_out_kernel,
        out_shape=jax.ShapeDtypeStruct((bsz, t, d), F32),
        grid=(bsz, t // tm),
        in_specs=[tok, tok, tok, tok2, tok, tok, row, row, row,
                  pl.BlockSpec((d, d), lambda b, i: (0, 0)),
                  pl.BlockSpec((None, tm, d), lambda b, i: (b, i, 0)),
                  pl.BlockSpec((None, 1, d), lambda b, i: (b, 0, 0))],
        out_specs=pl.BlockSpec((None, tm, d), lambda b, i: (b, i, 0)),
        compiler_params=_params("parallel", "parallel"),
        name="rwkv_out",
    )(y0, y1, r, kd, v, g, rk.reshape(1, d), lnw.reshape(1, d), lnb.reshape(1, d), wo_bf, x, gate)


def _moe_route_kernel(x_ref, g_ref, sc_ref, sh_ref, rt_ref, xs_ref, gs_ref, code_ref,
                      h_scr, aff_scr, *, cap):
    t, d = x_ref.shape
    n_e = rt_ref.shape[0]

    @pl.when(pl.program_id(1) == 0)
    def _():
        g = g_ref[...]
        sc = sc_ref[...]
        sh = sh_ref[...]
        rt_hi, rt_lo = _split2(rt_ref[...])
        tc = min(t, 256)
        for j in range(t // tc):
            h = _norm_mod(x_ref[j * tc:(j + 1) * tc, :], g, sc, sh)
            h_hi, h_lo = _split2(h)
            h_scr[j * tc:(j + 1) * tc, :] = h_hi
            aff_scr[:, j * tc:(j + 1) * tc] = (_dot_nt(rt_hi, h_hi) + _dot_nt(rt_hi, h_lo)
                                               + _dot_nt(rt_lo, h_hi))
        logits = aff_scr[...]
        m = jnp.max(logits, axis=0, keepdims=True)
        ex = jnp.exp(logits - m)
        aff = ex / jnp.sum(ex, axis=0, keepdims=True)
        aff_scr[...] = aff
        bits = pltpu.bitcast(aff, jnp.int32)

        def search(i, thr):
            cand = thr | (jnp.int32(1) << (30 - i))
            cnt = jnp.sum(jnp.where(bits >= cand, 1, 0), axis=1, keepdims=True)
            return jnp.where(cnt >= cap, cand, thr)

        thr = lax.fori_loop(0, 31, search, jnp.zeros((n_e, 1), jnp.int32))
        gt = bits > thr
        eq = bits == thr
        key = jnp.where(gt, 1, 0) + jnp.where(eq, 4096, 0)
        lane = lax.broadcasted_iota(jnp.int32, (n_e, t), 1)
        csum = key
        sh_amt = 1
        while sh_amt < t:
            csum = csum + jnp.where(lane >= sh_amt, pltpu.roll(csum, sh_amt, 1), 0)
            sh_amt *= 2
        before = csum - key
        n_gt = before & 4095
        n_eq = before >> 12
        need = cap - jnp.sum(jnp.where(gt, 1, 0), axis=1, keepdims=True)
        sel = gt | (eq & (n_eq < need))
        code_ref[...] = jnp.where(sel, n_gt + jnp.minimum(n_eq, need), -1)

    e = pl.program_id(1)
    code = code_ref[pl.ds(e, 1), :]
    aff_e = aff_scr[pl.ds(e, 1), :]
    slot = lax.broadcasted_iota(jnp.int32, (cap, t), 0)
    hit = code == slot
    xs_ref[...] = _dot(jnp.where(hit, 1.0, 0.0).astype(BF16), h_scr[...]).astype(BF16)
    gs_ref[...] = jnp.sum(jnp.where(hit, aff_e, 0.0), axis=1, keepdims=True)


def _moe_route(x, norm_g, sc, sh, router):
    bsz, t, d = x.shape
    n_e = router.shape[1]
    cap = CAPACITY_FACTOR * t // n_e
    vec = lambda b, e: (b, 0, 0)
    return pl.pallas_call(
        functools.partial(_moe_route_kernel, cap=cap),
        out_shape=[jax.ShapeDtypeStruct((bsz, n_e, cap, d), BF16),
                   jax.ShapeDtypeStruct((bsz, n_e, cap, 1), F32),
                   jax.ShapeDtypeStruct((bsz, n_e, t), jnp.int32)],
        grid=(bsz, n_e),
        in_specs=[pl.BlockSpec((None, t, d), lambda b, e: (b, 0, 0)),
                  pl.BlockSpec((1, d), lambda b, e: (0, 0)),
                  pl.BlockSpec((None, 1, d), vec),
                  pl.BlockSpec((None, 1, d), vec),
                  pl.BlockSpec((n_e, d), lambda b, e: (0, 0))],
        out_specs=[pl.BlockSpec((None, None, cap, d), lambda b, e: (b, e, 0, 0)),
                   pl.BlockSpec((None, None, cap, 1), lambda b, e: (b, e, 0, 0)),
                   pl.BlockSpec((None, n_e, t), lambda b, e: (b, 0, 0))],
        scratch_shapes=[pltpu.VMEM((t, d), BF16), pltpu.VMEM((n_e, t), F32)],
        compiler_params=_params("parallel", "arbitrary"),
        name="moe_route",
    )(x, norm_g.reshape(1, d), sc, sh, router.T)


def _moe_ffn_kernel(xs_ref, gs_ref, wg_ref, wu_ref, wd_ref, ys_ref, wg_scr, wu_scr, wd_scr):
    @pl.when(pl.program_id(1) == 0)
    def _():
        wg_scr[...] = wg_ref[...].astype(BF16)
        wu_scr[...] = wu_ref[...].astype(BF16)
        wd_scr[...] = wd_ref[...].astype(BF16)

    tb, cap, d = xs_ref.shape
    xs = xs_ref[...].reshape(tb * cap, d)
    hg = _dot(xs, wg_scr[...])
    hu = _dot(xs, wu_scr[...])
    hid = (hg * _sigmoid(hg) * hu).astype(BF16)
    ys = _dot(hid, wd_scr[...]) * gs_ref[...].reshape(tb * cap, 1)
    ys_ref[...] = ys.reshape(tb, cap, d).astype(BF16)


def _moe_ffn(xs, gs, wg, wu, wd):
    bsz, n_e, cap, d = xs.shape
    f = wg.shape[-1]
    tb = max(1, min(bsz, 512 // cap))
    while bsz % tb:
        tb -= 1
    return pl.pallas_call(
        _moe_ffn_kernel,
        out_shape=jax.ShapeDtypeStruct((bsz, n_e, cap, d), BF16),
        grid=(n_e, bsz // tb),
        in_specs=[pl.BlockSpec((tb, None, cap, d), lambda e, j: (j, e, 0, 0)),
                  pl.BlockSpec((tb, None, cap, 1), lambda e, j: (j, e, 0, 0)),
                  pl.BlockSpec((None, d, f), lambda e, j: (e, 0, 0)),
                  pl.BlockSpec((None, d, f), lambda e, j: (e, 0, 0)),
                  pl.BlockSpec((None, f, d), lambda e, j: (e, 0, 0))],
        out_specs=pl.BlockSpec((tb, None, cap, d), lambda e, j: (j, e, 0, 0)),
        scratch_shapes=[pltpu.VMEM((d, f), BF16), pltpu.VMEM((d, f), BF16), pltpu.VMEM((f, d), BF16)],
        compiler_params=_params("parallel", "arbitrary"),
        name="moe_ffn",
    )(xs, gs, wg, wu, wd)


def _moe_combine_kernel(code_ref, ys_ref, x_ref, gate_ref, fg_ref, o_ref, *, cap, final_norm):
    n_e = code_ref.shape[1]
    slot = lax.broadcasted_iota(jnp.int32, (1, cap), 1)
    pieces = [jnp.where(code_ref[:, e:e + 1] == slot, 1.0, 0.0).astype(BF16) for e in range(n_e)]
    scat = jnp.concatenate(pieces, axis=1)
    x = x_ref[...] + gate_ref[...] * _dot(scat, ys_ref[...])
    if final_norm:
        x = x * lax.rsqrt(jnp.mean(x * x, axis=-1, keepdims=True) + NORM_EPS) * fg_ref[...]
    o_ref[...] = x


def _moe_combine(code_t, ys, x, gate, final_g, final_norm):
    bsz, t, d = x.shape
    n_e = code_t.shape[-1]
    cap = ys.shape[1] // n_e
    tm = min(t, 512)
    return pl.pallas_call(
        functools.partial(_moe_combine_kernel, cap=cap, final_norm=final_norm),
        out_shape=jax.ShapeDtypeStruct((bsz, t, d), F32),
        grid=(bsz, t // tm),
        in_specs=[pl.BlockSpec((None, tm, n_e), lambda b, i: (b, i, 0)),
                  pl.BlockSpec((None, n_e * cap, d), lambda b, i: (b, 0, 0)),
                  pl.BlockSpec((None, tm, d), lambda b, i: (b, i, 0)),
                  pl.BlockSpec((None, 1, d), lambda b, i: (b, 0, 0)),
                  pl.BlockSpec((1, d), lambda b, i: (0, 0))],
        out_specs=pl.BlockSpec((None, tm, d), lambda b, i: (b, i, 0)),
        compiler_params=_params("parallel", "parallel"),
        name="moe_combine",
    )(code_t, ys, x, gate, final_g.reshape(1, d))


def _ec_moe_layer(x, norm_g, sc, sh, gate, router, wg, wu, wd, final_g, final_norm):
    bsz, t, d = x.shape
    xs, gs, code = _moe_route(x, norm_g, sc, sh, router)
    ys = _moe_ffn(xs, gs, wg, wu, wd)
    n_e, cap = xs.shape[1], xs.shape[2]
    return _moe_combine(jnp.swapaxes(code, 1, 2), ys.reshape(bsz, n_e * cap, d), x, gate,
                        final_g, final_norm)


def kernel(x, c, ctx, c_ctx, ada_w, ada_b, norm_g, fnet_wo, fnet_bo, rw_mix, rw_wr, rw_wk, rw_wv, rw_wo,
           rw_w0, rw_w1, rw_w2, rw_a0, rw_a1, rw_a2, rw_v0, rw_v1, rw_v2, rw_g1, rw_g2, rw_kk, rw_ka, rw_rk,
           rw_lnx_w, rw_lnx_b, moe_router, moe_wg, moe_wu, moe_wd, final_g):
    bsz, t, d = x.shape
    t_ctx = ctx.shape[1]
    depth = ada_w.shape[0]
    n_mixers = 2

    rows = -(-(bsz + 1) // 8) * 8
    cc = jnp.concatenate([c, c_ctx[None, :], jnp.zeros((rows - bsz - 1, d), F32)], axis=0)
    mods = _adaln(cc, ada_w, ada_b)

    def mod_x(i, j):
        return mods[i, :bsz, j * d:(j + 1) * d].reshape(bsz, 1, d)

    def mod_c(i, j):
        return jnp.broadcast_to(mods[i, bsz, j * d:(j + 1) * d].reshape(1, 1, d), (bsz, 1, d))

    mats_x = _dft_mats(t, d // FNET_GROUPS)
    mats_c = _dft_mats(t_ctx, d // FNET_GROUPS)
    vfirst = None
    for i in range(depth):
        need_ctx = i < depth - 1
        if i % n_mixers == 0:
            fi = i // n_mixers
            wo_bf = fnet_wo[fi].astype(BF16)
            x = _fnet_layer(x, norm_g[i, 0], mod_x(i, 1), mod_x(i, 0), mod_x(i, 2), wo_bf, fnet_bo[fi], mats_x)
            if need_ctx:
                ctx = _fnet_layer(ctx, norm_g[i, 0], mod_c(i, 1), mod_c(i, 0), mod_c(i, 2), wo_bf,
                                  fnet_bo[fi], mats_c)
        else:
            ri = i // n_mixers
            hx = _normmod(x, norm_g[i, 0], mod_x(i, 1), mod_x(i, 0))
            hc = _normmod(ctx, norm_g[i, 0], mod_c(i, 1), mod_c(i, 0))
            h = jnp.concatenate([hc, hx], axis=1)
            hs = jnp.concatenate([_shift_seq(hc), _qshift_grid(hx)], axis=1)
            p = dict(mix=rw_mix[ri], wr=rw_wr[ri], wk=rw_wk[ri], wv=rw_wv[ri], w0=rw_w0[ri], w1=rw_w1[ri],
                     w2=rw_w2[ri], a0=rw_a0[ri], a1=rw_a1[ri], a2=rw_a2[ri], g1=rw_g1[ri], g2=rw_g2[ri],
                     kk=rw_kk[ri], ka=rw_ka[ri])
            if ri > 0:
                p.update(v0=rw_v0[ri - 1], v1=rw_v1[ri - 1], v2=rw_v2[ri - 1])
            r, v, g, a, lw, kd, bd = _rwkv_project(h, hs, vfirst if ri > 0 else None, p)
            if ri == 0:
                vfirst = v
            ncb = t_ctx // _tile(WKV_BLOCK, t_ctx + t)
            y0 = _wkv_scan(r, lw, kd, v, a, bd, 0, ncb, False)
            y1 = _wkv_scan(r, lw, kd, v, a, bd, 1, ncb, True)
            wo_bf = rw_wo[ri].astype(BF16)
            args = (y0, y1, r, kd, v, g, rw_rk[ri], rw_lnx_w[ri], rw_lnx_b[ri], wo_bf)
            x = _rwkv_output(*args, x, mod_x(i, 2), t_ctx)
            if need_ctx:
                ctx = _rwkv_output(*args, ctx, mod_c(i, 2), 0)
        last = i == depth - 1
        x = _ec_moe_layer(x, norm_g[i, 1], mod_x(i, 4), mod_x(i, 3), mod_x(i, 5), moe_router[i],
                          moe_wg[i], moe_wu[i], moe_wd[i], final_g, last)
        if need_ctx:
            ctx = _ec_moe_layer(ctx, norm_g[i, 1], mod_c(i, 4), mod_c(i, 3), mod_c(i, 5), moe_router[i],
                                moe_wg[i], moe_wu[i], moe_wd[i], final_g, False)
    return x
```

```python
import functools

import jax
import jax.numpy as jnp
from jax import lax
from jax.experimental import pallas as pl
from jax.experimental.pallas import tpu as pltpu

F32 = jnp.float32
BF16 = jnp.bfloat16

HEAD_SIZE = 64
LANE_GROUP = 256
GRID_W = 64
FNET_GROUPS = 4
CAPACITY_FACTOR = 2
NORM_EPS = 1e-6
LNX_EPS = 64e-5
WKV_CHUNK = 64
WKV_BLOCK = 256
WKV_WAVE = 4
MOE_GATHER_ROWS = 1024
VMEM_LIMIT_BYTES = 56 * 1024 * 1024


def _params(*semantics):
    return pltpu.CompilerParams(dimension_semantics=semantics, vmem_limit_bytes=VMEM_LIMIT_BYTES)


def _tile(pref, *extents):
    tm = pref
    while any(n % tm for n in extents if n):
        tm //= 2
    return tm


def _dot(a, b):
    return jnp.dot(a, b, preferred_element_type=F32)


def _dot_nt(a, b):
    return lax.dot_general(a, b, (((1,), (1,)), ((), ())), preferred_element_type=F32)


def _dot_tn(a, b):
    return lax.dot_general(a, b, (((0,), (0,)), ((), ())), preferred_element_type=F32)


def _split2(x):
    hi = x.astype(BF16)
    lo = (x - hi.astype(F32)).astype(BF16)
    return hi, lo


def _split3(x):
    hi = x.astype(BF16)
    r1 = x - hi.astype(F32)
    mid = r1.astype(BF16)
    lo = (r1 - mid.astype(F32)).astype(BF16)
    return hi, mid, lo


def _norm_mod(x, g, sc, sh):
    ms = jnp.mean(x * x, axis=-1, keepdims=True)
    return x * lax.rsqrt(ms + NORM_EPS) * g * (1.0 + sc) + sh


def _sigmoid(x):
    return 1.0 / (1.0 + jnp.exp(-x))


def _seg_ones(n):
    r = lax.broadcasted_iota(jnp.int32, (n, n), 0) // HEAD_SIZE
    c = lax.broadcasted_iota(jnp.int32, (n, n), 1) // HEAD_SIZE
    return jnp.where(r == c, 1.0, 0.0).astype(BF16)


def _head_sum(x, ones):
    d = x.shape[-1]
    w = ones.shape[0]
    parts = []
    for j in range(d // w):
        hi, lo = _split2(x[:, j * w:(j + 1) * w])
        parts.append(_dot(hi, ones) + _dot(lo, ones))
    return parts[0] if len(parts) == 1 else jnp.concatenate(parts, axis=1)


def _adaln_kernel(c_ref, w_ref, b_ref, o_ref):
    c = c_ref[...]
    s = c * _sigmoid(c)
    s_hi, s_lo = _split2(s)
    w_hi, w_lo = _split2(w_ref[...])
    o_ref[...] = _dot(s_hi, w_hi) + _dot(s_lo, w_hi) + _dot(s_hi, w_lo) + b_ref[...]


def _adaln(cc, ada_w, ada_b):
    depth, d, n = ada_w.shape
    rows = cc.shape[0]
    tn = min(n, 1536)
    return pl.pallas_call(
        _adaln_kernel,
        out_shape=jax.ShapeDtypeStruct((depth, rows, n), F32),
        grid=(depth, n // tn),
        in_specs=[pl.BlockSpec((rows, d), lambda l, j: (0, 0)),
                  pl.BlockSpec((None, d, tn), lambda l, j: (l, 0, j)),
                  pl.BlockSpec((None, 1, tn), lambda l, j: (l, 0, j))],
        out_specs=pl.BlockSpec((None, rows, tn), lambda l, j: (l, 0, j)),
        compiler_params=_params("parallel", "parallel"),
        name="adaln",
    )(cc, ada_w, ada_b.reshape(depth, 1, n))


def _fnet_chan_kernel(x_ref, g_ref, sc_ref, sh_ref, cs_ref, o_ref):
    h = _norm_mod(x_ref[...], g_ref[...], sc_ref[...], sh_ref[...])
    gd = cs_ref.shape[0]
    cs = cs_ref[...]
    for j in range(h.shape[1] // gd):
        z = _dot(h[:, j * gd:(j + 1) * gd].astype(BF16), cs)
        o_ref[0, :, j * gd:(j + 1) * gd] = z[:, :gd].astype(BF16)
        o_ref[1, :, j * gd:(j + 1) * gd] = z[:, gd:].astype(BF16)


def _fnet_time_kernel(f_ref, hcs_ref, wo_ref, bo_ref, x_ref, gate_ref, o_ref):
    f = _dot(f_ref[...], hcs_ref[...])
    y = _dot(f.astype(BF16), wo_ref[...]) + bo_ref[...]
    o_ref[...] = x_ref[...] + gate_ref[...] * y


def _dft_mats(t, gd):
    def cs(n):
        i = jnp.arange(n, dtype=jnp.int32)
        ang = ((i[:, None] * i[None, :]) % n).astype(F32) * (2.0 * jnp.pi / n)
        return jnp.cos(ang), jnp.sin(ang)
    ct, st = cs(t)
    cc, sc = cs(gd)
    scale = 1.0 / jnp.sqrt(jnp.asarray(t * gd, F32))
    return (jnp.concatenate([ct, -st], axis=1).astype(BF16),
            (jnp.concatenate([cc, sc], axis=1) * scale).astype(BF16))


def _fnet_layer(x, norm_g, sc, sh, gate, wo_bf, bo, mats):
    bsz, t, d = x.shape
    f_mat, cs_mat = mats
    gd = d // FNET_GROUPS
    tm = min(t, 512)
    vec = lambda b, i: (b, 0, 0)
    hcs = pl.pallas_call(
        _fnet_chan_kernel,
        out_shape=jax.ShapeDtypeStruct((bsz, 2, t, d), BF16),
        grid=(bsz, t // tm),
        in_specs=[pl.BlockSpec((None, tm, d), lambda b, i: (b, i, 0)),
                  pl.BlockSpec((1, d), lambda b, i: (0, 0)),
                  pl.BlockSpec((None, 1, d), vec),
                  pl.BlockSpec((None, 1, d), vec),
                  pl.BlockSpec((gd, 2 * gd), lambda b, i: (0, 0))],
        out_specs=pl.BlockSpec((None, 2, tm, d), lambda b, i: (b, 0, i, 0)),
        compiler_params=_params("parallel", "parallel"),
        name="fnet_chan",
    )(x, norm_g.reshape(1, d), sc, sh, cs_mat)
    hcs = hcs.reshape(bsz, 2 * t, d)
    tm2 = min(t, 256)
    return pl.pallas_call(
        _fnet_time_kernel,
        out_shape=jax.ShapeDtypeStruct((bsz, t, d), F32),
        grid=(bsz, t // tm2),
        in_specs=[pl.BlockSpec((tm2, 2 * t), lambda b, i: (i, 0)),
                  pl.BlockSpec((None, 2 * t, d), lambda b, i: (b, 0, 0)),
                  pl.BlockSpec((d, d), lambda b, i: (0, 0)),
                  pl.BlockSpec((1, d), lambda b, i: (0, 0)),
                  pl.BlockSpec((None, tm2, d), lambda b, i: (b, i, 0)),
                  pl.BlockSpec((None, 1, d), vec)],
        out_specs=pl.BlockSpec((None, tm2, d), lambda b, i: (b, i, 0)),
        compiler_params=_params("parallel", "parallel"),
        name="fnet_time",
    )(f_mat, hcs, wo_bf, bo.reshape(1, d), x, gate)


def _shifted_grid(h, h_above, h_below):
    tm, d = h.shape
    q = d // 4
    colw = lax.broadcasted_iota(jnp.int32, (tm, q), 0) % GRID_W
    left = jnp.where(colw != 0, pltpu.roll(h[:, :q], 1, 0), 0.0)
    right = jnp.where(colw != GRID_W - 1, pltpu.roll(h[:, q:2 * q], tm - 1, 0), 0.0)
    up = jnp.concatenate([h_above[:, 2 * q:3 * q], h[:tm - GRID_W, 2 * q:3 * q]], axis=0)
    down = jnp.concatenate([h[GRID_W:, 3 * q:], h_below[:, 3 * q:]], axis=0)
    return jnp.concatenate([left, right, up, down], axis=1)


def _shifted_seq(h):
    tm, d = h.shape
    half = d // 2
    row = lax.broadcasted_iota(jnp.int32, (tm, half), 0)
    prev = jnp.where(row != 0, pltpu.roll(h[:, :half], 1, 0), 0.0)
    nxt = jnp.where(row != tm - 1, pltpu.roll(h[:, half:], tm - 1, 0), 0.0)
    return jnp.concatenate([prev, nxt], axis=1)


def _rwkv_proj_kernel(*refs, has_vres, grid_shift, n_alias):
    it = iter(refs)
    x_ref = next(it)
    xa_ref, xb_ref = (next(it), next(it)) if grid_shift else (None, None)
    ng_ref, sc_ref, sh_ref = next(it), next(it), next(it)
    vf_ref = next(it) if has_vres else None
    mix_ref, wr_ref, wk_ref, wv_ref = next(it), next(it), next(it), next(it)
    w0_ref, w1_ref, w2_ref = next(it), next(it), next(it)
    a0_ref, a1_ref, a2_ref = next(it), next(it), next(it)
    g1_ref, g2_ref, kk_ref, ka_ref = next(it), next(it), next(it), next(it)
    if has_vres:
        v0_ref, v1_ref, v2_ref = next(it), next(it), next(it)
    for _ in range(n_alias):
        next(it)
    r_out, v_out, g_out, a_out = next(it), next(it), next(it), next(it)
    lw_out, kd_out, bd_out = next(it), next(it), next(it)

    norm = lambda ref: _norm_mod(ref[...], ng_ref[...], sc_ref[...], sh_ref[...])
    h = norm(x_ref)
    if grid_shift:
        i = pl.program_id(1)
        h_above = jnp.where(i > 0, norm(xa_ref), 0.0)
        h_below = jnp.where(i < pl.num_programs(1) - 1, norm(xb_ref), 0.0)
        hs = _shifted_grid(h, h_above, h_below)
    else:
        hs = _shifted_seq(h)
    xx = hs - h
    xr, xw, xk, xv, xa, xg = [(h + xx * mix_ref[j:j + 1, :]).astype(BF16) for j in range(6)]
    r = _dot(xr, wr_ref[...])
    k = _dot(xk, wk_ref[...])
    v = _dot(xv, wv_ref[...])
    if has_vres:
        lora = _dot(_dot(xv, v1_ref[...]).astype(BF16), v2_ref[...])
        v = v + (vf_ref[...] - v) * _sigmoid(v0_ref[...] + lora)
    g = _dot(_sigmoid(_dot(xg, g1_ref[...])).astype(BF16), g2_ref[...])
    ones = _seg_ones(min(LANE_GROUP, h.shape[1]))
    kk = k * kk_ref[...]
    kk = kk * lax.rsqrt(_head_sum(kk * kk, ones) + 1e-12)
    r_out[...] = r.astype(BF16)
    v_out[...] = v
    g_out[...] = g.astype(BF16)
    a_out[...] = (-kk).astype(BF16)
    for d in range(2):
        wlog = w0_ref[d:d + 1, :] + _dot(jnp.tanh(_dot(xw, w1_ref[d])).astype(BF16), w2_ref[d])
        z = -wlog
        softplus = jnp.maximum(z, 0.0) + jnp.log(1.0 + jnp.exp(-jnp.abs(z)))
        lw_out[d] = -jnp.exp(-softplus - 0.5)
        a = _sigmoid(a0_ref[d:d + 1, :] + _dot(_dot(xa, a1_ref[d]).astype(BF16), a2_ref[d]))
        kd_out[d] = (k * (1.0 + (a - 1.0) * ka_ref[...])).astype(BF16)
        bd_out[d] = (kk * a).astype(BF16)


def _rwkv_project(x, norm_g, sc, sh, vfirst, p, tt, t_off, grid_shift, filled):
    bsz, t, d = x.shape
    tm = _tile(256, t, t_off)
    off = t_off // tm
    has_vres = vfirst is not None
    if not grid_shift:
        assert t == tm, "the sequence shift handles a stream that fits one tile"
    tok = pl.BlockSpec((None, tm, d), lambda b, i: (b, i + off, 0))
    tok2 = pl.BlockSpec((2, None, tm, d), lambda b, i: (0, b, i + off, 0))
    vec = pl.BlockSpec((None, 1, d), lambda b, i: (b, 0, 0))

    def full(a):
        nd = a.ndim
        return pl.BlockSpec(a.shape, lambda b, i: (0,) * nd)

    bf = lambda a: a.astype(BF16)
    weights = [p["mix"], bf(p["wr"]), bf(p["wk"]), bf(p["wv"]),
               p["w0"], bf(p["w1"]), bf(p["w2"]), p["a0"], bf(p["a1"]), bf(p["a2"]),
               bf(p["g1"]), bf(p["g2"]), p["kk"].reshape(1, d), p["ka"].reshape(1, d)]
    if has_vres:
        weights += [p["v0"].reshape(1, d), bf(p["v1"]), bf(p["v2"])]
    acts = [x]
    act_specs = [pl.BlockSpec((None, tm, d), lambda b, i: (b, i, 0))]
    if grid_shift:
        rpt = tm // GRID_W
        last = t // GRID_W - 1
        acts += [x, x]
        act_specs += [pl.BlockSpec((None, GRID_W, d), lambda b, i: (b, jnp.maximum(i * rpt - 1, 0), 0)),
                      pl.BlockSpec((None, GRID_W, d), lambda b, i: (b, jnp.minimum((i + 1) * rpt, last), 0))]
    acts += [norm_g.reshape(1, d), sc, sh]
    act_specs += [pl.BlockSpec((1, d), lambda b, i: (0, 0)), vec, vec]
    if has_vres:
        acts.append(vfirst)
        act_specs.append(tok)
    out_shape = [jax.ShapeDtypeStruct((bsz, tt, d), BF16),
                 jax.ShapeDtypeStruct((bsz, tt, d), F32),
                 jax.ShapeDtypeStruct((bsz, tt, d), BF16),
                 jax.ShapeDtypeStruct((bsz, tt, d), BF16),
                 jax.ShapeDtypeStruct((2, bsz, tt, d), F32),
                 jax.ShapeDtypeStruct((2, bsz, tt, d), BF16),
                 jax.ShapeDtypeStruct((2, bsz, tt, d), BF16)]
    filled = list(filled or [])
    n_in = len(acts) + len(weights)
    return pl.pallas_call(
        functools.partial(_rwkv_proj_kernel, has_vres=has_vres, grid_shift=grid_shift, n_alias=len(filled)),
        out_shape=out_shape,
        grid=(bsz, t // tm),
        in_specs=act_specs + [full(w) for w in weights] + [pl.BlockSpec(memory_space=pl.ANY)] * len(filled),
        out_specs=[tok, tok, tok, tok, tok2, tok2, tok2],
        input_output_aliases={n_in + j: j for j in range(len(filled))},
        compiler_params=_params("parallel", "parallel"),
        name="rwkv_proj",
    )(*acts, *weights, *filled)


def _wkv_kernel(r_ref, lw_ref, k_ref, v_ref, a_ref, b_ref, y_ref, s_scr, lhs_scr, n_scr, wl_scr, *, reverse):
    L = WKV_CHUNK
    tb, d = r_ref.shape
    nch = tb // L
    gw = s_scr.shape[-1]
    hpg = gw // HEAD_SIZE
    ng = d // gw

    @pl.when(pl.program_id(1) == 0)
    def _():
        s_scr[...] = jnp.zeros_like(s_scr)

    ti = lax.broadcasted_iota(jnp.int32, (L, L), 0)
    si = lax.broadcasted_iota(jnp.int32, (L, L), 1)
    tri = jnp.where((si >= ti) if reverse else (si <= ti), 1.0, 0.0).astype(BF16)
    last, mid = (0, L // 2) if reverse else (L - 1, L // 2 - 1)
    trow = lax.broadcasted_iota(jnp.int32, (L, gw), 0)
    scol = lax.broadcasted_iota(jnp.int32, (L, gw), 1) % HEAD_SIZE
    strict = (scol > trow) if reverse else (scol < trow)
    incl = (scol >= trow) if reverse else (scol <= trow)
    eye = jnp.where(scol == trow, 1.0, 0.0)
    bdm = (lax.broadcasted_iota(jnp.int32, (gw, gw), 0) // HEAD_SIZE
           == lax.broadcasted_iota(jnp.int32, (gw, gw), 1) // HEAD_SIZE)

    def bdiag(z):
        return jnp.where(bdm, jnp.concatenate([z] * hpg, axis=0), 0.0).astype(BF16)

    def compact(f):
        fm = jnp.where(bdm, f, 0.0)
        out = fm[:HEAD_SIZE]
        for h in range(1, hpg):
            out = out + fm[h * HEAD_SIZE:(h + 1) * HEAD_SIZE]
        return out

    def prep(j):
        rows = slice(j * L, (j + 1) * L)
        lw = lw_ref[rows, :]
        p1, p2, p3 = _split3(lw)
        cw = _dot(tri, p1) + _dot(tri, p2) + _dot(tri, p3)
        cw_end = cw[last:last + 1, :]
        cw_mid = cw[mid:mid + 1, :]
        r = r_ref[rows, :].astype(F32)
        k = k_ref[rows, :].astype(F32)
        a = a_ref[rows, :].astype(F32)
        b = b_ref[rows, :].astype(F32)
        e_in = jnp.exp(cw)
        e_ex = jnp.exp(cw - lw)
        e_inv = jnp.exp(cw_mid - cw)
        e_mid = jnp.exp(-cw_mid)
        e_end = e_inv * jnp.exp(cw_end - cw_mid)
        r0 = r * e_in
        a0 = a * e_ex
        return dict(rows=rows, lw=lw, r0=r0, a0=a0, ra=r0 * e_mid, aa=a0 * e_mid, kb=k * e_inv, bb=b * e_inv,
                    ke=k * e_end, be=b * e_end)

    for w0 in range(0, nch, WKV_WAVE):
        pj = {j: prep(j) for j in range(w0, min(nch, w0 + WKV_WAVE))}
        ch = [(j, slice(g * gw, (g + 1) * gw)) for j in pj for g in range(ng)]
        cs = range(len(ch))
        col = lambda name, c: pj[ch[c][0]][name][:, ch[c][1]]
        xq = [jnp.concatenate([col("aa", c), col("ra", c)], axis=0).astype(BF16) for c in cs]
        gb = [_dot_nt(xq[c], bdiag(col("bb", c))) for c in cs]
        gk = [_dot_nt(xq[c], bdiag(col("kb", c))) for c in cs]
        a_ab = [jnp.where(strict, gb[c][:L], 0.0) for c in cs]
        vg = [v_ref[pj[ch[c][0]]["rows"], ch[c][1]] for c in cs]
        akrk = [jnp.concatenate([jnp.where(strict, gk[c][:L], 0.0), jnp.where(incl, gk[c][L:], 0.0)],
                                axis=0).astype(BF16) for c in cs]
        tmat = [eye + a_ab[c] for c in cs]
        pw = [_dot(a_ab[c].astype(BF16), bdiag(a_ab[c])) for c in cs]
        avyv = [_dot(akrk[c], bdiag(vg[c])) for c in cs]
        n_lvl = L.bit_length() - 2
        for lvl in range(n_lvl):
            if lvl + 1 < n_lvl:
                z = [_dot(jnp.concatenate([tmat[c], pw[c]], axis=0).astype(BF16), bdiag(pw[c])) for c in cs]
                tmat = [tmat[c] + z[c][:L] for c in cs]
                pw = [z[c][L:] for c in cs]
            else:
                z = [_dot(tmat[c].astype(BF16), bdiag(pw[c])) for c in cs]
                tmat = [tmat[c] + z[c] for c in cs]
        t16 = [tmat[c].astype(BF16) for c in cs]
        ta = [_dot(t16[c], bdiag(col("a0", c))) for c in cs]
        tav = [_dot(t16[c], bdiag(avyv[c][:L])) for c in cs]
        rb = [jnp.where(incl, gb[c][L:], 0.0).astype(BF16) for c in cs]
        ekt = [jnp.concatenate([col("be", c), col("ke", c)], axis=0).T.astype(BF16) for c in cs]
        rbta = [_dot(rb[c], bdiag(ta[c])) for c in cs]
        rbtav = [_dot(rb[c], bdiag(tav[c])) for c in cs]
        mfull = [_dot(ekt[c][:, :L], ta[c].astype(BF16)) for c in cs]
        nfull = [_dot(ekt[c], jnp.concatenate([tav[c], vg[c]], axis=0).astype(BF16)) for c in cs]
        for c in cs:
            j, sl = ch[c]
            g = c % ng
            y_ref[pj[j]["rows"], sl] = rbtav[c] + avyv[c][L:]
            lhs_scr[j, g, :L, :] = (col("r0", c) + rbta[c]).astype(BF16)
            lhs_scr[j, g, L:, :] = compact(mfull[c]).astype(BF16)
            n_scr[j, g] = compact(nfull[c])
            lw = col("lw", c)
            lwt = jnp.concatenate([lw, jnp.zeros_like(lw)], axis=0).T
            wcol = jnp.exp(jnp.sum(lwt, axis=1, keepdims=True))
            wl_scr[j, g] = compact(jnp.broadcast_to(wcol, (gw, gw)))

    for j in (reversed(range(nch)) if reverse else range(nch)):
        rows = slice(j * L, (j + 1) * L)
        s_old = [s_scr[g] for g in range(ng)]
        z = [_dot(lhs_scr[j, g], bdiag(s_old[g])) for g in range(ng)]
        for g in range(ng):
            y_ref[rows, g * gw:(g + 1) * gw] += z[g][:L]
            s_scr[g] = wl_scr[j, g] * s_old[g] + z[g][L:] + n_scr[j, g]


def _wkv_scan(r, lw, k, v, a, b, d_idx, n_ctx_blocks, reverse):
    bsz, t, d = r.shape
    L = WKV_CHUNK
    tb = _tile(WKV_BLOCK, t)
    nb = t // tb
    nch = tb // L
    gw = min(LANE_GROUP, d)
    ng = d // gw
    if reverse:
        cidx = lambda c: jnp.where(c < n_ctx_blocks, n_ctx_blocks - 1 - c, nb + n_ctx_blocks - 1 - c)
    else:
        cidx = lambda c: c
    tok = pl.BlockSpec((None, tb, d), lambda bb, c: (bb, cidx(c), 0))
    tokd = pl.BlockSpec((None, None, tb, d), lambda bb, c: (d_idx, bb, cidx(c), 0))
    return pl.pallas_call(
        functools.partial(_wkv_kernel, reverse=reverse),
        out_shape=jax.ShapeDtypeStruct((bsz, t, d), F32),
        grid=(bsz, nb),
        in_specs=[tok, tokd, tokd, tok, tok, tokd],
        out_specs=tok,
        scratch_shapes=[pltpu.VMEM((ng, HEAD_SIZE, gw), F32),
                        pltpu.VMEM((nch, ng, 2 * L, gw), BF16),
                        pltpu.VMEM((nch, ng, HEAD_SIZE, gw), F32),
                        pltpu.VMEM((nch, ng, HEAD_SIZE, gw), F32)],
        compiler_params=_params("parallel", "arbitrary"),
        name="wkv_rev" if reverse else "wkv_fwd",
    )(r, lw, k, v, a, b)


def _rwkv_out_kernel(y0_ref, y1_ref, r_ref, k_ref, v_ref, g_ref, rk_ref, lnw_ref, lnb_ref,
                     wo_ref, x_ref, gate_ref, o_ref):
    y = y0_ref[...] + y1_ref[...]
    ones = _seg_ones(min(LANE_GROUP, y.shape[1]))
    inv_n = 1.0 / HEAD_SIZE
    mu = _head_sum(y, ones) * inv_n
    yc = y - mu
    var = _head_sum(yc * yc, ones) * inv_n
    yn = yc * lax.rsqrt(var + LNX_EPS) * lnw_ref[...] + lnb_ref[...]
    ksum = k_ref[0].astype(F32) + k_ref[1].astype(F32)
    bonus = _head_sum(r_ref[...].astype(F32) * ksum * rk_ref[...], ones) * v_ref[...]
    out = ((yn + bonus) * g_ref[...].astype(F32)).astype(BF16)
    o_ref[...] = x_ref[...] + gate_ref[...] * _dot(out, wo_ref[...])


def _rwkv_output(y0, y1, r, kd, v, g, rk, lnw, lnb, wo_bf, x, gate, t_off):
    bsz, t, d = x.shape
    tm = _tile(256, t, t_off)
    off = t_off // tm
    tok = pl.BlockSpec((None, tm, d), lambda b, i: (b, i + off, 0))
    tok2 = pl.BlockSpec((2, None, tm, d), lambda b, i: (0, b, i + off, 0))
    row = pl.BlockSpec((1, d), lambda b, i: (0, 0))
    return pl.pallas_call(
        _rwkv_out_kernel,
        out_shape=jax.ShapeDtypeStruct((bsz, t, d), F32),
        grid=(bsz, t // tm),
        in_specs=[tok, tok, tok, tok2, tok, tok, row, row, row,
                  pl.BlockSpec((d, d), lambda b, i: (0, 0)),
                  pl.BlockSpec((None, tm, d), lambda b, i: (b, i, 0)),
                  pl.BlockSpec((None, 1, d), lambda b, i: (b, 0, 0))],
        out_specs=pl.BlockSpec((None, tm, d), lambda b, i: (b, i, 0)),
        compiler_params=_params("parallel", "parallel"),
        name="rwkv_out",
    )(y0, y1, r, kd, v, g, rk.reshape(1, d), lnw.reshape(1, d), lnb.reshape(1, d), wo_bf, x, gate)


def _moe_route_kernel(x_ref, g_ref, sc_ref, sh_ref, rt_ref, xs_ref, gs_ref, code_ref,
                      h_scr, aff_scr, *, cap):
    t, d = x_ref.shape
    n_e = rt_ref.shape[0]

    @pl.when(pl.program_id(1) == 0)
    def _():
        g = g_ref[...]
        sc = sc_ref[...]
        sh = sh_ref[...]
        rt_hi, rt_lo = _split2(rt_ref[...])
        tc = min(t, 256)
        for j in range(t // tc):
            h = _norm_mod(x_ref[j * tc:(j + 1) * tc, :], g, sc, sh)
            h_hi, h_lo = _split2(h)
            h_scr[j * tc:(j + 1) * tc, :] = h_hi
            aff_scr[:, j * tc:(j + 1) * tc] = (_dot_nt(rt_hi, h_hi) + _dot_nt(rt_hi, h_lo)
                                               + _dot_nt(rt_lo, h_hi))
        logits = aff_scr[...]
        m = jnp.max(logits, axis=0, keepdims=True)
        ex = jnp.exp(logits - m)
        aff = ex / jnp.sum(ex, axis=0, keepdims=True)
        aff_scr[...] = aff
        bits = pltpu.bitcast(aff, jnp.int32)

        def search(i, thr):
            cand = thr | (jnp.int32(1) << (30 - i))
            cnt = jnp.sum(jnp.where(bits >= cand, 1, 0), axis=1, keepdims=True)
            return jnp.where(cnt >= cap, cand, thr)

        thr = lax.fori_loop(0, 31, search, jnp.zeros((n_e, 1), jnp.int32))
        gt = bits > thr
        eq = bits == thr
        key = jnp.where(gt, 1, 0) + jnp.where(eq, 4096, 0)
        lane = lax.broadcasted_iota(jnp.int32, (n_e, t), 1)
        csum = key
        sh_amt = 1
        while sh_amt < t:
            csum = csum + jnp.where(lane >= sh_amt, pltpu.roll(csum, sh_amt, 1), 0)
            sh_amt *= 2
        before = csum - key
        n_gt = before & 4095
        n_eq = before >> 12
        need = cap - jnp.sum(jnp.where(gt, 1, 0), axis=1, keepdims=True)
        sel = gt | (eq & (n_eq < need))
        code_ref[...] = jnp.where(sel, n_gt + jnp.minimum(n_eq, need), -1)

    eg = xs_ref.shape[0]
    e0 = pl.program_id(1) * eg
    slot = lax.broadcasted_iota(jnp.int32, (cap, t), 0)
    onehot = []
    for l in range(eg):
        hit = code_ref[pl.ds(e0 + l, 1), :] == slot
        onehot.append(jnp.where(hit, 1.0, 0.0).astype(BF16))
        gs_ref[l] = jnp.sum(jnp.where(hit, aff_scr[pl.ds(e0 + l, 1), :], 0.0), axis=1, keepdims=True)
    gathered = _dot(onehot[0] if eg == 1 else jnp.concatenate(onehot, axis=0), h_scr[...])
    xs_ref[...] = gathered.reshape(eg, cap, d).astype(BF16)


def _moe_route(x, norm_g, sc, sh, router):
    bsz, t, d = x.shape
    n_e = router.shape[1]
    cap = CAPACITY_FACTOR * t // n_e
    eg = _tile(max(1, MOE_GATHER_ROWS // cap), n_e)
    vec = lambda b, e: (b, 0, 0)
    return pl.pallas_call(
        functools.partial(_moe_route_kernel, cap=cap),
        out_shape=[jax.ShapeDtypeStruct((bsz, n_e, cap, d), BF16),
                   jax.ShapeDtypeStruct((bsz, n_e, cap, 1), F32),
                   jax.ShapeDtypeStruct((bsz, n_e, t), jnp.int32)],
        grid=(bsz, n_e // eg),
        in_specs=[pl.BlockSpec((None, t, d), lambda b, e: (b, 0, 0)),
                  pl.BlockSpec((1, d), lambda b, e: (0, 0)),
                  pl.BlockSpec((None, 1, d), vec),
                  pl.BlockSpec((None, 1, d), vec),
                  pl.BlockSpec((n_e, d), lambda b, e: (0, 0))],
        out_specs=[pl.BlockSpec((None, eg, cap, d), lambda b, e: (b, e, 0, 0)),
                   pl.BlockSpec((None, eg, cap, 1), lambda b, e: (b, e, 0, 0)),
                   pl.BlockSpec((None, n_e, t), lambda b, e: (b, 0, 0))],
        scratch_shapes=[pltpu.VMEM((t, d), BF16), pltpu.VMEM((n_e, t), F32)],
        compiler_params=_params("parallel", "arbitrary"),
        name="moe_route",
    )(x, norm_g.reshape(1, d), sc, sh, router.T)


def _moe_ffn_kernel(xs_ref, gs_ref, wg_ref, wu_ref, wd_ref, ys_ref, wg_scr, wu_scr, wd_scr):
    @pl.when(pl.program_id(1) == 0)
    def _():
        wg_scr[...] = wg_ref[...].astype(BF16)
        wu_scr[...] = wu_ref[...].astype(BF16)
        wd_scr[...] = wd_ref[...].astype(BF16)

    tb, cap, d = xs_ref.shape
    xs = xs_ref[...].reshape(tb * cap, d)
    hg = _dot(xs, wg_scr[...])
    hu = _dot(xs, wu_scr[...])
    hid = (hg * _sigmoid(hg) * hu).astype(BF16)
    ys = _dot(hid, wd_scr[...]) * gs_ref[...].reshape(tb * cap, 1)
    ys_ref[...] = ys.reshape(tb, cap, d).astype(BF16)


def _moe_ffn(xs, gs, wg, wu, wd):
    bsz, n_e, cap, d = xs.shape
    f = wg.shape[-1]
    tb = max(1, min(bsz, 512 // cap))
    while bsz % tb:
        tb -= 1
    return pl.pallas_call(
        _moe_ffn_kernel,
        out_shape=jax.ShapeDtypeStruct((bsz, n_e, cap, d), BF16),
        grid=(n_e, bsz // tb),
        in_specs=[pl.BlockSpec((tb, None, cap, d), lambda e, j: (j, e, 0, 0)),
                  pl.BlockSpec((tb, None, cap, 1), lambda e, j: (j, e, 0, 0)),
                  pl.BlockSpec((None, d, f), lambda e, j: (e, 0, 0)),
                  pl.BlockSpec((None, d, f), lambda e, j: (e, 0, 0)),
                  pl.BlockSpec((None, f, d), lambda e, j: (e, 0, 0))],
        out_specs=pl.BlockSpec((tb, None, cap, d), lambda e, j: (j, e, 0, 0)),
        scratch_shapes=[pltpu.VMEM((d, f), BF16), pltpu.VMEM((d, f), BF16), pltpu.VMEM((f, d), BF16)],
        compiler_params=_params("parallel", "arbitrary"),
        name="moe_ffn",
    )(xs, gs, wg, wu, wd)


def _moe_combine_kernel(code_ref, ys_ref, x_ref, gate_ref, fg_ref, o_ref, *, cap, final_norm):
    n_e = code_ref.shape[1]
    slot = lax.broadcasted_iota(jnp.int32, (1, cap), 1)
    pieces = [jnp.where(code_ref[:, e:e + 1] == slot, 1.0, 0.0).astype(BF16) for e in range(n_e)]
    scat = jnp.concatenate(pieces, axis=1)
    x = x_ref[...] + gate_ref[...] * _dot(scat, ys_ref[...])
    if final_norm:
        x = x * lax.rsqrt(jnp.mean(x * x, axis=-1, keepdims=True) + NORM_EPS) * fg_ref[...]
    o_ref[...] = x


def _moe_combine(code_t, ys, x, gate, final_g, final_norm):
    bsz, t, d = x.shape
    n_e = code_t.shape[-1]
    cap = ys.shape[1] // n_e
    tm = _tile(1024, t)
    return pl.pallas_call(
        functools.partial(_moe_combine_kernel, cap=cap, final_norm=final_norm),
        out_shape=jax.ShapeDtypeStruct((bsz, t, d), F32),
        grid=(bsz, t // tm),
        in_specs=[pl.BlockSpec((None, tm, n_e), lambda b, i: (b, i, 0)),
                  pl.BlockSpec((None, n_e * cap, d), lambda b, i: (b, 0, 0)),
                  pl.BlockSpec((None, tm, d), lambda b, i: (b, i, 0)),
                  pl.BlockSpec((None, 1, d), lambda b, i: (b, 0, 0)),
                  pl.BlockSpec((1, d), lambda b, i: (0, 0))],
        out_specs=pl.BlockSpec((None, tm, d), lambda b, i: (b, i, 0)),
        compiler_params=_params("parallel", "parallel"),
        name="moe_combine",
    )(code_t, ys, x, gate, final_g.reshape(1, d))


def _ec_moe_layer(x, norm_g, sc, sh, gate, router, wg, wu, wd, final_g, final_norm):
    bsz, t, d = x.shape
    xs, gs, code = _moe_route(x, norm_g, sc, sh, router)
    ys = _moe_ffn(xs, gs, wg, wu, wd)
    n_e, cap = xs.shape[1], xs.shape[2]
    return _moe_combine(jnp.swapaxes(code, 1, 2), ys.reshape(bsz, n_e * cap, d), x, gate,
                        final_g, final_norm)


def kernel(x, c, ctx, c_ctx, ada_w, ada_b, norm_g, fnet_wo, fnet_bo, rw_mix, rw_wr, rw_wk, rw_wv, rw_wo,
           rw_w0, rw_w1, rw_w2, rw_a0, rw_a1, rw_a2, rw_v0, rw_v1, rw_v2, rw_g1, rw_g2, rw_kk, rw_ka, rw_rk,
           rw_lnx_w, rw_lnx_b, moe_router, moe_wg, moe_wu, moe_wd, final_g):
    bsz, t, d = x.shape
    t_ctx = ctx.shape[1]
    depth = ada_w.shape[0]
    n_mixers = 2

    rows = -(-(bsz + 1) // 8) * 8
    cc = jnp.concatenate([c, c_ctx[None, :], jnp.zeros((rows - bsz - 1, d), F32)], axis=0)
    mods = _adaln(cc, ada_w, ada_b)

    def mod_x(i, j):
        return mods[i, :bsz, j * d:(j + 1) * d].reshape(bsz, 1, d)

    def mod_c(i, j):
        return jnp.broadcast_to(mods[i, bsz, j * d:(j + 1) * d].reshape(1, 1, d), (bsz, 1, d))

    mats_x = _dft_mats(t, d // FNET_GROUPS)
    mats_c = _dft_mats(t_ctx, d // FNET_GROUPS)
    vfirst = None
    for i in range(depth):
        need_ctx = i < depth - 1
        if i % n_mixers == 0:
            fi = i // n_mixers
            wo_bf = fnet_wo[fi].astype(BF16)
            x = _fnet_layer(x, norm_g[i, 0], mod_x(i, 1), mod_x(i, 0), mod_x(i, 2), wo_bf, fnet_bo[fi], mats_x)
            if need_ctx:
                ctx = _fnet_layer(ctx, norm_g[i, 0], mod_c(i, 1), mod_c(i, 0), mod_c(i, 2), wo_bf,
                                  fnet_bo[fi], mats_c)
        else:
            ri = i // n_mixers
            p = dict(mix=rw_mix[ri], wr=rw_wr[ri], wk=rw_wk[ri], wv=rw_wv[ri], w0=rw_w0[ri], w1=rw_w1[ri],
                     w2=rw_w2[ri], a0=rw_a0[ri], a1=rw_a1[ri], a2=rw_a2[ri], g1=rw_g1[ri], g2=rw_g2[ri],
                     kk=rw_kk[ri], ka=rw_ka[ri])
            if ri > 0:
                p.update(v0=rw_v0[ri - 1], v1=rw_v1[ri - 1], v2=rw_v2[ri - 1])
            vf = vfirst if ri > 0 else None
            tt = t_ctx + t
            part = _rwkv_project(ctx, norm_g[i, 0], mod_c(i, 1), mod_c(i, 0), vf, p, tt, 0, False, None)
            r, v, g, a, lw, kd, bd = _rwkv_project(x, norm_g[i, 0], mod_x(i, 1), mod_x(i, 0), vf, p, tt,
                                                   t_ctx, True, part)
            if ri == 0:
                vfirst = v
            ncb = t_ctx // _tile(WKV_BLOCK, t_ctx + t)
            y0 = _wkv_scan(r, lw, kd, v, a, bd, 0, ncb, False)
            y1 = _wkv_scan(r, lw, kd, v, a, bd, 1, ncb, True)
            wo_bf = rw_wo[ri].astype(BF16)
            args = (y0, y1, r, kd, v, g, rw_rk[ri], rw_lnx_w[ri], rw_lnx_b[ri], wo_bf)
            x = _rwkv_output(*args, x, mod_x(i, 2), t_ctx)
            if need_ctx:
                ctx = _rwkv_output(*args, ctx, mod_c(i, 2), 0)
        last = i == depth - 1
        x = _ec_moe_layer(x, norm_g[i, 1], mod_x(i, 4), mod_x(i, 3), mod_x(i, 5), moe_router[i],
                          moe_wg[i], moe_wu[i], moe_wd[i], final_g, last)
        if need_ctx:
            ctx = _ec_moe_layer(ctx, norm_g[i, 1], mod_c(i, 4), mod_c(i, 3), mod_c(i, 5), moe_router[i],
                                moe_wg[i], moe_wu[i], moe_wd[i], final_g, False)
    return x
```

```python
import functools

import jax
import jax.numpy as jnp
from jax import lax
from jax.experimental import pallas as pl
from jax.experimental.pallas import tpu as pltpu

F32 = jnp.float32
BF16 = jnp.bfloat16

HEAD_SIZE = 64
LANE_GROUP = 256
GRID_W = 64
FNET_GROUPS = 4
CAPACITY_FACTOR = 2
NORM_EPS = 1e-6
LNX_EPS = 64e-5
WKV_CHUNK = 64
WKV_BLOCK = 256
WKV_WAVE = 4
MOE_GATHER_ROWS = 1024
VMEM_LIMIT_BYTES = 56 * 1024 * 1024


def _params(*semantics):
    return pltpu.CompilerParams(dimension_semantics=semantics, vmem_limit_bytes=VMEM_LIMIT_BYTES)


def _tile(pref, *extents):
    tm = pref
    while any(n % tm for n in extents if n):
        tm //= 2
    return tm


def _dot(a, b):
    return jnp.dot(a, b, preferred_element_type=F32)


def _dot_nt(a, b):
    return lax.dot_general(a, b, (((1,), (1,)), ((), ())), preferred_element_type=F32)


def _dot_tn(a, b):
    return lax.dot_general(a, b, (((0,), (0,)), ((), ())), preferred_element_type=F32)


def _split2(x):
    hi = x.astype(BF16)
    lo = (x - hi.astype(F32)).astype(BF16)
    return hi, lo


def _split3(x):
    hi = x.astype(BF16)
    r1 = x - hi.astype(F32)
    mid = r1.astype(BF16)
    lo = (r1 - mid.astype(F32)).astype(BF16)
    return hi, mid, lo


def _norm_mod(x, g, sc, sh):
    ms = jnp.mean(x * x, axis=-1, keepdims=True)
    return x * lax.rsqrt(ms + NORM_EPS) * g * (1.0 + sc) + sh


def _sigmoid(x):
    return 1.0 / (1.0 + jnp.exp(-x))


def _seg_ones(n):
    r = lax.broadcasted_iota(jnp.int32, (n, n), 0) // HEAD_SIZE
    c = lax.broadcasted_iota(jnp.int32, (n, n), 1) // HEAD_SIZE
    return jnp.where(r == c, 1.0, 0.0).astype(BF16)


def _head_sum(x, ones):
    d = x.shape[-1]
    w = ones.shape[0]
    parts = []
    for j in range(d // w):
        hi, lo = _split2(x[:, j * w:(j + 1) * w])
        parts.append(_dot(hi, ones) + _dot(lo, ones))
    return parts[0] if len(parts) == 1 else jnp.concatenate(parts, axis=1)


def _adaln_kernel(c_ref, w_ref, b_ref, o_ref):
    c = c_ref[...]
    s = c * _sigmoid(c)
    s_hi, s_lo = _split2(s)
    w_hi, w_lo = _split2(w_ref[...])
    o_ref[...] = _dot(s_hi, w_hi) + _dot(s_lo, w_hi) + _dot(s_hi, w_lo) + b_ref[...]


def _adaln(cc, ada_w, ada_b):
    depth, d, n = ada_w.shape
    rows = cc.shape[0]
    tn = min(n, 1536)
    return pl.pallas_call(
        _adaln_kernel,
        out_shape=jax.ShapeDtypeStruct((depth, rows, n), F32),
        grid=(depth, n // tn),
        in_specs=[pl.BlockSpec((rows, d), lambda l, j: (0, 0)),
                  pl.BlockSpec((None, d, tn), lambda l, j: (l, 0, j)),
                  pl.BlockSpec((None, 1, tn), lambda l, j: (l, 0, j))],
        out_specs=pl.BlockSpec((None, rows, tn), lambda l, j: (l, 0, j)),
        compiler_params=_params("parallel", "parallel"),
        name="adaln",
    )(cc, ada_w, ada_b.reshape(depth, 1, n))


def _fnet_chan_kernel(x_ref, g_ref, sc_ref, sh_ref, cs_ref, o_ref):
    h = _norm_mod(x_ref[...], g_ref[...], sc_ref[...], sh_ref[...])
    gd = cs_ref.shape[0]
    cs = cs_ref[...]
    for j in range(h.shape[1] // gd):
        z = _dot(h[:, j * gd:(j + 1) * gd].astype(BF16), cs)
        o_ref[0, :, j * gd:(j + 1) * gd] = z[:, :gd].astype(BF16)
        o_ref[1, :, j * gd:(j + 1) * gd] = z[:, gd:].astype(BF16)


def _fnet_time_kernel(f_ref, hcs_ref, wo_ref, bo_ref, x_ref, gate_ref, o_ref):
    f = _dot(f_ref[...], hcs_ref[...])
    y = _dot(f.astype(BF16), wo_ref[...]) + bo_ref[...]
    o_ref[...] = x_ref[...] + gate_ref[...] * y


def _dft_mats(t, gd):
    def cs(n):
        i = jnp.arange(n, dtype=jnp.int32)
        ang = ((i[:, None] * i[None, :]) % n).astype(F32) * (2.0 * jnp.pi / n)
        return jnp.cos(ang), jnp.sin(ang)
    ct, st = cs(t)
    cc, sc = cs(gd)
    scale = 1.0 / jnp.sqrt(jnp.asarray(t * gd, F32))
    return (jnp.concatenate([ct, -st], axis=1).astype(BF16),
            (jnp.concatenate([cc, sc], axis=1) * scale).astype(BF16))


def _fnet_layer(x, norm_g, sc, sh, gate, wo_bf, bo, mats):
    bsz, t, d = x.shape
    f_mat, cs_mat = mats
    gd = d // FNET_GROUPS
    tm = min(t, 512)
    vec = lambda b, i: (b, 0, 0)
    hcs = pl.pallas_call(
        _fnet_chan_kernel,
        out_shape=jax.ShapeDtypeStruct((bsz, 2, t, d), BF16),
        grid=(bsz, t // tm),
        in_specs=[pl.BlockSpec((None, tm, d), lambda b, i: (b, i, 0)),
                  pl.BlockSpec((1, d), lambda b, i: (0, 0)),
                  pl.BlockSpec((None, 1, d), vec),
                  pl.BlockSpec((None, 1, d), vec),
                  pl.BlockSpec((gd, 2 * gd), lambda b, i: (0, 0))],
        out_specs=pl.BlockSpec((None, 2, tm, d), lambda b, i: (b, 0, i, 0)),
        compiler_params=_params("parallel", "parallel"),
        name="fnet_chan",
    )(x, norm_g.reshape(1, d), sc, sh, cs_mat)
    hcs = hcs.reshape(bsz, 2 * t, d)
    tm2 = min(t, 256)
    return pl.pallas_call(
        _fnet_time_kernel,
        out_shape=jax.ShapeDtypeStruct((bsz, t, d), F32),
        grid=(bsz, t // tm2),
        in_specs=[pl.BlockSpec((tm2, 2 * t), lambda b, i: (i, 0)),
                  pl.BlockSpec((None, 2 * t, d), lambda b, i: (b, 0, 0)),
                  pl.BlockSpec((d, d), lambda b, i: (0, 0)),
                  pl.BlockSpec((1, d), lambda b, i: (0, 0)),
                  pl.BlockSpec((None, tm2, d), lambda b, i: (b, i, 0)),
                  pl.BlockSpec((None, 1, d), vec)],
        out_specs=pl.BlockSpec((None, tm2, d), lambda b, i: (b, i, 0)),
        compiler_params=_params("parallel", "parallel"),
        name="fnet_time",
    )(f_mat, hcs, wo_bf, bo.reshape(1, d), x, gate)


def _shifted_grid(h, h_above, h_below):
    tm, d = h.shape
    q = d // 4
    colw = lax.broadcasted_iota(jnp.int32, (tm, q), 0) % GRID_W
    left = jnp.where(colw != 0, pltpu.roll(h[:, :q], 1, 0), 0.0)
    right = jnp.where(colw != GRID_W - 1, pltpu.roll(h[:, q:2 * q], tm - 1, 0), 0.0)
    up = jnp.concatenate([h_above[:, 2 * q:3 * q], h[:tm - GRID_W, 2 * q:3 * q]], axis=0)
    down = jnp.concatenate([h[GRID_W:, 3 * q:], h_below[:, 3 * q:]], axis=0)
    return jnp.concatenate([left, right, up, down], axis=1)


def _shifted_seq(h):
    tm, d = h.shape
    half = d // 2
    row = lax.broadcasted_iota(jnp.int32, (tm, half), 0)
    prev = jnp.where(row != 0, pltpu.roll(h[:, :half], 1, 0), 0.0)
    nxt = jnp.where(row != tm - 1, pltpu.roll(h[:, half:], tm - 1, 0), 0.0)
    return jnp.concatenate([prev, nxt], axis=1)


def _rwkv_proj_kernel(*refs, has_vres, grid_shift, n_alias):
    it = iter(refs)
    x_ref = next(it)
    xa_ref, xb_ref = (next(it), next(it)) if grid_shift else (None, None)
    ng_ref, sc_ref, sh_ref = next(it), next(it), next(it)
    vf_ref = next(it) if has_vres else None
    mix_ref, wr_ref, wk_ref, wv_ref = next(it), next(it), next(it), next(it)
    w0_ref, w1_ref, w2_ref = next(it), next(it), next(it)
    a0_ref, a1_ref, a2_ref = next(it), next(it), next(it)
    g1_ref, g2_ref, kk_ref, ka_ref = next(it), next(it), next(it), next(it)
    if has_vres:
        v0_ref, v1_ref, v2_ref = next(it), next(it), next(it)
    for _ in range(n_alias):
        next(it)
    r_out, v_out, g_out, a_out = next(it), next(it), next(it), next(it)
    lw_out, kd_out, bd_out = next(it), next(it), next(it)

    norm = lambda ref: _norm_mod(ref[...], ng_ref[...], sc_ref[...], sh_ref[...])
    h = norm(x_ref)
    if grid_shift:
        i = pl.program_id(1)
        h_above = jnp.where(i > 0, norm(xa_ref), 0.0)
        h_below = jnp.where(i < pl.num_programs(1) - 1, norm(xb_ref), 0.0)
        hs = _shifted_grid(h, h_above, h_below)
    else:
        hs = _shifted_seq(h)
    xx = hs - h
    xr, xw, xk, xv, xa, xg = [(h + xx * mix_ref[j:j + 1, :]).astype(BF16) for j in range(6)]
    r = _dot(xr, wr_ref[...])
    k = _dot(xk, wk_ref[...])
    v = _dot(xv, wv_ref[...])
    if has_vres:
        lora = _dot(_dot(xv, v1_ref[...]).astype(BF16), v2_ref[...])
        v = v + (vf_ref[...] - v) * _sigmoid(v0_ref[...] + lora)
    g = _dot(_sigmoid(_dot(xg, g1_ref[...])).astype(BF16), g2_ref[...])
    ones = _seg_ones(min(LANE_GROUP, h.shape[1]))
    kk = k * kk_ref[...]
    kk = kk * lax.rsqrt(_head_sum(kk * kk, ones) + 1e-12)
    r_out[...] = r.astype(BF16)
    v_out[...] = v
    g_out[...] = g.astype(BF16)
    a_out[...] = (-kk).astype(BF16)
    for d in range(2):
        wlog = w0_ref[d:d + 1, :] + _dot(jnp.tanh(_dot(xw, w1_ref[d])).astype(BF16), w2_ref[d])
        z = -wlog
        softplus = jnp.maximum(z, 0.0) + jnp.log(1.0 + jnp.exp(-jnp.abs(z)))
        lw_out[d] = -jnp.exp(-softplus - 0.5)
        a = _sigmoid(a0_ref[d:d + 1, :] + _dot(_dot(xa, a1_ref[d]).astype(BF16), a2_ref[d]))
        kd_out[d] = (k * (1.0 + (a - 1.0) * ka_ref[...])).astype(BF16)
        bd_out[d] = (kk * a).astype(BF16)


def _rwkv_project(x, norm_g, sc, sh, vfirst, p, tt, t_off, grid_shift, filled):
    bsz, t, d = x.shape
    tm = _tile(256, t, t_off)
    off = t_off // tm
    has_vres = vfirst is not None
    if not grid_shift:
        assert t == tm, "the sequence shift handles a stream that fits one tile"
    tok = pl.BlockSpec((None, tm, d), lambda b, i: (b, i + off, 0))
    tok2 = pl.BlockSpec((2, None, tm, d), lambda b, i: (0, b, i + off, 0))
    vec = pl.BlockSpec((None, 1, d), lambda b, i: (b, 0, 0))

    def full(a):
        nd = a.ndim
        return pl.BlockSpec(a.shape, lambda b, i: (0,) * nd)

    bf = lambda a: a.astype(BF16)
    weights = [p["mix"], bf(p["wr"]), bf(p["wk"]), bf(p["wv"]),
               p["w0"], bf(p["w1"]), bf(p["w2"]), p["a0"], bf(p["a1"]), bf(p["a2"]),
               bf(p["g1"]), bf(p["g2"]), p["kk"].reshape(1, d), p["ka"].reshape(1, d)]
    if has_vres:
        weights += [p["v0"].reshape(1, d), bf(p["v1"]), bf(p["v2"])]
    acts = [x]
    act_specs = [pl.BlockSpec((None, tm, d), lambda b, i: (b, i, 0))]
    if grid_shift:
        rpt = tm // GRID_W
        last = t // GRID_W - 1
        acts += [x, x]
        act_specs += [pl.BlockSpec((None, GRID_W, d), lambda b, i: (b, jnp.maximum(i * rpt - 1, 0), 0)),
                      pl.BlockSpec((None, GRID_W, d), lambda b, i: (b, jnp.minimum((i + 1) * rpt, last), 0))]
    acts += [norm_g.reshape(1, d), sc, sh]
    act_specs += [pl.BlockSpec((1, d), lambda b, i: (0, 0)), vec, vec]
    if has_vres:
        acts.append(vfirst)
        act_specs.append(tok)
    out_shape = [jax.ShapeDtypeStruct((bsz, tt, d), BF16),
                 jax.ShapeDtypeStruct((bsz, tt, d), F32),
                 jax.ShapeDtypeStruct((bsz, tt, d), BF16),
                 jax.ShapeDtypeStruct((bsz, tt, d), BF16),
                 jax.ShapeDtypeStruct((2, bsz, tt, d), F32),
                 jax.ShapeDtypeStruct((2, bsz, tt, d), BF16),
                 jax.ShapeDtypeStruct((2, bsz, tt, d), BF16)]
    filled = list(filled or [])
    n_in = len(acts) + len(weights)
    return pl.pallas_call(
        functools.partial(_rwkv_proj_kernel, has_vres=has_vres, grid_shift=grid_shift, n_alias=len(filled)),
        out_shape=out_shape,
        grid=(bsz, t // tm),
        in_specs=act_specs + [full(w) for w in weights] + [pl.BlockSpec(memory_space=pl.ANY)] * len(filled),
        out_specs=[tok, tok, tok, tok, tok2, tok2, tok2],
        input_output_aliases={n_in + j: j for j in range(len(filled))},
        compiler_params=_params("parallel", "parallel"),
        name="rwkv_proj",
    )(*acts, *weights, *filled)


def _wkv_kernel(r_ref, lw_ref, k_ref, v_ref, a_ref, b_ref, y_ref, s_scr, lhs_scr, n_scr, wl_scr, *, reverse):
    L = WKV_CHUNK
    tb, d = r_ref.shape
    nch = tb // L
    gw = s_scr.shape[-1]
    hpg = gw // HEAD_SIZE
    ng = d // gw

    @pl.when(pl.program_id(1) == 0)
    def _():
        s_scr[...] = jnp.zeros_like(s_scr)

    ti = lax.broadcasted_iota(jnp.int32, (L, L), 0)
    si = lax.broadcasted_iota(jnp.int32, (L, L), 1)
    tri = jnp.where((si >= ti) if reverse else (si <= ti), 1.0, 0.0).astype(BF16)
    last, mid = (0, L // 2) if reverse else (L - 1, L // 2 - 1)
    trow = lax.broadcasted_iota(jnp.int32, (L, gw), 0)
    scol = lax.broadcasted_iota(jnp.int32, (L, gw), 1) % HEAD_SIZE
    strict = (scol > trow) if reverse else (scol < trow)
    incl = (scol >= trow) if reverse else (scol <= trow)
    eye = jnp.where(scol == trow, 1.0, 0.0)
    bdm = (lax.broadcasted_iota(jnp.int32, (gw, gw), 0) // HEAD_SIZE
           == lax.broadcasted_iota(jnp.int32, (gw, gw), 1) // HEAD_SIZE)

    def bdiag(z):
        return jnp.where(bdm, jnp.concatenate([z] * hpg, axis=0), 0.0).astype(BF16)

    def per_head(f, lo):
        return jnp.concatenate([f[h * HEAD_SIZE:(h + 1) * HEAD_SIZE, lo:lo + L] for h in range(hpg)], axis=1)

    def compact(f):
        fm = jnp.where(bdm, f, 0.0)
        out = fm[:HEAD_SIZE]
        for h in range(1, hpg):
            out = out + fm[h * HEAD_SIZE:(h + 1) * HEAD_SIZE]
        return out

    def prep(j):
        rows = slice(j * L, (j + 1) * L)
        lw = lw_ref[rows, :]
        p1, p2, p3 = _split3(lw)
        cw = _dot(tri, p1) + _dot(tri, p2) + _dot(tri, p3)
        cw_end = cw[last:last + 1, :]
        cw_mid = cw[mid:mid + 1, :]
        r = r_ref[rows, :].astype(F32)
        k = k_ref[rows, :].astype(F32)
        a = a_ref[rows, :].astype(F32)
        b = b_ref[rows, :].astype(F32)
        e_in = jnp.exp(cw)
        e_ex = jnp.exp(cw - lw)
        e_inv = jnp.exp(cw_mid - cw)
        e_mid = jnp.exp(-cw_mid)
        e_end = e_inv * jnp.exp(cw_end - cw_mid)
        r0 = r * e_in
        a0 = a * e_ex
        return dict(rows=rows, lw=lw, r0=r0, a0=a0, ra=r0 * e_mid, aa=a0 * e_mid, kb=k * e_inv, bb=b * e_inv,
                    ke=k * e_end, be=b * e_end)

    for w0 in range(0, nch, WKV_WAVE):
        pj = {j: prep(j) for j in range(w0, min(nch, w0 + WKV_WAVE))}
        ch = [(j, slice(g * gw, (g + 1) * gw)) for j in pj for g in range(ng)]
        cs = range(len(ch))
        col = lambda name, c: pj[ch[c][0]][name][:, ch[c][1]]
        xq = [jnp.concatenate([col("aa", c), col("ra", c)], axis=0).astype(BF16) for c in cs]
        gb = [_dot_nt(xq[c], bdiag(col("bb", c))) for c in cs]
        gk = [_dot_nt(xq[c], bdiag(col("kb", c))) for c in cs]
        a_ab = [jnp.where(strict, gb[c][:L], 0.0) for c in cs]
        vg = [v_ref[pj[ch[c][0]]["rows"], ch[c][1]] for c in cs]
        ft = [jnp.concatenate([col("be", c), col("ke", c)], axis=0).T for c in cs]
        bet = [per_head(ft[c], 0) for c in cs]
        ket = [per_head(ft[c], L) for c in cs]
        akrk = [jnp.concatenate([jnp.where(strict, gk[c][:L], 0.0), jnp.where(incl, gk[c][L:], 0.0), ket[c]],
                                axis=0).astype(BF16) for c in cs]
        tmat = [eye + a_ab[c] for c in cs]
        pw = [_dot(a_ab[c].astype(BF16), bdiag(a_ab[c])) for c in cs]
        avk = [_dot(akrk[c], bdiag(vg[c])) for c in cs]
        n_lvl = L.bit_length() - 2
        for lvl in range(n_lvl):
            if lvl + 1 < n_lvl:
                z = [_dot(jnp.concatenate([tmat[c], pw[c]], axis=0).astype(BF16), bdiag(pw[c])) for c in cs]
                tmat = [tmat[c] + z[c][:L] for c in cs]
                pw = [z[c][L:] for c in cs]
            else:
                z = [_dot(tmat[c].astype(BF16), bdiag(pw[c])) for c in cs]
                tmat = [tmat[c] + z[c] for c in cs]
        rbe = [jnp.concatenate([jnp.where(incl, gb[c][L:], 0.0), bet[c]], axis=0).astype(BF16) for c in cs]
        rt = [_dot(rbe[c], bdiag(tmat[c])).astype(BF16) for c in cs]
        za = [_dot(rt[c], bdiag(col("a0", c))) for c in cs]
        zv = [_dot(rt[c], bdiag(avk[c][:L])) for c in cs]
        for c in cs:
            j, sl = ch[c]
            g = c % ng
            y_ref[pj[j]["rows"], sl] = zv[c][:L] + avk[c][L:2 * L]
            lhs_scr[j, g, :L, :] = (col("r0", c) + za[c][:L]).astype(BF16)
            lhs_scr[j, g, L:, :] = za[c][L:].astype(BF16)
            n_scr[j, g] = zv[c][L:] + avk[c][2 * L:]
            lw = col("lw", c)
            lwt = jnp.concatenate([lw, jnp.zeros_like(lw)], axis=0).T
            wcol = jnp.exp(jnp.sum(lwt, axis=1, keepdims=True))
            wl_scr[j, g] = compact(jnp.broadcast_to(wcol, (gw, gw)))

    for j in (reversed(range(nch)) if reverse else range(nch)):
        rows = slice(j * L, (j + 1) * L)
        s_old = [s_scr[g] for g in range(ng)]
        z = [_dot(lhs_scr[j, g], bdiag(s_old[g])) for g in range(ng)]
        for g in range(ng):
            y_ref[rows, g * gw:(g + 1) * gw] += z[g][:L]
            s_scr[g] = wl_scr[j, g] * s_old[g] + z[g][L:] + n_scr[j, g]


def _wkv_scan(r, lw, k, v, a, b, d_idx, n_ctx_blocks, reverse):
    bsz, t, d = r.shape
    L = WKV_CHUNK
    tb = _tile(WKV_BLOCK, t)
    nb = t // tb
    nch = tb // L
    gw = min(LANE_GROUP, d)
    ng = d // gw
    if reverse:
        cidx = lambda c: jnp.where(c < n_ctx_blocks, n_ctx_blocks - 1 - c, nb + n_ctx_blocks - 1 - c)
    else:
        cidx = lambda c: c
    tok = pl.BlockSpec((None, tb, d), lambda bb, c: (bb, cidx(c), 0))
    tokd = pl.BlockSpec((None, None, tb, d), lambda bb, c: (d_idx, bb, cidx(c), 0))
    return pl.pallas_call(
        functools.partial(_wkv_kernel, reverse=reverse),
        out_shape=jax.ShapeDtypeStruct((bsz, t, d), F32),
        grid=(bsz, nb),
        in_specs=[tok, tokd, tokd, tok, tok, tokd],
        out_specs=tok,
        scratch_shapes=[pltpu.VMEM((ng, HEAD_SIZE, gw), F32),
                        pltpu.VMEM((nch, ng, 2 * L, gw), BF16),
                        pltpu.VMEM((nch, ng, HEAD_SIZE, gw), F32),
                        pltpu.VMEM((nch, ng, HEAD_SIZE, gw), F32)],
        compiler_params=_params("parallel", "arbitrary"),
        name="wkv_rev" if reverse else "wkv_fwd",
    )(r, lw, k, v, a, b)


def _rwkv_out_kernel(y0_ref, y1_ref, r_ref, k_ref, v_ref, g_ref, rk_ref, lnw_ref, lnb_ref,
                     wo_ref, x_ref, gate_ref, o_ref):
    y = y0_ref[...] + y1_ref[...]
    ones = _seg_ones(min(LANE_GROUP, y.shape[1]))
    inv_n = 1.0 / HEAD_SIZE
    mu = _head_sum(y, ones) * inv_n
    yc = y - mu
    var = _head_sum(yc * yc, ones) * inv_n
    yn = yc * lax.rsqrt(var + LNX_EPS) * lnw_ref[...] + lnb_ref[...]
    ksum = k_ref[0].astype(F32) + k_ref[1].astype(F32)
    bonus = _head_sum(r_ref[...].astype(F32) * ksum * rk_ref[...], ones) * v_ref[...]
    out = ((yn + bonus) * g_ref[...].astype(F32)).astype(BF16)
    o_ref[...] = x_ref[...] + gate_ref[...] * _dot(out, wo_ref[...])


def _rwkv_output(y0, y1, r, kd, v, g, rk, lnw, lnb, wo_bf, x, gate, t_off):
    bsz, t, d = x.shape
    tm = _tile(256, t, t_off)
    off = t_off // tm
    tok = pl.BlockSpec((None, tm, d), lambda b, i: (b, i + off, 0))
    tok2 = pl.BlockSpec((2, None, tm, d), lambda b, i: (0, b, i + off, 0))
    row = pl.BlockSpec((1, d), lambda b, i: (0, 0))
    return pl.pallas_call(
        _rwkv_out_kernel,
        out_shape=jax.ShapeDtypeStruct((bsz, t, d), F32),
        grid=(bsz, t // tm),
        in_specs=[tok, tok, tok, tok2, tok, tok, row, row, row,
                  pl.BlockSpec((d, d), lambda b, i: (0, 0)),
                  pl.BlockSpec((None, tm, d), lambda b, i: (b, i, 0)),
                  pl.BlockSpec((None, 1, d), lambda b, i: (b, 0, 0))],
        out_specs=pl.BlockSpec((None, tm, d), lambda b, i: (b, i, 0)),
        compiler_params=_params("parallel", "parallel"),
        name="rwkv_out",
    )(y0, y1, r, kd, v, g, rk.reshape(1, d), lnw.reshape(1, d), lnb.reshape(1, d), wo_bf, x, gate)


def _moe_route_kernel(x_ref, g_ref, sc_ref, sh_ref, rt_ref, xs_ref, gs_ref, code_ref,
                      h_scr, aff_scr, *, cap):
    t, d = x_ref.shape
    n_e = rt_ref.shape[0]

    @pl.when(pl.program_id(1) == 0)
    def _():
        g = g_ref[...]
        sc = sc_ref[...]
        sh = sh_ref[...]
        rt_hi, rt_lo = _split2(rt_ref[...])
        tc = min(t, 256)
        for j in range(t // tc):
            h = _norm_mod(x_ref[j * tc:(j + 1) * tc, :], g, sc, sh)
            h_hi, h_lo = _split2(h)
            h_scr[j * tc:(j + 1) * tc, :] = h_hi
            aff_scr[:, j * tc:(j + 1) * tc] = (_dot_nt(rt_hi, h_hi) + _dot_nt(rt_hi, h_lo)
                                               + _dot_nt(rt_lo, h_hi))
        logits = aff_scr[...]
        m = jnp.max(logits, axis=0, keepdims=True)
        ex = jnp.exp(logits - m)
        aff = ex / jnp.sum(ex, axis=0, keepdims=True)
        aff_scr[...] = aff
        bits = pltpu.bitcast(aff, jnp.int32)

        def search(i, thr):
            cand = thr | (jnp.int32(1) << (30 - i))
            cnt = jnp.sum(jnp.where(bits >= cand, 1, 0), axis=1, keepdims=True)
            return jnp.where(cnt >= cap, cand, thr)

        thr = lax.fori_loop(0, 31, search, jnp.zeros((n_e, 1), jnp.int32))
        gt = bits > thr
        eq = bits == thr
        key = jnp.where(gt, 1, 0) + jnp.where(eq, 4096, 0)
        lane = lax.broadcasted_iota(jnp.int32, (n_e, t), 1)
        csum = key
        sh_amt = 1
        while sh_amt < t:
            csum = csum + jnp.where(lane >= sh_amt, pltpu.roll(csum, sh_amt, 1), 0)
            sh_amt *= 2
        before = csum - key
        n_gt = before & 4095
        n_eq = before >> 12
        need = cap - jnp.sum(jnp.where(gt, 1, 0), axis=1, keepdims=True)
        sel = gt | (eq & (n_eq < need))
        code_ref[...] = jnp.where(sel, n_gt + jnp.minimum(n_eq, need), -1)

    eg = xs_ref.shape[0]
    e0 = pl.program_id(1) * eg
    slot = lax.broadcasted_iota(jnp.int32, (cap, t), 0)
    onehot = []
    for l in range(eg):
        hit = code_ref[pl.ds(e0 + l, 1), :] == slot
        onehot.append(jnp.where(hit, 1.0, 0.0).astype(BF16))
        gs_ref[l] = jnp.sum(jnp.where(hit, aff_scr[pl.ds(e0 + l, 1), :], 0.0), axis=1, keepdims=True)
    gathered = _dot(onehot[0] if eg == 1 else jnp.concatenate(onehot, axis=0), h_scr[...])
    xs_ref[...] = gathered.reshape(eg, cap, d).astype(BF16)


def _moe_route(x, norm_g, sc, sh, router):
    bsz, t, d = x.shape
    n_e = router.shape[1]
    cap = CAPACITY_FACTOR * t // n_e
    eg = _tile(max(1, MOE_GATHER_ROWS // cap), n_e)
    vec = lambda b, e: (b, 0, 0)
    return pl.pallas_call(
        functools.partial(_moe_route_kernel, cap=cap),
        out_shape=[jax.ShapeDtypeStruct((bsz, n_e, cap, d), BF16),
                   jax.ShapeDtypeStruct((bsz, n_e, cap, 1), F32),
                   jax.ShapeDtypeStruct((bsz, n_e, t), jnp.int32)],
        grid=(bsz, n_e // eg),
        in_specs=[pl.BlockSpec((None, t, d), lambda b, e: (b, 0, 0)),
                  pl.BlockSpec((1, d), lambda b, e: (0, 0)),
                  pl.BlockSpec((None, 1, d), vec),
                  pl.BlockSpec((None, 1, d), vec),
                  pl.BlockSpec((n_e, d), lambda b, e: (0, 0))],
        out_specs=[pl.BlockSpec((None, eg, cap, d), lambda b, e: (b, e, 0, 0)),
                   pl.BlockSpec((None, eg, cap, 1), lambda b, e: (b, e, 0, 0)),
                   pl.BlockSpec((None, n_e, t), lambda b, e: (b, 0, 0))],
        scratch_shapes=[pltpu.VMEM((t, d), BF16), pltpu.VMEM((n_e, t), F32)],
        compiler_params=_params("parallel", "arbitrary"),
        name="moe_route",
    )(x, norm_g.reshape(1, d), sc, sh, router.T)


def _moe_ffn_kernel(xs_ref, gs_ref, wg_ref, wu_ref, wd_ref, ys_ref, wg_scr, wu_scr, wd_scr):
    @pl.when(pl.program_id(1) == 0)
    def _():
        wg_scr[...] = wg_ref[...].astype(BF16)
        wu_scr[...] = wu_ref[...].astype(BF16)
        wd_scr[...] = wd_ref[...].astype(BF16)

    tb, cap, d = xs_ref.shape
    xs = xs_ref[...].reshape(tb * cap, d)
    hg = _dot(xs, wg_scr[...])
    hu = _dot(xs, wu_scr[...])
    hid = (hg * _sigmoid(hg) * hu).astype(BF16)
    ys = _dot(hid, wd_scr[...]) * gs_ref[...].reshape(tb * cap, 1)
    ys_ref[...] = ys.reshape(tb, cap, d).astype(BF16)


def _moe_ffn(xs, gs, wg, wu, wd, layer):
    bsz, n_e, cap, d = xs.shape
    f = wg.shape[-1]
    tb = max(1, min(bsz, 512 // cap))
    while bsz % tb:
        tb -= 1
    return pl.pallas_call(
        _moe_ffn_kernel,
        out_shape=jax.ShapeDtypeStruct((bsz, n_e, cap, d), BF16),
        grid=(n_e, bsz // tb),
        in_specs=[pl.BlockSpec((tb, None, cap, d), lambda e, j: (j, e, 0, 0)),
                  pl.BlockSpec((tb, None, cap, 1), lambda e, j: (j, e, 0, 0)),
                  pl.BlockSpec((None, None, d, f), lambda e, j: (layer, e, 0, 0)),
                  pl.BlockSpec((None, None, d, f), lambda e, j: (layer, e, 0, 0)),
                  pl.BlockSpec((None, None, f, d), lambda e, j: (layer, e, 0, 0))],
        out_specs=pl.BlockSpec((tb, None, cap, d), lambda e, j: (j, e, 0, 0)),
        scratch_shapes=[pltpu.VMEM((d, f), BF16), pltpu.VMEM((d, f), BF16), pltpu.VMEM((f, d), BF16)],
        compiler_params=_params("parallel", "arbitrary"),
        name="moe_ffn",
    )(xs, gs, wg, wu, wd)


def _moe_combine_kernel(code_ref, ys_ref, x_ref, gate_ref, fg_ref, o_ref, *, cap, final_norm):
    n_e = code_ref.shape[1]
    slot = lax.broadcasted_iota(jnp.int32, (1, cap), 1)
    pieces = [jnp.where(code_ref[:, e:e + 1] == slot, 1.0, 0.0).astype(BF16) for e in range(n_e)]
    scat = jnp.concatenate(pieces, axis=1)
    x = x_ref[...] + gate_ref[...] * _dot(scat, ys_ref[...])
    if final_norm:
        x = x * lax.rsqrt(jnp.mean(x * x, axis=-1, keepdims=True) + NORM_EPS) * fg_ref[...]
    o_ref[...] = x


def _moe_combine(code_t, ys, x, gate, final_g, final_norm):
    bsz, t, d = x.shape
    n_e = code_t.shape[-1]
    cap = ys.shape[1] // n_e
    tm = _tile(1024, t)
    return pl.pallas_call(
        functools.partial(_moe_combine_kernel, cap=cap, final_norm=final_norm),
        out_shape=jax.ShapeDtypeStruct((bsz, t, d), F32),
        grid=(bsz, t // tm),
        in_specs=[pl.BlockSpec((None, tm, n_e), lambda b, i: (b, i, 0)),
                  pl.BlockSpec((None, n_e * cap, d), lambda b, i: (b, 0, 0)),
                  pl.BlockSpec((None, tm, d), lambda b, i: (b, i, 0)),
                  pl.BlockSpec((None, 1, d), lambda b, i: (b, 0, 0)),
                  pl.BlockSpec((1, d), lambda b, i: (0, 0))],
        out_specs=pl.BlockSpec((None, tm, d), lambda b, i: (b, i, 0)),
        compiler_params=_params("parallel", "parallel"),
        name="moe_combine",
    )(code_t, ys, x, gate, final_g.reshape(1, d))


def _ec_moe_layer(x, norm_g, sc, sh, gate, router, wg, wu, wd, layer, final_g, final_norm):
    bsz, t, d = x.shape
    xs, gs, code = _moe_route(x, norm_g, sc, sh, router)
    ys = _moe_ffn(xs, gs, wg, wu, wd, layer)
    n_e, cap = xs.shape[1], xs.shape[2]
    return _moe_combine(jnp.swapaxes(code, 1, 2), ys.reshape(bsz, n_e * cap, d), x, gate,
                        final_g, final_norm)


def kernel(x, c, ctx, c_ctx, ada_w, ada_b, norm_g, fnet_wo, fnet_bo, rw_mix, rw_wr, rw_wk, rw_wv, rw_wo,
           rw_w0, rw_w1, rw_w2, rw_a0, rw_a1, rw_a2, rw_v0, rw_v1, rw_v2, rw_g1, rw_g2, rw_kk, rw_ka, rw_rk,
           rw_lnx_w, rw_lnx_b, moe_router, moe_wg, moe_wu, moe_wd, final_g):
    bsz, t, d = x.shape
    t_ctx = ctx.shape[1]
    depth = ada_w.shape[0]
    n_mixers = 2

    rows = -(-(bsz + 1) // 8) * 8
    cc = jnp.concatenate([c, c_ctx[None, :], jnp.zeros((rows - bsz - 1, d), F32)], axis=0)
    mods = _adaln(cc, ada_w, ada_b)

    def mod_x(i, j):
        return mods[i, :bsz, j * d:(j + 1) * d].reshape(bsz, 1, d)

    def mod_c(i, j):
        return jnp.broadcast_to(mods[i, bsz, j * d:(j + 1) * d].reshape(1, 1, d), (bsz, 1, d))

    mats_x = _dft_mats(t, d // FNET_GROUPS)
    mats_c = _dft_mats(t_ctx, d // FNET_GROUPS)
    vfirst = None
    for i in range(depth):
        need_ctx = i < depth - 1
        if i % n_mixers == 0:
            fi = i // n_mixers
            wo_bf = fnet_wo[fi].astype(BF16)
            x = _fnet_layer(x, norm_g[i, 0], mod_x(i, 1), mod_x(i, 0), mod_x(i, 2), wo_bf, fnet_bo[fi], mats_x)
            if need_ctx:
                ctx = _fnet_layer(ctx, norm_g[i, 0], mod_c(i, 1), mod_c(i, 0), mod_c(i, 2), wo_bf,
                                  fnet_bo[fi], mats_c)
        else:
            ri = i // n_mixers
            p = dict(mix=rw_mix[ri], wr=rw_wr[ri], wk=rw_wk[ri], wv=rw_wv[ri], w0=rw_w0[ri], w1=rw_w1[ri],
                     w2=rw_w2[ri], a0=rw_a0[ri], a1=rw_a1[ri], a2=rw_a2[ri], g1=rw_g1[ri], g2=rw_g2[ri],
                     kk=rw_kk[ri], ka=rw_ka[ri])
            if ri > 0:
                p.update(v0=rw_v0[ri - 1], v1=rw_v1[ri - 1], v2=rw_v2[ri - 1])
            vf = vfirst if ri > 0 else None
            tt = t_ctx + t
            part = _rwkv_project(ctx, norm_g[i, 0], mod_c(i, 1), mod_c(i, 0), vf, p, tt, 0, False, None)
            r, v, g, a, lw, kd, bd = _rwkv_project(x, norm_g[i, 0], mod_x(i, 1), mod_x(i, 0), vf, p, tt,
                                                   t_ctx, True, part)
            if ri == 0:
                vfirst = v
            ncb = t_ctx // _tile(WKV_BLOCK, t_ctx + t)
            y0 = _wkv_scan(r, lw, kd, v, a, bd, 0, ncb, False)
            y1 = _wkv_scan(r, lw, kd, v, a, bd, 1, ncb, True)
            wo_bf = rw_wo[ri].astype(BF16)
            args = (y0, y1, r, kd, v, g, rw_rk[ri], rw_lnx_w[ri], rw_lnx_b[ri], wo_bf)
            x = _rwkv_output(*args, x, mod_x(i, 2), t_ctx)
            if need_ctx:
                ctx = _rwkv_output(*args, ctx, mod_c(i, 2), 0)
        last = i == depth - 1
        x = _ec_moe_layer(x, norm_g[i, 1], mod_x(i, 4), mod_x(i, 3), mod_x(i, 5), moe_router[i],
                          moe_wg, moe_wu, moe_wd, i, final_g, last)
        if need_ctx:
            ctx = _ec_moe_layer(ctx, norm_g[i, 1], mod_c(i, 4), mod_c(i, 3), mod_c(i, 5), moe_router[i],
                                moe_wg, moe_wu, moe_wd, i, final_g, False)
    return x
```

```python
import functools

import jax
import jax.numpy as jnp
from jax import lax
from jax.experimental import pallas as pl
from jax.experimental.pallas import tpu as pltpu

F32 = jnp.float32
BF16 = jnp.bfloat16

HEAD_SIZE = 64
LANE_GROUP = 256
GRID_W = 64
FNET_GROUPS = 4
CAPACITY_FACTOR = 2
NORM_EPS = 1e-6
LNX_EPS = 64e-5
DECAY_SCALE = 0.6065306597126334
WKV_CHUNK = 64
WKV_BLOCK = 256
WKV_WAVE = 4
MOE_GATHER_ROWS = 1024
VMEM_LIMIT_BYTES = 56 * 1024 * 1024


def _params(*semantics):
    return pltpu.CompilerParams(dimension_semantics=semantics, vmem_limit_bytes=VMEM_LIMIT_BYTES)


def _tile(pref, *extents):
    tm = pref
    while any(n % tm for n in extents if n):
        tm //= 2
    return tm


def _dot(a, b):
    return jnp.dot(a, b, preferred_element_type=F32)


def _dot_nt(a, b):
    return lax.dot_general(a, b, (((1,), (1,)), ((), ())), preferred_element_type=F32)


def _dot_tn(a, b):
    return lax.dot_general(a, b, (((0,), (0,)), ((), ())), preferred_element_type=F32)


def _split2(x):
    hi = x.astype(BF16)
    lo = (x - hi.astype(F32)).astype(BF16)
    return hi, lo


def _split3(x):
    hi = x.astype(BF16)
    r1 = x - hi.astype(F32)
    mid = r1.astype(BF16)
    lo = (r1 - mid.astype(F32)).astype(BF16)
    return hi, mid, lo


def _norm_mod(x, g, sc, sh):
    ms = jnp.mean(x * x, axis=-1, keepdims=True)
    return x * lax.rsqrt(ms + NORM_EPS) * g * (1.0 + sc) + sh


def _sigmoid(x):
    return 1.0 / (1.0 + jnp.exp(-x))


def _seg_ones(n):
    r = lax.broadcasted_iota(jnp.int32, (n, n), 0) // HEAD_SIZE
    c = lax.broadcasted_iota(jnp.int32, (n, n), 1) // HEAD_SIZE
    return jnp.where(r == c, 1.0, 0.0).astype(BF16)


def _head_sum(x, ones):
    d = x.shape[-1]
    w = ones.shape[0]
    parts = []
    for j in range(d // w):
        hi, lo = _split2(x[:, j * w:(j + 1) * w])
        parts.append(_dot(hi, ones) + _dot(lo, ones))
    return parts[0] if len(parts) == 1 else jnp.concatenate(parts, axis=1)


def _adaln_kernel(c_ref, w_ref, b_ref, o_ref):
    c = c_ref[...]
    s = c * _sigmoid(c)
    s_hi, s_lo = _split2(s)
    w_hi, w_lo = _split2(w_ref[...])
    o_ref[...] = _dot(s_hi, w_hi) + _dot(s_lo, w_hi) + _dot(s_hi, w_lo) + b_ref[...]


def _adaln(cc, ada_w, ada_b):
    depth, d, n = ada_w.shape
    rows = cc.shape[0]
    tn = min(n, 1536)
    return pl.pallas_call(
        _adaln_kernel,
        out_shape=jax.ShapeDtypeStruct((depth, rows, n), F32),
        grid=(depth, n // tn),
        in_specs=[pl.BlockSpec((rows, d), lambda l, j: (0, 0)),
                  pl.BlockSpec((None, d, tn), lambda l, j: (l, 0, j)),
                  pl.BlockSpec((None, 1, tn), lambda l, j: (l, 0, j))],
        out_specs=pl.BlockSpec((None, rows, tn), lambda l, j: (l, 0, j)),
        compiler_params=_params("parallel", "parallel"),
        name="adaln",
    )(cc, ada_w, ada_b.reshape(depth, 1, n))


def _fnet_chan_kernel(x_ref, g_ref, sc_ref, sh_ref, cs_ref, o_ref):
    h = _norm_mod(x_ref[...], g_ref[...], sc_ref[...], sh_ref[...])
    gd = cs_ref.shape[0]
    cs = cs_ref[...]
    for j in range(h.shape[1] // gd):
        z = _dot(h[:, j * gd:(j + 1) * gd].astype(BF16), cs)
        o_ref[0, :, j * gd:(j + 1) * gd] = z[:, :gd].astype(BF16)
        o_ref[1, :, j * gd:(j + 1) * gd] = z[:, gd:].astype(BF16)


def _fnet_time_kernel(f_ref, hcs_ref, wo_ref, bo_ref, x_ref, gate_ref, o_ref):
    f = _dot(f_ref[...], hcs_ref[...])
    y = _dot(f.astype(BF16), wo_ref[...]) + bo_ref[...]
    o_ref[...] = x_ref[...] + gate_ref[...] * y


def _dft_mats(t, gd):
    def cs(n):
        i = jnp.arange(n, dtype=jnp.int32)
        ang = ((i[:, None] * i[None, :]) % n).astype(F32) * (2.0 * jnp.pi / n)
        return jnp.cos(ang), jnp.sin(ang)
    ct, st = cs(t)
    cc, sc = cs(gd)
    scale = 1.0 / jnp.sqrt(jnp.asarray(t * gd, F32))
    return (jnp.concatenate([ct, -st], axis=1).astype(BF16),
            (jnp.concatenate([cc, sc], axis=1) * scale).astype(BF16))


def _fnet_layer(x, norm_g, sc, sh, gate, wo_bf, bo, mats):
    bsz, t, d = x.shape
    f_mat, cs_mat = mats
    gd = d // FNET_GROUPS
    tm = min(t, 512)
    vec = lambda b, i: (b, 0, 0)
    hcs = pl.pallas_call(
        _fnet_chan_kernel,
        out_shape=jax.ShapeDtypeStruct((bsz, 2, t, d), BF16),
        grid=(bsz, t // tm),
        in_specs=[pl.BlockSpec((None, tm, d), lambda b, i: (b, i, 0)),
                  pl.BlockSpec((1, d), lambda b, i: (0, 0)),
                  pl.BlockSpec((None, 1, d), vec),
                  pl.BlockSpec((None, 1, d), vec),
                  pl.BlockSpec((gd, 2 * gd), lambda b, i: (0, 0))],
        out_specs=pl.BlockSpec((None, 2, tm, d), lambda b, i: (b, 0, i, 0)),
        compiler_params=_params("parallel", "parallel"),
        name="fnet_chan",
    )(x, norm_g.reshape(1, d), sc, sh, cs_mat)
    hcs = hcs.reshape(bsz, 2 * t, d)
    tm2 = min(t, 256)
    return pl.pallas_call(
        _fnet_time_kernel,
        out_shape=jax.ShapeDtypeStruct((bsz, t, d), F32),
        grid=(bsz, t // tm2),
        in_specs=[pl.BlockSpec((tm2, 2 * t), lambda b, i: (i, 0)),
                  pl.BlockSpec((None, 2 * t, d), lambda b, i: (b, 0, 0)),
                  pl.BlockSpec((d, d), lambda b, i: (0, 0)),
                  pl.BlockSpec((1, d), lambda b, i: (0, 0)),
                  pl.BlockSpec((None, tm2, d), lambda b, i: (b, i, 0)),
                  pl.BlockSpec((None, 1, d), vec)],
        out_specs=pl.BlockSpec((None, tm2, d), lambda b, i: (b, i, 0)),
        compiler_params=_params("parallel", "parallel"),
        name="fnet_time",
    )(f_mat, hcs, wo_bf, bo.reshape(1, d), x, gate)


def _shifted_grid(h, h_above, h_below):
    tm, d = h.shape
    q = d // 4
    colw = lax.broadcasted_iota(jnp.int32, (tm, q), 0) % GRID_W
    left = jnp.where(colw != 0, pltpu.roll(h[:, :q], 1, 0), 0.0)
    right = jnp.where(colw != GRID_W - 1, pltpu.roll(h[:, q:2 * q], tm - 1, 0), 0.0)
    up = jnp.concatenate([h_above[:, 2 * q:3 * q], h[:tm - GRID_W, 2 * q:3 * q]], axis=0)
    down = jnp.concatenate([h[GRID_W:, 3 * q:], h_below[:, 3 * q:]], axis=0)
    return jnp.concatenate([left, right, up, down], axis=1)


def _shifted_seq(h):
    tm, d = h.shape
    half = d // 2
    row = lax.broadcasted_iota(jnp.int32, (tm, half), 0)
    prev = jnp.where(row != 0, pltpu.roll(h[:, :half], 1, 0), 0.0)
    nxt = jnp.where(row != tm - 1, pltpu.roll(h[:, half:], tm - 1, 0), 0.0)
    return jnp.concatenate([prev, nxt], axis=1)


def _rwkv_proj_kernel(*refs, has_vres, n_ctx_tiles):
    it = iter(refs)
    c_ref, x_ref, xa_ref, xb_ref = next(it), next(it), next(it), next(it)
    ng_ref, scc_ref, shc_ref, scx_ref, shx_ref = next(it), next(it), next(it), next(it), next(it)
    vf_ref = next(it) if has_vres else None
    mix_ref, wr_ref, wk_ref, wv_ref = next(it), next(it), next(it), next(it)
    w0_ref, w1_ref, w2_ref = next(it), next(it), next(it)
    a0_ref, a1_ref, a2_ref = next(it), next(it), next(it)
    g1_ref, g2_ref, kk_ref, ka_ref = next(it), next(it), next(it), next(it)
    if has_vres:
        v0_ref, v1_ref, v2_ref = next(it), next(it), next(it)
    r_out, v_out, g_out, a_out = next(it), next(it), next(it), next(it)
    lw_out, kd_out, bd_out = next(it), next(it), next(it)

    i = pl.program_id(1)
    is_ctx = i < n_ctx_tiles
    ng = ng_ref[...]
    scx, shx = scx_ref[...], shx_ref[...]
    h = _norm_mod(jnp.where(is_ctx, c_ref[...], x_ref[...]), ng,
                  jnp.where(is_ctx, scc_ref[...], scx), jnp.where(is_ctx, shc_ref[...], shx))
    h_above = jnp.where(i > n_ctx_tiles, _norm_mod(xa_ref[...], ng, scx, shx), 0.0)
    h_below = jnp.where(i < pl.num_programs(1) - 1, _norm_mod(xb_ref[...], ng, scx, shx), 0.0)
    hs = jnp.where(is_ctx, _shifted_seq(h), _shifted_grid(h, h_above, h_below))
    xx = hs - h
    xr, xw, xk, xv, xa, xg = [(h + xx * mix_ref[j:j + 1, :]).astype(BF16) for j in range(6)]
    r = _dot(xr, wr_ref[...])
    k = _dot(xk, wk_ref[...])
    v = _dot(xv, wv_ref[...])
    if has_vres:
        lora = _dot(_dot(xv, v1_ref[...]).astype(BF16), v2_ref[...])
        v = v + (vf_ref[...] - v) * _sigmoid(v0_ref[...] + lora)
    g = _dot(_sigmoid(_dot(xg, g1_ref[...])).astype(BF16), g2_ref[...])
    ones = _seg_ones(min(LANE_GROUP, h.shape[1]))
    kk = k * kk_ref[...]
    kk = kk * lax.rsqrt(_head_sum(kk * kk, ones) + 1e-12)
    r_out[...] = r.astype(BF16)
    v_out[...] = v
    g_out[...] = g.astype(BF16)
    a_out[...] = (-kk).astype(BF16)
    for d in range(2):
        wlog = w0_ref[d:d + 1, :] + _dot(jnp.tanh(_dot(xw, w1_ref[d])).astype(BF16), w2_ref[d])
        lw_out[d] = -DECAY_SCALE * _sigmoid(wlog)
        a = _sigmoid(a0_ref[d:d + 1, :] + _dot(_dot(xa, a1_ref[d]).astype(BF16), a2_ref[d]))
        kd_out[d] = (k * (1.0 + (a - 1.0) * ka_ref[...])).astype(BF16)
        bd_out[d] = (kk * a).astype(BF16)


def _rwkv_project(ctx, x, norm_g, mod_c, mod_x, vfirst, p):
    bsz, t, d = x.shape
    t_ctx = ctx.shape[1]
    tm = _tile(256, t, t_ctx)
    nct = t_ctx // tm
    assert nct == 1, "the sequence shift handles a context that fits one tile"
    ntx = t // tm
    has_vres = vfirst is not None
    tok = pl.BlockSpec((None, tm, d), lambda b, i: (b, i, 0))
    tok2 = pl.BlockSpec((2, None, tm, d), lambda b, i: (0, b, i, 0))
    vec = pl.BlockSpec((None, 1, d), lambda b, i: (b, 0, 0))

    def full(a):
        nd = a.ndim
        return pl.BlockSpec(a.shape, lambda b, i: (0,) * nd)

    bf = lambda a: a.astype(BF16)
    weights = [p["mix"], bf(p["wr"]), bf(p["wk"]), bf(p["wv"]),
               p["w0"], bf(p["w1"]), bf(p["w2"]), p["a0"], bf(p["a1"]), bf(p["a2"]),
               bf(p["g1"]), bf(p["g2"]), p["kk"].reshape(1, d), p["ka"].reshape(1, d)]
    if has_vres:
        weights += [p["v0"].reshape(1, d), bf(p["v1"]), bf(p["v2"])]
    rpt = tm // GRID_W
    last = t // GRID_W - 1
    acts = [ctx, x, x, x, norm_g.reshape(1, d), *mod_c, *mod_x]
    act_specs = [pl.BlockSpec((None, tm, d), lambda b, i: (b, jnp.minimum(i, nct - 1), 0)),
                 pl.BlockSpec((None, tm, d), lambda b, i: (b, jnp.maximum(i - nct, 0), 0)),
                 pl.BlockSpec((None, GRID_W, d), lambda b, i: (b, jnp.maximum((i - nct) * rpt - 1, 0), 0)),
                 pl.BlockSpec((None, GRID_W, d), lambda b, i: (b, jnp.clip((i - nct + 1) * rpt, 0, last), 0)),
                 pl.BlockSpec((1, d), lambda b, i: (0, 0)), vec, vec, vec, vec]
    if has_vres:
        acts.append(vfirst)
        act_specs.append(tok)
    tt = t_ctx + t
    out_shape = [jax.ShapeDtypeStruct((bsz, tt, d), BF16),
                 jax.ShapeDtypeStruct((bsz, tt, d), F32),
                 jax.ShapeDtypeStruct((bsz, tt, d), BF16),
                 jax.ShapeDtypeStruct((bsz, tt, d), BF16),
                 jax.ShapeDtypeStruct((2, bsz, tt, d), F32),
                 jax.ShapeDtypeStruct((2, bsz, tt, d), BF16),
                 jax.ShapeDtypeStruct((2, bsz, tt, d), BF16)]
    return pl.pallas_call(
        functools.partial(_rwkv_proj_kernel, has_vres=has_vres, n_ctx_tiles=nct),
        out_shape=out_shape,
        grid=(bsz, nct + ntx),
        in_specs=act_specs + [full(w) for w in weights],
        out_specs=[tok, tok, tok, tok, tok2, tok2, tok2],
        compiler_params=_params("parallel", "parallel"),
        name="rwkv_proj",
    )(*acts, *weights)


def _wkv_kernel(r_ref, lw_ref, k_ref, v_ref, a_ref, b_ref, y_ref, s_scr, lhs_scr, n_scr, wl_scr, *, reverse):
    L = WKV_CHUNK
    tb, d = r_ref.shape
    nch = tb // L
    gw = s_scr.shape[-1]
    hpg = gw // HEAD_SIZE
    ng = d // gw

    @pl.when(pl.program_id(1) == 0)
    def _():
        s_scr[...] = jnp.zeros_like(s_scr)

    ti = lax.broadcasted_iota(jnp.int32, (L, L), 0)
    si = lax.broadcasted_iota(jnp.int32, (L, L), 1)
    tri = jnp.where((si >= ti) if reverse else (si <= ti), 1.0, 0.0).astype(BF16)
    last, mid = (0, L // 2) if reverse else (L - 1, L // 2 - 1)
    trow = lax.broadcasted_iota(jnp.int32, (L, gw), 0)
    scol = lax.broadcasted_iota(jnp.int32, (L, gw), 1) % HEAD_SIZE
    strict = (scol > trow) if reverse else (scol < trow)
    incl = (scol >= trow) if reverse else (scol <= trow)
    eye = jnp.where(scol == trow, 1.0, 0.0)
    bdm = (lax.broadcasted_iota(jnp.int32, (gw, gw), 0) // HEAD_SIZE
           == lax.broadcasted_iota(jnp.int32, (gw, gw), 1) // HEAD_SIZE)

    def bdiag(z):
        return jnp.where(bdm, jnp.concatenate([z] * hpg, axis=0), 0.0).astype(BF16)

    def per_head(f, lo):
        return jnp.concatenate([f[h * HEAD_SIZE:(h + 1) * HEAD_SIZE, lo:lo + L] for h in range(hpg)], axis=1)

    def compact(f):
        fm = jnp.where(bdm, f, 0.0)
        out = fm[:HEAD_SIZE]
        for h in range(1, hpg):
            out = out + fm[h * HEAD_SIZE:(h + 1) * HEAD_SIZE]
        return out

    def prep(j):
        rows = slice(j * L, (j + 1) * L)
        lw = lw_ref[rows, :]
        p1, p2, p3 = _split3(lw)
        cw = _dot(tri, p1) + _dot(tri, p2) + _dot(tri, p3)
        cw_end = cw[last:last + 1, :]
        cw_mid = cw[mid:mid + 1, :]
        r = r_ref[rows, :].astype(F32)
        k = k_ref[rows, :].astype(F32)
        a = a_ref[rows, :].astype(F32)
        b = b_ref[rows, :].astype(F32)
        e_in = jnp.exp(cw)
        e_ex = jnp.exp(cw - lw)
        e_inv = jnp.exp(cw_mid - cw)
        e_mid = jnp.exp(-cw_mid)
        e_end = e_inv * jnp.exp(cw_end - cw_mid)
        r0 = r * e_in
        a0 = a * e_ex
        return dict(rows=rows, lw=lw, r0=r0, a0=a0, ra=r0 * e_mid, aa=a0 * e_mid, kb=k * e_inv, bb=b * e_inv,
                    ke=k * e_end, be=b * e_end)

    for w0 in range(0, nch, WKV_WAVE):
        pj = {j: prep(j) for j in range(w0, min(nch, w0 + WKV_WAVE))}
        ch = [(j, slice(g * gw, (g + 1) * gw)) for j in pj for g in range(ng)]
        cs = range(len(ch))
        col = lambda name, c: pj[ch[c][0]][name][:, ch[c][1]]
        xq = [jnp.concatenate([col("aa", c), col("ra", c)], axis=0).astype(BF16) for c in cs]
        gb = [_dot_nt(xq[c], bdiag(col("bb", c))) for c in cs]
        gk = [_dot_nt(xq[c], bdiag(col("kb", c))) for c in cs]
        a_ab = [jnp.where(strict, gb[c][:L], 0.0) for c in cs]
        vg = [v_ref[pj[ch[c][0]]["rows"], ch[c][1]] for c in cs]
        ft = [jnp.concatenate([col("be", c), col("ke", c)], axis=0).T for c in cs]
        bet = [per_head(ft[c], 0) for c in cs]
        ket = [per_head(ft[c], L) for c in cs]
        akrk = [jnp.concatenate([jnp.where(strict, gk[c][:L], 0.0), jnp.where(incl, gk[c][L:], 0.0), ket[c]],
                                axis=0).astype(BF16) for c in cs]
        tmat = [eye + a_ab[c] for c in cs]
        pw = [_dot(a_ab[c].astype(BF16), bdiag(a_ab[c])) for c in cs]
        avk = [_dot(akrk[c], bdiag(vg[c])) for c in cs]
        n_lvl = L.bit_length() - 2
        for lvl in range(n_lvl):
            if lvl + 1 < n_lvl:
                z = [_dot(jnp.concatenate([tmat[c], pw[c]], axis=0).astype(BF16), bdiag(pw[c])) for c in cs]
                tmat = [tmat[c] + z[c][:L] for c in cs]
                pw = [z[c][L:] for c in cs]
            else:
                z = [_dot(tmat[c].astype(BF16), bdiag(pw[c])) for c in cs]
                tmat = [tmat[c] + z[c] for c in cs]
        rbe = [jnp.concatenate([jnp.where(incl, gb[c][L:], 0.0), bet[c]], axis=0).astype(BF16) for c in cs]
        rt = [_dot(rbe[c], bdiag(tmat[c])).astype(BF16) for c in cs]
        za = [_dot(rt[c], bdiag(col("a0", c))) for c in cs]
        zv = [_dot(rt[c], bdiag(avk[c][:L])) for c in cs]
        for c in cs:
            j, sl = ch[c]
            g = c % ng
            y_ref[pj[j]["rows"], sl] = zv[c][:L] + avk[c][L:2 * L]
            lhs_scr[j, g, :L, :] = (col("r0", c) + za[c][:L]).astype(BF16)
            lhs_scr[j, g, L:, :] = za[c][L:].astype(BF16)
            n_scr[j, g] = zv[c][L:] + avk[c][2 * L:]
            lw = col("lw", c)
            lwt = jnp.concatenate([lw, jnp.zeros_like(lw)], axis=0).T
            wcol = jnp.exp(jnp.sum(lwt, axis=1, keepdims=True))
            wl_scr[j, g] = compact(jnp.broadcast_to(wcol, (gw, gw)))

    for j in (reversed(range(nch)) if reverse else range(nch)):
        rows = slice(j * L, (j + 1) * L)
        s_old = [s_scr[g] for g in range(ng)]
        z = [_dot(lhs_scr[j, g], bdiag(s_old[g])) for g in range(ng)]
        for g in range(ng):
            y_ref[rows, g * gw:(g + 1) * gw] += z[g][:L]
            s_scr[g] = wl_scr[j, g] * s_old[g] + z[g][L:] + n_scr[j, g]


def _wkv_scan(r, lw, k, v, a, b, d_idx, n_ctx_blocks, reverse):
    bsz, t, d = r.shape
    L = WKV_CHUNK
    tb = _tile(WKV_BLOCK, t)
    nb = t // tb
    nch = tb // L
    gw = min(LANE_GROUP, d)
    ng = d // gw
    if reverse:
        cidx = lambda c: jnp.where(c < n_ctx_blocks, n_ctx_blocks - 1 - c, nb + n_ctx_blocks - 1 - c)
    else:
        cidx = lambda c: c
    tok = pl.BlockSpec((None, tb, d), lambda bb, c: (bb, cidx(c), 0))
    tokd = pl.BlockSpec((None, None, tb, d), lambda bb, c: (d_idx, bb, cidx(c), 0))
    return pl.pallas_call(
        functools.partial(_wkv_kernel, reverse=reverse),
        out_shape=jax.ShapeDtypeStruct((bsz, t, d), F32),
        grid=(bsz, nb),
        in_specs=[tok, tokd, tokd, tok, tok, tokd],
        out_specs=tok,
        scratch_shapes=[pltpu.VMEM((ng, HEAD_SIZE, gw), F32),
                        pltpu.VMEM((nch, ng, 2 * L, gw), BF16),
                        pltpu.VMEM((nch, ng, HEAD_SIZE, gw), F32),
                        pltpu.VMEM((nch, ng, HEAD_SIZE, gw), F32)],
        compiler_params=_params("parallel", "arbitrary"),
        name="wkv_rev" if reverse else "wkv_fwd",
    )(r, lw, k, v, a, b)


def _rwkv_out_kernel(y0_ref, y1_ref, r_ref, k_ref, v_ref, g_ref, rk_ref, lnw_ref, lnb_ref,
                     wo_ref, x_ref, gate_ref, o_ref):
    y = y0_ref[...] + y1_ref[...]
    ones = _seg_ones(min(LANE_GROUP, y.shape[1]))
    inv_n = 1.0 / HEAD_SIZE
    mu = _head_sum(y, ones) * inv_n
    yc = y - mu
    var = _head_sum(yc * yc, ones) * inv_n
    yn = yc * lax.rsqrt(var + LNX_EPS) * lnw_ref[...] + lnb_ref[...]
    ksum = k_ref[0].astype(F32) + k_ref[1].astype(F32)
    bonus = _head_sum(r_ref[...].astype(F32) * ksum * rk_ref[...], ones) * v_ref[...]
    out = ((yn + bonus) * g_ref[...].astype(F32)).astype(BF16)
    o_ref[...] = x_ref[...] + gate_ref[...] * _dot(out, wo_ref[...])


def _rwkv_output(y0, y1, r, kd, v, g, rk, lnw, lnb, wo_bf, x, gate, t_off):
    bsz, t, d = x.shape
    tm = _tile(256, t, t_off)
    off = t_off // tm
    tok = pl.BlockSpec((None, tm, d), lambda b, i: (b, i + off, 0))
    tok2 = pl.BlockSpec((2, None, tm, d), lambda b, i: (0, b, i + off, 0))
    row = pl.BlockSpec((1, d), lambda b, i: (0, 0))
    return pl.pallas_call(
        _rwkv_out_kernel,
        out_shape=jax.ShapeDtypeStruct((bsz, t, d), F32),
        grid=(bsz, t // tm),
        in_specs=[tok, tok, tok, tok2, tok, tok, row, row, row,
                  pl.BlockSpec((d, d), lambda b, i: (0, 0)),
                  pl.BlockSpec((None, tm, d), lambda b, i: (b, i, 0)),
                  pl.BlockSpec((None, 1, d), lambda b, i: (b, 0, 0))],
        out_specs=pl.BlockSpec((None, tm, d), lambda b, i: (b, i, 0)),
        compiler_params=_params("parallel", "parallel"),
        name="rwkv_out",
    )(y0, y1, r, kd, v, g, rk.reshape(1, d), lnw.reshape(1, d), lnb.reshape(1, d), wo_bf, x, gate)


def _moe_route_kernel(x_ref, g_ref, sc_ref, sh_ref, rt_ref, xs_ref, gs_ref, code_ref,
                      h_scr, aff_scr, *, cap):
    t, d = x_ref.shape
    n_e = rt_ref.shape[0]

    @pl.when(pl.program_id(1) == 0)
    def _():
        g = g_ref[...]
        sc = sc_ref[...]
        sh = sh_ref[...]
        rt_hi, rt_lo = _split2(rt_ref[...])
        tc = min(t, 256)
        for j in range(t // tc):
            h = _norm_mod(x_ref[j * tc:(j + 1) * tc, :], g, sc, sh)
            h_hi, h_lo = _split2(h)
            h_scr[j * tc:(j + 1) * tc, :] = h_hi
            aff_scr[:, j * tc:(j + 1) * tc] = (_dot_nt(rt_hi, h_hi) + _dot_nt(rt_hi, h_lo)
                                               + _dot_nt(rt_lo, h_hi))
        logits = aff_scr[...]
        m = jnp.max(logits, axis=0, keepdims=True)
        ex = jnp.exp(logits - m)
        aff = ex / jnp.sum(ex, axis=0, keepdims=True)
        aff_scr[...] = aff
        bits = pltpu.bitcast(aff, jnp.int32)

        def count_ge(cand):
            return jnp.sum(jnp.where(bits >= cand, 1, 0), axis=1, keepdims=True)

        def search(i, thr):
            lo = 28 - 2 * i
            c1, c2, c3 = [thr | (jnp.int32(m) << lo) for m in (1, 2, 3)]
            thr = jnp.where(count_ge(c1) >= cap, c1, thr)
            thr = jnp.where(count_ge(c2) >= cap, c2, thr)
            return jnp.where(count_ge(c3) >= cap, c3, thr)

        thr = jnp.zeros((n_e, 1), jnp.int32)
        thr = jnp.where(count_ge(thr | (jnp.int32(1) << 30)) >= cap, thr | (jnp.int32(1) << 30), thr)
        thr = lax.fori_loop(0, 15, search, thr)
        gt = bits > thr
        eq = bits == thr
        key = jnp.where(gt, 1, 0) + jnp.where(eq, 4096, 0)
        lane = lax.broadcasted_iota(jnp.int32, (n_e, t), 1)
        csum = key
        sh_amt = 1
        while sh_amt < t:
            csum = csum + jnp.where(lane >= sh_amt, pltpu.roll(csum, sh_amt, 1), 0)
            sh_amt *= 2
        before = csum - key
        n_gt = before & 4095
        n_eq = before >> 12
        need = cap - jnp.sum(jnp.where(gt, 1, 0), axis=1, keepdims=True)
        sel = gt | (eq & (n_eq < need))
        code_ref[...] = jnp.where(sel, n_gt + jnp.minimum(n_eq, need), -1)

    eg = xs_ref.shape[0]
    e0 = pl.program_id(1) * eg
    slot = lax.broadcasted_iota(jnp.int32, (cap, t), 0)
    onehot = []
    for l in range(eg):
        hit = code_ref[pl.ds(e0 + l, 1), :] == slot
        onehot.append(jnp.where(hit, 1.0, 0.0).astype(BF16))
        gs_ref[l] = jnp.sum(jnp.where(hit, aff_scr[pl.ds(e0 + l, 1), :], 0.0), axis=1, keepdims=True)
    gathered = _dot(onehot[0] if eg == 1 else jnp.concatenate(onehot, axis=0), h_scr[...])
    xs_ref[...] = gathered.reshape(eg, cap, d).astype(BF16)


def _moe_route(x, norm_g, sc, sh, router):
    bsz, t, d = x.shape
    n_e = router.shape[1]
    cap = CAPACITY_FACTOR * t // n_e
    eg = _tile(max(1, MOE_GATHER_ROWS // cap), n_e)
    vec = lambda b, e: (b, 0, 0)
    return pl.pallas_call(
        functools.partial(_moe_route_kernel, cap=cap),
        out_shape=[jax.ShapeDtypeStruct((bsz, n_e, cap, d), BF16),
                   jax.ShapeDtypeStruct((bsz, n_e, cap, 1), F32),
                   jax.ShapeDtypeStruct((bsz, n_e, t), jnp.int32)],
        grid=(bsz, n_e // eg),
        in_specs=[pl.BlockSpec((None, t, d), lambda b, e: (b, 0, 0)),
                  pl.BlockSpec((1, d), lambda b, e: (0, 0)),
                  pl.BlockSpec((None, 1, d), vec),
                  pl.BlockSpec((None, 1, d), vec),
                  pl.BlockSpec((n_e, d), lambda b, e: (0, 0))],
        out_specs=[pl.BlockSpec((None, eg, cap, d), lambda b, e: (b, e, 0, 0)),
                   pl.BlockSpec((None, eg, cap, 1), lambda b, e: (b, e, 0, 0)),
                   pl.BlockSpec((None, n_e, t), lambda b, e: (b, 0, 0))],
        scratch_shapes=[pltpu.VMEM((t, d), BF16), pltpu.VMEM((n_e, t), F32)],
        compiler_params=_params("parallel", "arbitrary"),
        name="moe_route",
    )(x, norm_g.reshape(1, d), sc, sh, router.T)


def _moe_ffn_kernel(xs_ref, gs_ref, wg_ref, wu_ref, wd_ref, ys_ref, wg_scr, wu_scr, wd_scr):
    @pl.when(pl.program_id(1) == 0)
    def _():
        wg_scr[...] = wg_ref[...].astype(BF16)
        wu_scr[...] = wu_ref[...].astype(BF16)
        wd_scr[...] = wd_ref[...].astype(BF16)

    tb, cap, d = xs_ref.shape
    xs = xs_ref[...].reshape(tb * cap, d)
    hg = _dot(xs, wg_scr[...])
    hu = _dot(xs, wu_scr[...])
    hid = (hg * _sigmoid(hg) * hu).astype(BF16)
    ys = _dot(hid, wd_scr[...]) * gs_ref[...].reshape(tb * cap, 1)
    ys_ref[...] = ys.reshape(tb, cap, d).astype(BF16)


def _moe_ffn(xs, gs, wg, wu, wd, layer):
    bsz, n_e, cap, d = xs.shape
    f = wg.shape[-1]
    tb = max(1, min(bsz, 512 // cap))
    while bsz % tb:
        tb -= 1
    return pl.pallas_call(
        _moe_ffn_kernel,
        out_shape=jax.ShapeDtypeStruct((bsz, n_e, cap, d), BF16),
        grid=(n_e, bsz // tb),
        in_specs=[pl.BlockSpec((tb, None, cap, d), lambda e, j: (j, e, 0, 0)),
                  pl.BlockSpec((tb, None, cap, 1), lambda e, j: (j, e, 0, 0)),
                  pl.BlockSpec((None, None, d, f), lambda e, j: (layer, e, 0, 0)),
                  pl.BlockSpec((None, None, d, f), lambda e, j: (layer, e, 0, 0)),
                  pl.BlockSpec((None, None, f, d), lambda e, j: (layer, e, 0, 0))],
        out_specs=pl.BlockSpec((tb, None, cap, d), lambda e, j: (j, e, 0, 0)),
        scratch_shapes=[pltpu.VMEM((d, f), BF16), pltpu.VMEM((d, f), BF16), pltpu.VMEM((f, d), BF16)],
        compiler_params=_params("parallel", "arbitrary"),
        name="moe_ffn",
    )(xs, gs, wg, wu, wd)


def _moe_combine_kernel(code_ref, ys_ref, x_ref, gate_ref, fg_ref, o_ref, *, cap, final_norm):
    n_e = code_ref.shape[1]
    slot = lax.broadcasted_iota(jnp.int32, (1, cap), 1)
    pieces = [jnp.where(code_ref[:, e:e + 1] == slot, 1.0, 0.0).astype(BF16) for e in range(n_e)]
    scat = jnp.concatenate(pieces, axis=1)
    x = x_ref[...] + gate_ref[...] * _dot(scat, ys_ref[...])
    if final_norm:
        x = x * lax.rsqrt(jnp.mean(x * x, axis=-1, keepdims=True) + NORM_EPS) * fg_ref[...]
    o_ref[...] = x


def _moe_combine(code_t, ys, x, gate, final_g, final_norm):
    bsz, t, d = x.shape
    n_e = code_t.shape[-1]
    cap = ys.shape[1] // n_e
    tm = _tile(1024, t)
    return pl.pallas_call(
        functools.partial(_moe_combine_kernel, cap=cap, final_norm=final_norm),
        out_shape=jax.ShapeDtypeStruct((bsz, t, d), F32),
        grid=(bsz, t // tm),
        in_specs=[pl.BlockSpec((None, tm, n_e), lambda b, i: (b, i, 0)),
                  pl.BlockSpec((None, n_e * cap, d), lambda b, i: (b, 0, 0)),
                  pl.BlockSpec((None, tm, d), lambda b, i: (b, i, 0)),
                  pl.BlockSpec((None, 1, d), lambda b, i: (b, 0, 0)),
                  pl.BlockSpec((1, d), lambda b, i: (0, 0))],
        out_specs=pl.BlockSpec((None, tm, d), lambda b, i: (b, i, 0)),
        compiler_params=_params("parallel", "parallel"),
        name="moe_combine",
    )(code_t, ys, x, gate, final_g.reshape(1, d))


def _ec_moe_layer(x, norm_g, sc, sh, gate, router, wg, wu, wd, layer, final_g, final_norm):
    bsz, t, d = x.shape
    xs, gs, code = _moe_route(x, norm_g, sc, sh, router)
    ys = _moe_ffn(xs, gs, wg, wu, wd, layer)
    n_e, cap = xs.shape[1], xs.shape[2]
    return _moe_combine(jnp.swapaxes(code, 1, 2), ys.reshape(bsz, n_e * cap, d), x, gate,
                        final_g, final_norm)


def kernel(x, c, ctx, c_ctx, ada_w, ada_b, norm_g, fnet_wo, fnet_bo, rw_mix, rw_wr, rw_wk, rw_wv, rw_wo,
           rw_w0, rw_w1, rw_w2, rw_a0, rw_a1, rw_a2, rw_v0, rw_v1, rw_v2, rw_g1, rw_g2, rw_kk, rw_ka, rw_rk,
           rw_lnx_w, rw_lnx_b, moe_router, moe_wg, moe_wu, moe_wd, final_g):
    bsz, t, d = x.shape
    t_ctx = ctx.shape[1]
    depth = ada_w.shape[0]
    n_mixers = 2

    rows = -(-(bsz + 1) // 8) * 8
    cc = jnp.concatenate([c, c_ctx[None, :], jnp.zeros((rows - bsz - 1, d), F32)], axis=0)
    mods = _adaln(cc, ada_w, ada_b)

    def mod_x(i, j):
        return mods[i, :bsz, j * d:(j + 1) * d].reshape(bsz, 1, d)

    def mod_c(i, j):
        return jnp.broadcast_to(mods[i, bsz, j * d:(j + 1) * d].reshape(1, 1, d), (bsz, 1, d))

    mats_x = _dft_mats(t, d // FNET_GROUPS)
    mats_c = _dft_mats(t_ctx, d // FNET_GROUPS)
    vfirst = None
    for i in range(depth):
        need_ctx = i < depth - 1
        if i % n_mixers == 0:
            fi = i // n_mixers
            wo_bf = fnet_wo[fi].astype(BF16)
            x = _fnet_layer(x, norm_g[i, 0], mod_x(i, 1), mod_x(i, 0), mod_x(i, 2), wo_bf, fnet_bo[fi], mats_x)
            if need_ctx:
                ctx = _fnet_layer(ctx, norm_g[i, 0], mod_c(i, 1), mod_c(i, 0), mod_c(i, 2), wo_bf,
                                  fnet_bo[fi], mats_c)
        else:
            ri = i // n_mixers
            p = dict(mix=rw_mix[ri], wr=rw_wr[ri], wk=rw_wk[ri], wv=rw_wv[ri], w0=rw_w0[ri], w1=rw_w1[ri],
                     w2=rw_w2[ri], a0=rw_a0[ri], a1=rw_a1[ri], a2=rw_a2[ri], g1=rw_g1[ri], g2=rw_g2[ri],
                     kk=rw_kk[ri], ka=rw_ka[ri])
            if ri > 0:
                p.update(v0=rw_v0[ri - 1], v1=rw_v1[ri - 1], v2=rw_v2[ri - 1])
            vf = vfirst if ri > 0 else None
            r, v, g, a, lw, kd, bd = _rwkv_project(ctx, x, norm_g[i, 0], (mod_c(i, 1), mod_c(i, 0)),
                                                   (mod_x(i, 1), mod_x(i, 0)), vf, p)
            if ri == 0:
                vfirst = v
            ncb = t_ctx // _tile(WKV_BLOCK, t_ctx + t)
            y0 = _wkv_scan(r, lw, kd, v, a, bd, 0, ncb, False)
            y1 = _wkv_scan(r, lw, kd, v, a, bd, 1, ncb, True)
            wo_bf = rw_wo[ri].astype(BF16)
            args = (y0, y1, r, kd, v, g, rw_rk[ri], rw_lnx_w[ri], rw_lnx_b[ri], wo_bf)
            x = _rwkv_output(*args, x, mod_x(i, 2), t_ctx)
            if need_ctx:
                ctx = _rwkv_output(*args, ctx, mod_c(i, 2), 0)
        last = i == depth - 1
        x = _ec_moe_layer(x, norm_g[i, 1], mod_x(i, 4), mod_x(i, 3), mod_x(i, 5), moe_router[i],
                          moe_wg, moe_wu, moe_wd, i, final_g, last)
        if need_ctx:
            ctx = _ec_moe_layer(ctx, norm_g[i, 1], mod_c(i, 4), mod_c(i, 3), mod_c(i, 5), moe_router[i],
                                moe_wg, moe_wu, moe_wd, i, final_g, False)
    return x
```

```python
import functools

import jax
import jax.numpy as jnp
from jax import lax
from jax.experimental import pallas as pl
from jax.experimental.pallas import tpu as pltpu

F32 = jnp.float32
BF16 = jnp.bfloat16

HEAD_SIZE = 64
LANES = 128
LANE_GROUP = 256
GRID_W = 64
FNET_GROUPS = 4
CAPACITY_FACTOR = 2
NORM_EPS = 1e-6
LNX_EPS = 64e-5
DECAY_SCALE = 0.6065306597126334
WKV_CHUNK = 64
WKV_BLOCK = 256
WKV_WAVE = 4
MOE_GATHER_ROWS = 1024
VMEM_LIMIT_BYTES = 56 * 1024 * 1024


def _params(*semantics):
    return pltpu.CompilerParams(dimension_semantics=semantics, vmem_limit_bytes=VMEM_LIMIT_BYTES)


def _tile(pref, *extents):
    tm = pref
    while any(n % tm for n in extents if n):
        tm //= 2
    return tm


def _dot(a, b):
    return jnp.dot(a, b, preferred_element_type=F32)


def _dot_nt(a, b):
    return lax.dot_general(a, b, (((1,), (1,)), ((), ())), preferred_element_type=F32)


def _split2(x):
    hi = x.astype(BF16)
    lo = (x - hi.astype(F32)).astype(BF16)
    return hi, lo


def _split3(x):
    hi = x.astype(BF16)
    r1 = x - hi.astype(F32)
    mid = r1.astype(BF16)
    lo = (r1 - mid.astype(F32)).astype(BF16)
    return hi, mid, lo


def _norm_mod(x, g, sc, sh):
    ms = jnp.mean(x * x, axis=-1, keepdims=True)
    return x * lax.rsqrt(ms + NORM_EPS) * g * (1.0 + sc) + sh


def _sigmoid(x):
    return 0.5 * jnp.tanh(0.5 * x) + 0.5


def _seg_ones(n):
    r = lax.broadcasted_iota(jnp.int32, (n, n), 0) // HEAD_SIZE
    c = lax.broadcasted_iota(jnp.int32, (n, n), 1) // HEAD_SIZE
    return jnp.where(r == c, 1.0, 0.0).astype(BF16)


def _head_sum(x, ones):
    d = x.shape[-1]
    w = ones.shape[0]
    parts = []
    for j in range(d // w):
        hi, lo = _split2(x[:, j * w:(j + 1) * w])
        parts.append(_dot(hi, ones) + _dot(lo, ones))
    return parts[0] if len(parts) == 1 else jnp.concatenate(parts, axis=1)


def _adaln_kernel(c_ref, w_ref, b_ref, o_ref):
    c = c_ref[...]
    s = c * _sigmoid(c)
    s_hi, s_lo = _split2(s)
    w_hi, w_lo = _split2(w_ref[...])
    o_ref[...] = _dot(s_hi, w_hi) + _dot(s_lo, w_hi) + _dot(s_hi, w_lo) + b_ref[...]


def _adaln(cc, ada_w, ada_b):
    depth, d, n = ada_w.shape
    rows = cc.shape[0]
    tn = min(n, 1536)
    return pl.pallas_call(
        _adaln_kernel,
        out_shape=jax.ShapeDtypeStruct((depth, rows, n), F32),
        grid=(depth, n // tn),
        in_specs=[pl.BlockSpec((rows, d), lambda l, j: (0, 0)),
                  pl.BlockSpec((None, d, tn), lambda l, j: (l, 0, j)),
                  pl.BlockSpec((None, 1, tn), lambda l, j: (l, 0, j))],
        out_specs=pl.BlockSpec((None, rows, tn), lambda l, j: (l, 0, j)),
        compiler_params=_params("parallel", "parallel"),
        name="adaln",
    )(cc, ada_w, ada_b.reshape(depth, 1, n))


def _fnet_chan_kernel(x_ref, g_ref, sc_ref, sh_ref, cs_ref, o_ref):
    h = _norm_mod(x_ref[...], g_ref[...], sc_ref[...], sh_ref[...])
    gd = cs_ref.shape[0]
    cs = cs_ref[...]
    for j in range(h.shape[1] // gd):
        z = _dot(h[:, j * gd:(j + 1) * gd].astype(BF16), cs)
        o_ref[0, :, j * gd:(j + 1) * gd] = z[:, :gd].astype(BF16)
        o_ref[1, :, j * gd:(j + 1) * gd] = z[:, gd:].astype(BF16)


def _fnet_time_kernel(f_ref, hcs_ref, wo_ref, bo_ref, x_ref, gate_ref, o_ref):
    f = _dot(f_ref[...], hcs_ref[...])
    y = _dot(f.astype(BF16), wo_ref[...]) + bo_ref[...]
    o_ref[...] = x_ref[...] + gate_ref[...] * y


def _dft_mats(t, gd):
    def cs(n):
        i = jnp.arange(n, dtype=jnp.int32)
        ang = ((i[:, None] * i[None, :]) % n).astype(F32) * (2.0 * jnp.pi / n)
        return jnp.cos(ang), jnp.sin(ang)
    ct, st = cs(t)
    cc, sc = cs(gd)
    scale = 1.0 / jnp.sqrt(jnp.asarray(t * gd, F32))
    return (jnp.concatenate([ct, -st], axis=1).astype(BF16),
            (jnp.concatenate([cc, sc], axis=1) * scale).astype(BF16))


def _fnet_layer(x, norm_g, sc, sh, gate, wo_bf, bo, mats):
    bsz, t, d = x.shape
    f_mat, cs_mat = mats
    gd = d // FNET_GROUPS
    tm = min(t, 512)
    vec = lambda b, i: (b, 0, 0)
    hcs = pl.pallas_call(
        _fnet_chan_kernel,
        out_shape=jax.ShapeDtypeStruct((bsz, 2, t, d), BF16),
        grid=(bsz, t // tm),
        in_specs=[pl.BlockSpec((None, tm, d), lambda b, i: (b, i, 0)),
                  pl.BlockSpec((1, d), lambda b, i: (0, 0)),
                  pl.BlockSpec((None, 1, d), vec),
                  pl.BlockSpec((None, 1, d), vec),
                  pl.BlockSpec((gd, 2 * gd), lambda b, i: (0, 0))],
        out_specs=pl.BlockSpec((None, 2, tm, d), lambda b, i: (b, 0, i, 0)),
        compiler_params=_params("parallel", "parallel"),
        name="fnet_chan",
    )(x, norm_g.reshape(1, d), sc, sh, cs_mat)
    hcs = hcs.reshape(bsz, 2 * t, d)
    tm2 = min(t, 256)
    return pl.pallas_call(
        _fnet_time_kernel,
        out_shape=jax.ShapeDtypeStruct((bsz, t, d), F32),
        grid=(bsz, t // tm2),
        in_specs=[pl.BlockSpec((tm2, 2 * t), lambda b, i: (i, 0)),
                  pl.BlockSpec((None, 2 * t, d), lambda b, i: (b, 0, 0)),
                  pl.BlockSpec((d, d), lambda b, i: (0, 0)),
                  pl.BlockSpec((1, d), lambda b, i: (0, 0)),
                  pl.BlockSpec((None, tm2, d), lambda b, i: (b, i, 0)),
                  pl.BlockSpec((None, 1, d), vec)],
        out_specs=pl.BlockSpec((None, tm2, d), lambda b, i: (b, i, 0)),
        compiler_params=_params("parallel", "parallel"),
        name="fnet_time",
    )(f_mat, hcs, wo_bf, bo.reshape(1, d), x, gate)


def _shifted_grid(h, h_above, h_below):
    tm, d = h.shape
    q = d // 4
    colw = lax.broadcasted_iota(jnp.int32, (tm, q), 0) % GRID_W
    left = jnp.where(colw != 0, pltpu.roll(h[:, :q], 1, 0), 0.0)
    right = jnp.where(colw != GRID_W - 1, pltpu.roll(h[:, q:2 * q], tm - 1, 0), 0.0)
    up = jnp.concatenate([h_above[:, 2 * q:3 * q], h[:tm - GRID_W, 2 * q:3 * q]], axis=0)
    down = jnp.concatenate([h[GRID_W:, 3 * q:], h_below[:, 3 * q:]], axis=0)
    return jnp.concatenate([left, right, up, down], axis=1)


def _shifted_seq(h):
    tm, d = h.shape
    half = d // 2
    row = lax.broadcasted_iota(jnp.int32, (tm, half), 0)
    prev = jnp.where(row != 0, pltpu.roll(h[:, :half], 1, 0), 0.0)
    nxt = jnp.where(row != tm - 1, pltpu.roll(h[:, half:], tm - 1, 0), 0.0)
    return jnp.concatenate([prev, nxt], axis=1)


def _rwkv_proj_kernel(*refs, has_vres, n_ctx_tiles):
    it = iter(refs)
    c_ref, x_ref, xa_ref, xb_ref = next(it), next(it), next(it), next(it)
    ng_ref, scc_ref, shc_ref, scx_ref, shx_ref = next(it), next(it), next(it), next(it), next(it)
    vf_ref = next(it) if has_vres else None
    mix_ref, wr_ref, wk_ref, wv_ref = next(it), next(it), next(it), next(it)
    w0_ref, w1_ref, w2_ref = next(it), next(it), next(it)
    a0_ref, a1_ref, a2_ref = next(it), next(it), next(it)
    g1_ref, g2_ref, kk_ref, ka_ref = next(it), next(it), next(it), next(it)
    if has_vres:
        v0_ref, v1_ref, v2_ref = next(it), next(it), next(it)
    r_out, v_out, g_out, a_out = next(it), next(it), next(it), next(it)
    lw_out, kd_out, bd_out = next(it), next(it), next(it)

    i = pl.program_id(1)
    is_ctx = i < n_ctx_tiles
    ng = ng_ref[...]
    scx, shx = scx_ref[...], shx_ref[...]
    h = _norm_mod(jnp.where(is_ctx, c_ref[...], x_ref[...]), ng,
                  jnp.where(is_ctx, scc_ref[...], scx), jnp.where(is_ctx, shc_ref[...], shx))
    h_above = jnp.where(i > n_ctx_tiles, _norm_mod(xa_ref[...], ng, scx, shx), 0.0)
    h_below = jnp.where(i < pl.num_programs(1) - 1, _norm_mod(xb_ref[...], ng, scx, shx), 0.0)
    hs = jnp.where(is_ctx, _shifted_seq(h), _shifted_grid(h, h_above, h_below))
    xx = hs - h
    xr, xw, xk, xv, xa, xg = [(h + xx * mix_ref[j:j + 1, :]).astype(BF16) for j in range(6)]
    r = _dot(xr, wr_ref[...])
    k = _dot(xk, wk_ref[...])
    v = _dot(xv, wv_ref[...])
    if has_vres:
        lora = _dot(_dot(xv, v1_ref[...]).astype(BF16), v2_ref[...])
        v = v + (vf_ref[...] - v) * _sigmoid(v0_ref[...] + lora)
    g = _dot(_sigmoid(_dot(xg, g1_ref[...])).astype(BF16), g2_ref[...])
    ones = _seg_ones(min(LANE_GROUP, h.shape[1]))
    kk = k * kk_ref[...]
    kk = kk * lax.rsqrt(_head_sum(kk * kk, ones) + 1e-12)
    r_out[...] = r.astype(BF16)
    v_out[...] = v
    g_out[...] = g.astype(BF16)
    a_out[...] = (-kk).astype(BF16)
    for d in range(2):
        wlog = w0_ref[d:d + 1, :] + _dot(jnp.tanh(_dot(xw, w1_ref[d])).astype(BF16), w2_ref[d])
        lw_out[d] = -DECAY_SCALE * _sigmoid(wlog)
        a = _sigmoid(a0_ref[d:d + 1, :] + _dot(_dot(xa, a1_ref[d]).astype(BF16), a2_ref[d]))
        kd_out[d] = (k * (1.0 + (a - 1.0) * ka_ref[...])).astype(BF16)
        bd_out[d] = (kk * a).astype(BF16)


def _rwkv_project(ctx, x, norm_g, mod_c, mod_x, vfirst, p):
    bsz, t, d = x.shape
    t_ctx = ctx.shape[1]
    tm = _tile(256, t, t_ctx)
    nct = t_ctx // tm
    assert nct == 1, "the sequence shift handles a context that fits one tile"
    ntx = t // tm
    has_vres = vfirst is not None
    tok = pl.BlockSpec((None, tm, d), lambda b, i: (b, i, 0))
    tok2 = pl.BlockSpec((2, None, tm, d), lambda b, i: (0, b, i, 0))
    vec = pl.BlockSpec((None, 1, d), lambda b, i: (b, 0, 0))

    def full(a):
        nd = a.ndim
        return pl.BlockSpec(a.shape, lambda b, i: (0,) * nd)

    bf = lambda a: a.astype(BF16)
    weights = [p["mix"], bf(p["wr"]), bf(p["wk"]), bf(p["wv"]),
               p["w0"], bf(p["w1"]), bf(p["w2"]), p["a0"], bf(p["a1"]), bf(p["a2"]),
               bf(p["g1"]), bf(p["g2"]), p["kk"].reshape(1, d), p["ka"].reshape(1, d)]
    if has_vres:
        weights += [p["v0"].reshape(1, d), bf(p["v1"]), bf(p["v2"])]
    rpt = tm // GRID_W
    last = t // GRID_W - 1
    acts = [ctx, x, x, x, norm_g.reshape(1, d), *mod_c, *mod_x]
    act_specs = [pl.BlockSpec((None, tm, d), lambda b, i: (b, jnp.minimum(i, nct - 1), 0)),
                 pl.BlockSpec((None, tm, d), lambda b, i: (b, jnp.maximum(i - nct, 0), 0)),
                 pl.BlockSpec((None, GRID_W, d), lambda b, i: (b, jnp.maximum((i - nct) * rpt - 1, 0), 0)),
                 pl.BlockSpec((None, GRID_W, d), lambda b, i: (b, jnp.clip((i - nct + 1) * rpt, 0, last), 0)),
                 pl.BlockSpec((1, d), lambda b, i: (0, 0)), vec, vec, vec, vec]
    if has_vres:
        acts.append(vfirst)
        act_specs.append(tok)
    tt = t_ctx + t
    out_shape = [jax.ShapeDtypeStruct((bsz, tt, d), BF16),
                 jax.ShapeDtypeStruct((bsz, tt, d), F32),
                 jax.ShapeDtypeStruct((bsz, tt, d), BF16),
                 jax.ShapeDtypeStruct((bsz, tt, d), BF16),
                 jax.ShapeDtypeStruct((2, bsz, tt, d), F32),
                 jax.ShapeDtypeStruct((2, bsz, tt, d), BF16),
                 jax.ShapeDtypeStruct((2, bsz, tt, d), BF16)]
    return pl.pallas_call(
        functools.partial(_rwkv_proj_kernel, has_vres=has_vres, n_ctx_tiles=nct),
        out_shape=out_shape,
        grid=(bsz, nct + ntx),
        in_specs=act_specs + [full(w) for w in weights],
        out_specs=[tok, tok, tok, tok, tok2, tok2, tok2],
        compiler_params=_params("parallel", "parallel"),
        name="rwkv_proj",
    )(*acts, *weights)


def _wkv_kernel(r_ref, lw_ref, k_ref, v_ref, a_ref, b_ref, y_ref, s_scr, lhs_scr, n_scr, wl_scr, *, reverse):
    L = WKV_CHUNK
    tb, d = r_ref.shape
    nch = tb // L
    gw = s_scr.shape[-1]
    hpg = gw // HEAD_SIZE
    ng = d // gw

    @pl.when(pl.program_id(1) == 0)
    def _():
        s_scr[...] = jnp.zeros_like(s_scr)

    ti = lax.broadcasted_iota(jnp.int32, (L, L), 0)
    si = lax.broadcasted_iota(jnp.int32, (L, L), 1)
    tri = jnp.where((si >= ti) if reverse else (si <= ti), 1.0, 0.0).astype(BF16)
    last, mid = (0, L // 2) if reverse else (L - 1, L // 2 - 1)
    trow = lax.broadcasted_iota(jnp.int32, (L, gw), 0)
    scol = lax.broadcasted_iota(jnp.int32, (L, gw), 1) % HEAD_SIZE
    strict = (scol > trow) if reverse else (scol < trow)
    incl = (scol >= trow) if reverse else (scol <= trow)
    eye = jnp.where(scol == trow, 1.0, 0.0)
    bdm = (lax.broadcasted_iota(jnp.int32, (gw, gw), 0) // HEAD_SIZE
           == lax.broadcasted_iota(jnp.int32, (gw, gw), 1) // HEAD_SIZE)

    def bdiag(z):
        return jnp.where(bdm, jnp.concatenate([z] * hpg, axis=0), 0.0).astype(BF16)

    def per_head(f, lo):
        return jnp.concatenate([f[h * HEAD_SIZE:(h + 1) * HEAD_SIZE, lo:lo + L] for h in range(hpg)], axis=1)

    def compact(f):
        fm = jnp.where(bdm, f, 0.0)
        out = fm[:HEAD_SIZE]
        for h in range(1, hpg):
            out = out + fm[h * HEAD_SIZE:(h + 1) * HEAD_SIZE]
        return out

    def prep(j):
        rows = slice(j * L, (j + 1) * L)
        lw = lw_ref[rows, :]
        p1, p2, p3 = _split3(lw)
        cw = _dot(tri, p1) + _dot(tri, p2) + _dot(tri, p3)
        cw_end = cw[last:last + 1, :]
        cw_mid = cw[mid:mid + 1, :]
        r = r_ref[rows, :].astype(F32)
        k = k_ref[rows, :].astype(F32)
        a = a_ref[rows, :].astype(F32)
        b = b_ref[rows, :].astype(F32)
        e_in = jnp.exp(cw)
        e_ex = jnp.exp(cw - lw)
        e_inv = jnp.exp(cw_mid - cw)
        e_mid = jnp.exp(-cw_mid)
        e_end = e_inv * jnp.exp(cw_end - cw_mid)
        r0 = r * e_in
        a0 = a * e_ex
        return dict(rows=rows, cw_end=cw_end, r0=r0, a0=a0, ra=r0 * e_mid, aa=a0 * e_mid, kb=k * e_inv, bb=b * e_inv,
                    ke=k * e_end, be=b * e_end)

    for w0 in range(0, nch, WKV_WAVE):
        pj = {j: prep(j) for j in range(w0, min(nch, w0 + WKV_WAVE))}
        ch = [(j, slice(g * gw, (g + 1) * gw)) for j in pj for g in range(ng)]
        cs = range(len(ch))
        col = lambda name, c: pj[ch[c][0]][name][:, ch[c][1]]
        xq = [jnp.concatenate([col("aa", c), col("ra", c)], axis=0).astype(BF16) for c in cs]
        gb = [_dot_nt(xq[c], bdiag(col("bb", c))) for c in cs]
        gk = [_dot_nt(xq[c], bdiag(col("kb", c))) for c in cs]
        a_ab = [jnp.where(strict, gb[c][:L], 0.0) for c in cs]
        vg = [v_ref[pj[ch[c][0]]["rows"], ch[c][1]] for c in cs]
        ft = [jnp.concatenate([col("be", c), col("ke", c)], axis=0).T for c in cs]
        bet = [per_head(ft[c], 0) for c in cs]
        ket = [per_head(ft[c], L) for c in cs]
        akrk = [jnp.concatenate([jnp.where(strict, gk[c][:L], 0.0), jnp.where(incl, gk[c][L:], 0.0), ket[c]],
                                axis=0).astype(BF16) for c in cs]
        tmat = [eye + a_ab[c] for c in cs]
        pw = [_dot(a_ab[c].astype(BF16), bdiag(a_ab[c])) for c in cs]
        avk = [_dot(akrk[c], bdiag(vg[c])) for c in cs]
        n_lvl = L.bit_length() - 2
        for lvl in range(n_lvl):
            if lvl + 1 < n_lvl:
                z = [_dot(jnp.concatenate([tmat[c], pw[c]], axis=0).astype(BF16), bdiag(pw[c])) for c in cs]
                tmat = [tmat[c] + z[c][:L] for c in cs]
                pw = [z[c][L:] for c in cs]
            else:
                z = [_dot(tmat[c].astype(BF16), bdiag(pw[c])) for c in cs]
                tmat = [tmat[c] + z[c] for c in cs]
        rbe = [jnp.concatenate([jnp.where(incl, gb[c][L:], 0.0), bet[c]], axis=0).astype(BF16) for c in cs]
        rt = [_dot(rbe[c], bdiag(tmat[c])).astype(BF16) for c in cs]
        za = [_dot(rt[c], bdiag(col("a0", c))) for c in cs]
        zv = [_dot(rt[c], bdiag(avk[c][:L])) for c in cs]
        ridx = lax.broadcasted_iota(jnp.int32, (LANES, d), 0)
        ends = jnp.zeros((LANES, d), F32)
        for n, j in enumerate(pj):
            ends = jnp.where(ridx == n, pj[j]["cw_end"], ends)
        w_end = jnp.exp(ends.T)
        for c in cs:
            j, sl = ch[c]
            g = c % ng
            y_ref[pj[j]["rows"], sl] = zv[c][:L] + avk[c][L:2 * L]
            lhs_scr[j, g, :L, :] = (col("r0", c) + za[c][:L]).astype(BF16)
            lhs_scr[j, g, L:, :] = za[c][L:].astype(BF16)
            n_scr[j, g] = zv[c][L:] + avk[c][2 * L:]
            wcol = w_end[sl, c // ng:c // ng + 1]
            wl_scr[j, g] = compact(jnp.broadcast_to(wcol, (gw, gw)))

    for j in (reversed(range(nch)) if reverse else range(nch)):
        rows = slice(j * L, (j + 1) * L)
        s_old = [s_scr[g] for g in range(ng)]
        z = [_dot(lhs_scr[j, g], bdiag(s_old[g])) for g in range(ng)]
        for g in range(ng):
            y_ref[rows, g * gw:(g + 1) * gw] += z[g][:L]
            s_scr[g] = wl_scr[j, g] * s_old[g] + z[g][L:] + n_scr[j, g]


def _wkv_scan(r, lw, k, v, a, b, d_idx, t_ctx, reverse):
    bsz, t, d = r.shape
    L = WKV_CHUNK
    tb = _tile(WKV_BLOCK, t_ctx, t - t_ctx)
    nb = t // tb
    n_ctx_blocks = t_ctx // tb
    nch = tb // L
    gw = min(LANE_GROUP, d)
    ng = d // gw
    if reverse:
        cidx = lambda c: jnp.where(c < n_ctx_blocks, n_ctx_blocks - 1 - c, nb + n_ctx_blocks - 1 - c)
    else:
        cidx = lambda c: c
    tok = pl.BlockSpec((None, tb, d), lambda bb, c: (bb, cidx(c), 0))
    tokd = pl.BlockSpec((None, None, tb, d), lambda bb, c: (d_idx, bb, cidx(c), 0))
    return pl.pallas_call(
        functools.partial(_wkv_kernel, reverse=reverse),
        out_shape=jax.ShapeDtypeStruct((bsz, t, d), F32),
        grid=(bsz, nb),
        in_specs=[tok, tokd, tokd, tok, tok, tokd],
        out_specs=tok,
        scratch_shapes=[pltpu.VMEM((ng, HEAD_SIZE, gw), F32),
                        pltpu.VMEM((nch, ng, 2 * L, gw), BF16),
                        pltpu.VMEM((nch, ng, HEAD_SIZE, gw), F32),
                        pltpu.VMEM((nch, ng, HEAD_SIZE, gw), F32)],
        compiler_params=_params("parallel", "arbitrary"),
        name="wkv_rev" if reverse else "wkv_fwd",
    )(r, lw, k, v, a, b)


def _rwkv_out_kernel(y0_ref, y1_ref, r_ref, k_ref, v_ref, g_ref, rk_ref, lnw_ref, lnb_ref,
                     wo_ref, x_ref, gate_ref, o_ref):
    y = y0_ref[...] + y1_ref[...]
    ones = _seg_ones(min(LANE_GROUP, y.shape[1]))
    inv_n = 1.0 / HEAD_SIZE
    mu = _head_sum(y, ones) * inv_n
    yc = y - mu
    var = _head_sum(yc * yc, ones) * inv_n
    yn = yc * lax.rsqrt(var + LNX_EPS) * lnw_ref[...] + lnb_ref[...]
    ksum = k_ref[0].astype(F32) + k_ref[1].astype(F32)
    bonus = _head_sum(r_ref[...].astype(F32) * ksum * rk_ref[...], ones) * v_ref[...]
    out = ((yn + bonus) * g_ref[...].astype(F32)).astype(BF16)
    o_ref[...] = x_ref[...] + gate_ref[...] * _dot(out, wo_ref[...])


def _rwkv_output(y0, y1, r, kd, v, g, rk, lnw, lnb, wo_bf, x, gate, t_off):
    bsz, t, d = x.shape
    tm = _tile(256, t, t_off)
    off = t_off // tm
    tok = pl.BlockSpec((None, tm, d), lambda b, i: (b, i + off, 0))
    tok2 = pl.BlockSpec((2, None, tm, d), lambda b, i: (0, b, i + off, 0))
    row = pl.BlockSpec((1, d), lambda b, i: (0, 0))
    return pl.pallas_call(
        _rwkv_out_kernel,
        out_shape=jax.ShapeDtypeStruct((bsz, t, d), F32),
        grid=(bsz, t // tm),
        in_specs=[tok, tok, tok, tok2, tok, tok, row, row, row,
                  pl.BlockSpec((d, d), lambda b, i: (0, 0)),
                  pl.BlockSpec((None, tm, d), lambda b, i: (b, i, 0)),
                  pl.BlockSpec((None, 1, d), lambda b, i: (b, 0, 0))],
        out_specs=pl.BlockSpec((None, tm, d), lambda b, i: (b, i, 0)),
        compiler_params=_params("parallel", "parallel"),
        name="rwkv_out",
    )(y0, y1, r, kd, v, g, rk.reshape(1, d), lnw.reshape(1, d), lnb.reshape(1, d), wo_bf, x, gate)


def _moe_route_kernel(x_ref, g_ref, sc_ref, sh_ref, rt_ref, xs_ref, gs_ref, code_ref,
                      h_scr, aff_scr, *, cap):
    t, d = x_ref.shape
    n_e = rt_ref.shape[0]

    @pl.when(pl.program_id(1) == 0)
    def _():
        g = g_ref[...]
        sc = sc_ref[...]
        sh = sh_ref[...]
        rt_hi, rt_lo = _split2(rt_ref[...])
        tc = min(t, 256)
        for j in range(t // tc):
            h = _norm_mod(x_ref[j * tc:(j + 1) * tc, :], g, sc, sh)
            h_hi, h_lo = _split2(h)
            h_scr[j * tc:(j + 1) * tc, :] = h_hi
            aff_scr[:, j * tc:(j + 1) * tc] = (_dot_nt(rt_hi, h_hi) + _dot_nt(rt_hi, h_lo)
                                               + _dot_nt(rt_lo, h_hi))
        logits = aff_scr[...]
        m = jnp.max(logits, axis=0, keepdims=True)
        ex = jnp.exp(logits - m)
        aff = ex / jnp.sum(ex, axis=0, keepdims=True)
        aff_scr[...] = aff
        bits = pltpu.bitcast(aff, jnp.int32)

        def count_ge(cand):
            return jnp.sum(jnp.where(bits >= cand, 1, 0), axis=1, keepdims=True)

        def search(i, thr):
            lo = 28 - 2 * i
            c1, c2, c3 = [thr | (jnp.int32(m) << lo) for m in (1, 2, 3)]
            thr = jnp.where(count_ge(c1) >= cap, c1, thr)
            thr = jnp.where(count_ge(c2) >= cap, c2, thr)
            return jnp.where(count_ge(c3) >= cap, c3, thr)

        thr = jnp.zeros((n_e, 1), jnp.int32)
        thr = jnp.where(count_ge(thr | (jnp.int32(1) << 30)) >= cap, thr | (jnp.int32(1) << 30), thr)
        thr = lax.fori_loop(0, 15, search, thr)
        gt = bits > thr
        eq = bits == thr
        key = jnp.where(gt, 1, 0) + jnp.where(eq, 4096, 0)
        lane = lax.broadcasted_iota(jnp.int32, (n_e, t), 1)
        csum = key
        sh_amt = 1
        while sh_amt < t:
            csum = csum + jnp.where(lane >= sh_amt, pltpu.roll(csum, sh_amt, 1), 0)
            sh_amt *= 2
        before = csum - key
        n_gt = before & 4095
        n_eq = before >> 12
        need = cap - jnp.sum(jnp.where(gt, 1, 0), axis=1, keepdims=True)
        sel = gt | (eq & (n_eq < need))
        code_ref[...] = jnp.where(sel, n_gt + jnp.minimum(n_eq, need), -1)

    eg = xs_ref.shape[0]
    e0 = pl.program_id(1) * eg
    slot = lax.broadcasted_iota(jnp.int32, (cap, t), 0)
    onehot = []
    for l in range(eg):
        hit = code_ref[pl.ds(e0 + l, 1), :] == slot
        onehot.append(jnp.where(hit, 1.0, 0.0).astype(BF16))
        gs_ref[l] = jnp.sum(jnp.where(hit, aff_scr[pl.ds(e0 + l, 1), :], 0.0), axis=1, keepdims=True)
    gathered = _dot(onehot[0] if eg == 1 else jnp.concatenate(onehot, axis=0), h_scr[...])
    xs_ref[...] = gathered.reshape(eg, cap, d).astype(BF16)


def _moe_route(x, norm_g, sc, sh, router):
    bsz, t, d = x.shape
    n_e = router.shape[1]
    cap = CAPACITY_FACTOR * t // n_e
    eg = _tile(max(1, MOE_GATHER_ROWS // cap), n_e)
    vec = lambda b, e: (b, 0, 0)
    return pl.pallas_call(
        functools.partial(_moe_route_kernel, cap=cap),
        out_shape=[jax.ShapeDtypeStruct((bsz, n_e, cap, d), BF16),
                   jax.ShapeDtypeStruct((bsz, n_e, cap, 1), F32),
                   jax.ShapeDtypeStruct((bsz, n_e, t), jnp.int32)],
        grid=(bsz, n_e // eg),
        in_specs=[pl.BlockSpec((None, t, d), lambda b, e: (b, 0, 0)),
                  pl.BlockSpec((1, d), lambda b, e: (0, 0)),
                  pl.BlockSpec((None, 1, d), vec),
                  pl.BlockSpec((None, 1, d), vec),
                  pl.BlockSpec((n_e, d), lambda b, e: (0, 0))],
        out_specs=[pl.BlockSpec((None, eg, cap, d), lambda b, e: (b, e, 0, 0)),
                   pl.BlockSpec((None, eg, cap, 1), lambda b, e: (b, e, 0, 0)),
                   pl.BlockSpec((None, n_e, t), lambda b, e: (b, 0, 0))],
        scratch_shapes=[pltpu.VMEM((t, d), BF16), pltpu.VMEM((n_e, t), F32)],
        compiler_params=_params("parallel", "arbitrary"),
        name="moe_route",
    )(x, norm_g.reshape(1, d), sc, sh, router.T)


def _moe_ffn_kernel(xs_ref, gs_ref, wg_ref, wu_ref, wd_ref, ys_ref, wg_scr, wu_scr, wd_scr):
    @pl.when(pl.program_id(1) == 0)
    def _():
        wg_scr[...] = wg_ref[...].astype(BF16)
        wu_scr[...] = wu_ref[...].astype(BF16)
        wd_scr[...] = wd_ref[...].astype(BF16)

    tb, cap, d = xs_ref.shape
    xs = xs_ref[...].reshape(tb * cap, d)
    hg = _dot(xs, wg_scr[...])
    hu = _dot(xs, wu_scr[...])
    hid = (hg * _sigmoid(hg) * hu).astype(BF16)
    ys = _dot(hid, wd_scr[...]) * gs_ref[...].reshape(tb * cap, 1)
    ys_ref[...] = ys.reshape(tb, cap, d).astype(BF16)


def _moe_ffn(xs, gs, wg, wu, wd, layer):
    bsz, n_e, cap, d = xs.shape
    f = wg.shape[-1]
    tb = max(1, min(bsz, 512 // cap))
    while bsz % tb:
        tb -= 1
    return pl.pallas_call(
        _moe_ffn_kernel,
        out_shape=jax.ShapeDtypeStruct((bsz, n_e, cap, d), BF16),
        grid=(n_e, bsz // tb),
        in_specs=[pl.BlockSpec((tb, None, cap, d), lambda e, j: (j, e, 0, 0)),
                  pl.BlockSpec((tb, None, cap, 1), lambda e, j: (j, e, 0, 0)),
                  pl.BlockSpec((None, None, d, f), lambda e, j: (layer, e, 0, 0)),
                  pl.BlockSpec((None, None, d, f), lambda e, j: (layer, e, 0, 0)),
                  pl.BlockSpec((None, None, f, d), lambda e, j: (layer, e, 0, 0))],
        out_specs=pl.BlockSpec((tb, None, cap, d), lambda e, j: (j, e, 0, 0)),
        scratch_shapes=[pltpu.VMEM((d, f), BF16), pltpu.VMEM((d, f), BF16), pltpu.VMEM((f, d), BF16)],
        compiler_params=_params("parallel", "arbitrary"),
        name="moe_ffn",
    )(xs, gs, wg, wu, wd)


def _moe_combine_kernel(code_ref, ys_ref, x_ref, gate_ref, fg_ref, o_ref, *, cap, final_norm):
    n_e = code_ref.shape[1]
    slot = lax.broadcasted_iota(jnp.int32, (1, cap), 1)
    pieces = [jnp.where(code_ref[:, e:e + 1] == slot, 1.0, 0.0).astype(BF16) for e in range(n_e)]
    scat = jnp.concatenate(pieces, axis=1)
    x = x_ref[...] + gate_ref[...] * _dot(scat, ys_ref[...])
    if final_norm:
        x = x * lax.rsqrt(jnp.mean(x * x, axis=-1, keepdims=True) + NORM_EPS) * fg_ref[...]
    o_ref[...] = x


def _moe_combine(code_t, ys, x, gate, final_g, final_norm):
    bsz, t, d = x.shape
    n_e = code_t.shape[-1]
    cap = ys.shape[1] // n_e
    tm = _tile(1024, t)
    return pl.pallas_call(
        functools.partial(_moe_combine_kernel, cap=cap, final_norm=final_norm),
        out_shape=jax.ShapeDtypeStruct((bsz, t, d), F32),
        grid=(bsz, t // tm),
        in_specs=[pl.BlockSpec((None, tm, n_e), lambda b, i: (b, i, 0)),
                  pl.BlockSpec((None, n_e * cap, d), lambda b, i: (b, 0, 0)),
                  pl.BlockSpec((None, tm, d), lambda b, i: (b, i, 0)),
                  pl.BlockSpec((None, 1, d), lambda b, i: (b, 0, 0)),
                  pl.BlockSpec((1, d), lambda b, i: (0, 0))],
        out_specs=pl.BlockSpec((None, tm, d), lambda b, i: (b, i, 0)),
        compiler_params=_params("parallel", "parallel"),
        name="moe_combine",
    )(code_t, ys, x, gate, final_g.reshape(1, d))


def _ec_moe_layer(x, norm_g, sc, sh, gate, router, wg, wu, wd, layer, final_g, final_norm):
    bsz, t, d = x.shape
    xs, gs, code = _moe_route(x, norm_g, sc, sh, router)
    ys = _moe_ffn(xs, gs, wg, wu, wd, layer)
    n_e, cap = xs.shape[1], xs.shape[2]
    return _moe_combine(jnp.swapaxes(code, 1, 2), ys.reshape(bsz, n_e * cap, d), x, gate,
                        final_g, final_norm)


def kernel(x, c, ctx, c_ctx, ada_w, ada_b, norm_g, fnet_wo, fnet_bo, rw_mix, rw_wr, rw_wk, rw_wv, rw_wo,
           rw_w0, rw_w1, rw_w2, rw_a0, rw_a1, rw_a2, rw_v0, rw_v1, rw_v2, rw_g1, rw_g2, rw_kk, rw_ka, rw_rk,
           rw_lnx_w, rw_lnx_b, moe_router, moe_wg, moe_wu, moe_wd, final_g):
    bsz, t, d = x.shape
    t_ctx = ctx.shape[1]
    depth = ada_w.shape[0]
    n_mixers = 2

    rows = -(-(bsz + 1) // 8) * 8
    cc = jnp.concatenate([c, c_ctx[None, :], jnp.zeros((rows - bsz - 1, d), F32)], axis=0)
    mods = _adaln(cc, ada_w, ada_b)

    def mod_x(i, j):
        return mods[i, :bsz, j * d:(j + 1) * d].reshape(bsz, 1, d)

    def mod_c(i, j):
        return jnp.broadcast_to(mods[i, bsz, j * d:(j + 1) * d].reshape(1, 1, d), (bsz, 1, d))

    mats_x = _dft_mats(t, d // FNET_GROUPS)
    mats_c = _dft_mats(t_ctx, d // FNET_GROUPS)
    vfirst = None
    for i in range(depth):
        need_ctx = i < depth - 1
        if i % n_mixers == 0:
            fi = i // n_mixers
            wo_bf = fnet_wo[fi].astype(BF16)
            x = _fnet_layer(x, norm_g[i, 0], mod_x(i, 1), mod_x(i, 0), mod_x(i, 2), wo_bf, fnet_bo[fi], mats_x)
            if need_ctx:
                ctx = _fnet_layer(ctx, norm_g[i, 0], mod_c(i, 1), mod_c(i, 0), mod_c(i, 2), wo_bf,
                                  fnet_bo[fi], mats_c)
        else:
            ri = i // n_mixers
            p = dict(mix=rw_mix[ri], wr=rw_wr[ri], wk=rw_wk[ri], wv=rw_wv[ri], w0=rw_w0[ri], w1=rw_w1[ri],
                     w2=rw_w2[ri], a0=rw_a0[ri], a1=rw_a1[ri], a2=rw_a2[ri], g1=rw_g1[ri], g2=rw_g2[ri],
                     kk=rw_kk[ri], ka=rw_ka[ri])
            if ri > 0:
                p.update(v0=rw_v0[ri - 1], v1=rw_v1[ri - 1], v2=rw_v2[ri - 1])
            vf = vfirst if ri > 0 else None
            r, v, g, a, lw, kd, bd = _rwkv_project(ctx, x, norm_g[i, 0], (mod_c(i, 1), mod_c(i, 0)),
                                                   (mod_x(i, 1), mod_x(i, 0)), vf, p)
            if ri == 0:
                vfirst = v
            y0 = _wkv_scan(r, lw, kd, v, a, bd, 0, t_ctx, False)
            y1 = _wkv_scan(r, lw, kd, v, a, bd, 1, t_ctx, True)
            wo_bf = rw_wo[ri].astype(BF16)
            args = (y0, y1, r, kd, v, g, rw_rk[ri], rw_lnx_w[ri], rw_lnx_b[ri], wo_bf)
            x = _rwkv_output(*args, x, mod_x(i, 2), t_ctx)
            if need_ctx:
                ctx = _rwkv_output(*args, ctx, mod_c(i, 2), 0)
        last = i == depth - 1
        x = _ec_moe_layer(x, norm_g[i, 1], mod_x(i, 4), mod_x(i, 3), mod_x(i, 5), moe_router[i],
                          moe_wg, moe_wu, moe_wd, i, final_g, last)
        if need_ctx:
            ctx = _ec_moe_layer(ctx, norm_g[i, 1], mod_c(i, 4), mod_c(i, 3), mod_c(i, 5), moe_router[i],
                                moe_wg, moe_wu, moe_wd, i, final_g, False)
    return x
```

```python
import functools

import jax
import jax.numpy as jnp
from jax import lax
from jax.experimental import pallas as pl
from jax.experimental.pallas import tpu as pltpu

F32 = jnp.float32
BF16 = jnp.bfloat16

HEAD_SIZE = 64
LANES = 128
LANE_GROUP = 256
GRID_W = 64
FNET_GROUPS = 4
CAPACITY_FACTOR = 2
NORM_EPS = 1e-6
LNX_EPS = 64e-5
DECAY_SCALE = 0.6065306597126334
WKV_CHUNK = 64
WKV_BLOCK = 256
WKV_WAVE = 4
MOE_GATHER_ROWS = 1024
MOE_FFN_ROWS = 512
MOE_COMBINE_ROWS = 1024
TOKEN_TILE = 256
CHAN_TILE = 512
ADALN_COLS = 1536
TIE_BITS = 12
VMEM_LIMIT_BYTES = 56 * 1024 * 1024


def _params(*semantics):
    return pltpu.CompilerParams(dimension_semantics=semantics, vmem_limit_bytes=VMEM_LIMIT_BYTES)


def _tile(pref, *extents):
    tm = pref
    while any(n % tm for n in extents if n):
        tm //= 2
    return tm


def _dot(a, b):
    return jnp.dot(a, b, preferred_element_type=F32)


def _dot_nt(a, b):
    return lax.dot_general(a, b, (((1,), (1,)), ((), ())), preferred_element_type=F32)


def _split2(x):
    hi = x.astype(BF16)
    lo = (x - hi.astype(F32)).astype(BF16)
    return hi, lo


def _split3(x):
    hi = x.astype(BF16)
    r1 = x - hi.astype(F32)
    mid = r1.astype(BF16)
    lo = (r1 - mid.astype(F32)).astype(BF16)
    return hi, mid, lo


def _norm_mod(x, g, sc, sh):
    ms = jnp.mean(x * x, axis=-1, keepdims=True)
    return x * lax.rsqrt(ms + NORM_EPS) * g * (1.0 + sc) + sh


def _sigmoid(x):
    return 0.5 * jnp.tanh(0.5 * x) + 0.5


def _seg_ones(n):
    r = lax.broadcasted_iota(jnp.int32, (n, n), 0) // HEAD_SIZE
    c = lax.broadcasted_iota(jnp.int32, (n, n), 1) // HEAD_SIZE
    return jnp.where(r == c, 1.0, 0.0).astype(BF16)


def _head_sum(x, ones):
    d = x.shape[-1]
    w = ones.shape[0]
    parts = []
    for j in range(d // w):
        hi, lo = _split2(x[:, j * w:(j + 1) * w])
        parts.append(_dot(hi, ones) + _dot(lo, ones))
    return parts[0] if len(parts) == 1 else jnp.concatenate(parts, axis=1)


def _adaln_kernel(c_ref, w_ref, b_ref, o_ref):
    c = c_ref[...]
    s = c * _sigmoid(c)
    s_hi, s_lo = _split2(s)
    w_hi, w_lo = _split2(w_ref[...])
    o_ref[...] = _dot(s_hi, w_hi) + _dot(s_lo, w_hi) + _dot(s_hi, w_lo) + b_ref[...]


def _adaln(cc, ada_w, ada_b):
    depth, d, n = ada_w.shape
    rows = cc.shape[0]
    tn = _tile(ADALN_COLS, n)
    return pl.pallas_call(
        _adaln_kernel,
        out_shape=jax.ShapeDtypeStruct((depth, rows, n), F32),
        grid=(depth, n // tn),
        in_specs=[pl.BlockSpec((rows, d), lambda l, j: (0, 0)),
                  pl.BlockSpec((None, d, tn), lambda l, j: (l, 0, j)),
                  pl.BlockSpec((None, 1, tn), lambda l, j: (l, 0, j))],
        out_specs=pl.BlockSpec((None, rows, tn), lambda l, j: (l, 0, j)),
        compiler_params=_params("parallel", "parallel"),
        name="adaln",
    )(cc, ada_w, ada_b.reshape(depth, 1, n))


def _fnet_chan_kernel(x_ref, g_ref, sc_ref, sh_ref, cs_ref, o_ref):
    h = _norm_mod(x_ref[...], g_ref[...], sc_ref[...], sh_ref[...])
    gd = cs_ref.shape[0]
    cs = cs_ref[...]
    for j in range(h.shape[1] // gd):
        z = _dot(h[:, j * gd:(j + 1) * gd].astype(BF16), cs)
        o_ref[0, :, j * gd:(j + 1) * gd] = z[:, :gd].astype(BF16)
        o_ref[1, :, j * gd:(j + 1) * gd] = z[:, gd:].astype(BF16)


def _fnet_time_kernel(f_ref, hcs_ref, wo_ref, bo_ref, x_ref, gate_ref, o_ref):
    f = _dot(f_ref[...], hcs_ref[...])
    y = _dot(f.astype(BF16), wo_ref[...]) + bo_ref[...]
    o_ref[...] = x_ref[...] + gate_ref[...] * y


def _dft_mats(t, gd):
    def cs(n):
        i = jnp.arange(n, dtype=jnp.int32)
        ang = ((i[:, None] * i[None, :]) % n).astype(F32) * (2.0 * jnp.pi / n)
        return jnp.cos(ang), jnp.sin(ang)
    ct, st = cs(t)
    cc, sc = cs(gd)
    scale = 1.0 / jnp.sqrt(jnp.asarray(t * gd, F32))
    return (jnp.concatenate([ct, -st], axis=1).astype(BF16),
            (jnp.concatenate([cc, sc], axis=1) * scale).astype(BF16))


def _fnet_layer(x, norm_g, sc, sh, gate, wo_bf, bo, mats):
    bsz, t, d = x.shape
    f_mat, cs_mat = mats
    gd = d // FNET_GROUPS
    tm = _tile(CHAN_TILE, t)
    vec = lambda b, i: (b, 0, 0)
    hcs = pl.pallas_call(
        _fnet_chan_kernel,
        out_shape=jax.ShapeDtypeStruct((bsz, 2, t, d), BF16),
        grid=(bsz, t // tm),
        in_specs=[pl.BlockSpec((None, tm, d), lambda b, i: (b, i, 0)),
                  pl.BlockSpec((1, d), lambda b, i: (0, 0)),
                  pl.BlockSpec((None, 1, d), vec),
                  pl.BlockSpec((None, 1, d), vec),
                  pl.BlockSpec((gd, 2 * gd), lambda b, i: (0, 0))],
        out_specs=pl.BlockSpec((None, 2, tm, d), lambda b, i: (b, 0, i, 0)),
        compiler_params=_params("parallel", "parallel"),
        name="fnet_chan",
    )(x, norm_g.reshape(1, d), sc, sh, cs_mat)
    hcs = hcs.reshape(bsz, 2 * t, d)
    tm2 = _tile(TOKEN_TILE, t)
    return pl.pallas_call(
        _fnet_time_kernel,
        out_shape=jax.ShapeDtypeStruct((bsz, t, d), F32),
        grid=(bsz, t // tm2),
        in_specs=[pl.BlockSpec((tm2, 2 * t), lambda b, i: (i, 0)),
                  pl.BlockSpec((None, 2 * t, d), lambda b, i: (b, 0, 0)),
                  pl.BlockSpec((d, d), lambda b, i: (0, 0)),
                  pl.BlockSpec((1, d), lambda b, i: (0, 0)),
                  pl.BlockSpec((None, tm2, d), lambda b, i: (b, i, 0)),
                  pl.BlockSpec((None, 1, d), vec)],
        out_specs=pl.BlockSpec((None, tm2, d), lambda b, i: (b, i, 0)),
        compiler_params=_params("parallel", "parallel"),
        name="fnet_time",
    )(f_mat, hcs, wo_bf, bo.reshape(1, d), x, gate)


def _shifted_grid(h, h_above, h_below):
    tm, d = h.shape
    q = d // 4
    colw = lax.broadcasted_iota(jnp.int32, (tm, q), 0) % GRID_W
    left = jnp.where(colw != 0, pltpu.roll(h[:, :q], 1, 0), 0.0)
    right = jnp.where(colw != GRID_W - 1, pltpu.roll(h[:, q:2 * q], tm - 1, 0), 0.0)
    up = jnp.concatenate([h_above[:, 2 * q:3 * q], h[:tm - GRID_W, 2 * q:3 * q]], axis=0)
    down = jnp.concatenate([h[GRID_W:, 3 * q:], h_below[:, 3 * q:]], axis=0)
    return jnp.concatenate([left, right, up, down], axis=1)


def _shifted_seq(h):
    tm, d = h.shape
    half = d // 2
    row = lax.broadcasted_iota(jnp.int32, (tm, half), 0)
    prev = jnp.where(row != 0, pltpu.roll(h[:, :half], 1, 0), 0.0)
    nxt = jnp.where(row != tm - 1, pltpu.roll(h[:, half:], tm - 1, 0), 0.0)
    return jnp.concatenate([prev, nxt], axis=1)


def _rwkv_proj_kernel(*refs, has_vres, n_ctx_tiles):
    it = iter(refs)
    c_ref, x_ref, xa_ref, xb_ref = next(it), next(it), next(it), next(it)
    ng_ref, scc_ref, shc_ref, scx_ref, shx_ref = next(it), next(it), next(it), next(it), next(it)
    vf_ref = next(it) if has_vres else None
    mix_ref, wr_ref, wk_ref, wv_ref = next(it), next(it), next(it), next(it)
    w0_ref, w1_ref, w2_ref = next(it), next(it), next(it)
    a0_ref, a1_ref, a2_ref = next(it), next(it), next(it)
    g1_ref, g2_ref, kk_ref, ka_ref = next(it), next(it), next(it), next(it)
    if has_vres:
        v0_ref, v1_ref, v2_ref = next(it), next(it), next(it)
    r_out, v_out, g_out, a_out = next(it), next(it), next(it), next(it)
    lw_out, kd_out, bd_out = next(it), next(it), next(it)

    i = pl.program_id(1)
    is_ctx = i < n_ctx_tiles
    ng = ng_ref[...]
    scx, shx = scx_ref[...], shx_ref[...]
    h = _norm_mod(jnp.where(is_ctx, c_ref[...], x_ref[...]), ng,
                  jnp.where(is_ctx, scc_ref[...], scx), jnp.where(is_ctx, shc_ref[...], shx))
    h_above = jnp.where(i > n_ctx_tiles, _norm_mod(xa_ref[...], ng, scx, shx), 0.0)
    h_below = jnp.where(i < pl.num_programs(1) - 1, _norm_mod(xb_ref[...], ng, scx, shx), 0.0)
    hs = jnp.where(is_ctx, _shifted_seq(h), _shifted_grid(h, h_above, h_below))
    xx = hs - h
    xr, xw, xk, xv, xa, xg = [(h + xx * mix_ref[j:j + 1, :]).astype(BF16) for j in range(6)]
    r = _dot(xr, wr_ref[...])
    k = _dot(xk, wk_ref[...])
    v = _dot(xv, wv_ref[...])
    if has_vres:
        lora = _dot(_dot(xv, v1_ref[...]).astype(BF16), v2_ref[...])
        v = v + (vf_ref[...] - v) * _sigmoid(v0_ref[...] + lora)
    g = _dot(_sigmoid(_dot(xg, g1_ref[...])).astype(BF16), g2_ref[...])
    ones = _seg_ones(min(LANE_GROUP, h.shape[1]))
    kk = k * kk_ref[...]
    kk = kk * lax.rsqrt(_head_sum(kk * kk, ones) + 1e-12)
    r_out[...] = r.astype(BF16)
    v_out[...] = v
    g_out[...] = g.astype(BF16)
    a_out[...] = (-kk).astype(BF16)
    k_in = k * ka_ref[...]
    k_out = k - k_in
    for d in range(2):
        wlog = w0_ref[d:d + 1, :] + _dot(jnp.tanh(_dot(xw, w1_ref[d])).astype(BF16), w2_ref[d])
        lw_out[d] = (-0.5 * DECAY_SCALE) * (jnp.tanh(0.5 * wlog) + 1.0)
        a = _sigmoid(a0_ref[d:d + 1, :] + _dot(_dot(xa, a1_ref[d]).astype(BF16), a2_ref[d]))
        kd_out[d] = (k_out + k_in * a).astype(BF16)
        bd_out[d] = (kk * a).astype(BF16)


def _rwkv_project(ctx, x, norm_g, mod_c, mod_x, vfirst, p):
    bsz, t, d = x.shape
    t_ctx = ctx.shape[1]
    tm = _tile(TOKEN_TILE, t, t_ctx)
    nct = t_ctx // tm
    assert nct == 1, "the sequence shift handles a context that fits one tile"
    ntx = t // tm
    has_vres = vfirst is not None
    tok = pl.BlockSpec((None, tm, d), lambda b, i: (b, i, 0))
    tok2 = pl.BlockSpec((2, None, tm, d), lambda b, i: (0, b, i, 0))
    vec = pl.BlockSpec((None, 1, d), lambda b, i: (b, 0, 0))

    def full(a):
        nd = a.ndim
        return pl.BlockSpec(a.shape, lambda b, i: (0,) * nd)

    bf = lambda a: a.astype(BF16)
    weights = [p["mix"], bf(p["wr"]), bf(p["wk"]), bf(p["wv"]),
               p["w0"], bf(p["w1"]), bf(p["w2"]), p["a0"], bf(p["a1"]), bf(p["a2"]),
               bf(p["g1"]), bf(p["g2"]), p["kk"].reshape(1, d), p["ka"].reshape(1, d)]
    if has_vres:
        weights += [p["v0"].reshape(1, d), bf(p["v1"]), bf(p["v2"])]
    rpt = tm // GRID_W
    last = t // GRID_W - 1
    acts = [ctx, x, x, x, norm_g.reshape(1, d), *mod_c, *mod_x]
    act_specs = [pl.BlockSpec((None, tm, d), lambda b, i: (b, jnp.minimum(i, nct - 1), 0)),
                 pl.BlockSpec((None, tm, d), lambda b, i: (b, jnp.maximum(i - nct, 0), 0)),
                 pl.BlockSpec((None, GRID_W, d), lambda b, i: (b, jnp.maximum((i - nct) * rpt - 1, 0), 0)),
                 pl.BlockSpec((None, GRID_W, d), lambda b, i: (b, jnp.clip((i - nct + 1) * rpt, 0, last), 0)),
                 pl.BlockSpec((1, d), lambda b, i: (0, 0)), vec, vec, vec, vec]
    if has_vres:
        acts.append(vfirst)
        act_specs.append(tok)
    tt = t_ctx + t
    out_shape = [jax.ShapeDtypeStruct((bsz, tt, d), BF16),
                 jax.ShapeDtypeStruct((bsz, tt, d), F32),
                 jax.ShapeDtypeStruct((bsz, tt, d), BF16),
                 jax.ShapeDtypeStruct((bsz, tt, d), BF16),
                 jax.ShapeDtypeStruct((2, bsz, tt, d), F32),
                 jax.ShapeDtypeStruct((2, bsz, tt, d), BF16),
                 jax.ShapeDtypeStruct((2, bsz, tt, d), BF16)]
    return pl.pallas_call(
        functools.partial(_rwkv_proj_kernel, has_vres=has_vres, n_ctx_tiles=nct),
        out_shape=out_shape,
        grid=(bsz, nct + ntx),
        in_specs=act_specs + [full(w) for w in weights],
        out_specs=[tok, tok, tok, tok, tok2, tok2, tok2],
        compiler_params=_params("parallel", "parallel"),
        name="rwkv_proj",
    )(*acts, *weights)


def _wkv_kernel(r_ref, lw_ref, k_ref, v_ref, a_ref, b_ref, y_ref, s_scr, lhs_scr, n_scr, wl_scr, *, reverse):
    L = WKV_CHUNK
    tb, d = r_ref.shape
    nch = tb // L
    gw = s_scr.shape[-1]
    hpg = gw // HEAD_SIZE
    ng = d // gw

    @pl.when(pl.program_id(1) == 0)
    def _():
        s_scr[...] = jnp.zeros_like(s_scr)

    ti = lax.broadcasted_iota(jnp.int32, (L, L), 0)
    si = lax.broadcasted_iota(jnp.int32, (L, L), 1)
    tri = jnp.where((si >= ti) if reverse else (si <= ti), 1.0, 0.0).astype(BF16)
    last, mid = (0, L // 2) if reverse else (L - 1, L // 2 - 1)
    trow = lax.broadcasted_iota(jnp.int32, (L, gw), 0)
    scol = lax.broadcasted_iota(jnp.int32, (L, gw), 1) % HEAD_SIZE
    strict = (scol > trow) if reverse else (scol < trow)
    incl = (scol >= trow) if reverse else (scol <= trow)
    eye = jnp.where(scol == trow, 1.0, 0.0)
    bdm = (lax.broadcasted_iota(jnp.int32, (gw, gw), 0) // HEAD_SIZE
           == lax.broadcasted_iota(jnp.int32, (gw, gw), 1) // HEAD_SIZE)

    def bdiag(z):
        return jnp.where(bdm, jnp.concatenate([z] * hpg, axis=0), 0.0).astype(BF16)

    def per_head(f, lo):
        return jnp.concatenate([f[h * HEAD_SIZE:(h + 1) * HEAD_SIZE, lo:lo + L] for h in range(hpg)], axis=1)

    def compact(f):
        fm = jnp.where(bdm, f, 0.0)
        out = fm[:HEAD_SIZE]
        for h in range(1, hpg):
            out = out + fm[h * HEAD_SIZE:(h + 1) * HEAD_SIZE]
        return out

    def prep(j):
        rows = slice(j * L, (j + 1) * L)
        lw = lw_ref[rows, :]
        p1, p2, p3 = _split3(lw)
        cw = _dot(tri, p1) + _dot(tri, p2) + _dot(tri, p3)
        cw_end = cw[last:last + 1, :]
        cw_mid = cw[mid:mid + 1, :]
        r = r_ref[rows, :].astype(F32)
        k = k_ref[rows, :].astype(F32)
        a = a_ref[rows, :].astype(F32)
        b = b_ref[rows, :].astype(F32)
        e_in = jnp.exp(cw)
        e_ex = jnp.exp(cw - lw)
        e_inv = jnp.exp(cw_mid - cw)
        e_mid = jnp.exp(-cw_mid)
        e_end = e_inv * jnp.exp(cw_end - cw_mid)
        r0 = r * e_in
        a0 = a * e_ex
        return dict(rows=rows, cw_end=cw_end, r0=r0, a0=a0, ra=r0 * e_mid, aa=a0 * e_mid, kb=k * e_inv, bb=b * e_inv,
                    ke=k * e_end, be=b * e_end)

    for w0 in range(0, nch, WKV_WAVE):
        pj = {j: prep(j) for j in range(w0, min(nch, w0 + WKV_WAVE))}
        ch = [(j, slice(g * gw, (g + 1) * gw)) for j in pj for g in range(ng)]
        cs = range(len(ch))
        col = lambda name, c: pj[ch[c][0]][name][:, ch[c][1]]
        xq = [jnp.concatenate([col("aa", c), col("ra", c)], axis=0).astype(BF16) for c in cs]
        gb = [_dot_nt(xq[c], bdiag(col("bb", c))) for c in cs]
        gk = [_dot_nt(xq[c], bdiag(col("kb", c))) for c in cs]
        a_ab = [jnp.where(strict, gb[c][:L], 0.0) for c in cs]
        vg = [v_ref[pj[ch[c][0]]["rows"], ch[c][1]] for c in cs]
        ft = [jnp.concatenate([col("be", c), col("ke", c)], axis=0).T for c in cs]
        bet = [per_head(ft[c], 0) for c in cs]
        ket = [per_head(ft[c], L) for c in cs]
        akrk = [jnp.concatenate([jnp.where(strict, gk[c][:L], 0.0), jnp.where(incl, gk[c][L:], 0.0), ket[c]],
                                axis=0).astype(BF16) for c in cs]
        tmat = [eye + a_ab[c] for c in cs]
        pw = [_dot(a_ab[c].astype(BF16), bdiag(a_ab[c])) for c in cs]
        avk = [_dot(akrk[c], bdiag(vg[c])) for c in cs]
        n_lvl = L.bit_length() - 2
        for lvl in range(n_lvl):
            if lvl + 1 < n_lvl:
                z = [_dot(jnp.concatenate([tmat[c], pw[c]], axis=0).astype(BF16), bdiag(pw[c])) for c in cs]
                tmat = [tmat[c] + z[c][:L] for c in cs]
                pw = [z[c][L:] for c in cs]
            else:
                z = [_dot(tmat[c].astype(BF16), bdiag(pw[c])) for c in cs]
                tmat = [tmat[c] + z[c] for c in cs]
        rbe = [jnp.concatenate([jnp.where(incl, gb[c][L:], 0.0), bet[c]], axis=0).astype(BF16) for c in cs]
        rt = [_dot(rbe[c], bdiag(tmat[c])).astype(BF16) for c in cs]
        za = [_dot(rt[c], bdiag(col("a0", c))) for c in cs]
        zv = [_dot(rt[c], bdiag(avk[c][:L])) for c in cs]
        ridx = lax.broadcasted_iota(jnp.int32, (LANES, d), 0)
        ends = jnp.zeros((LANES, d), F32)
        for n, j in enumerate(pj):
            ends = jnp.where(ridx == n, pj[j]["cw_end"], ends)
        w_end = jnp.exp(ends.T)
        for c in cs:
            j, sl = ch[c]
            g = c % ng
            y_ref[pj[j]["rows"], sl] = zv[c][:L] + avk[c][L:2 * L]
            lhs_scr[j, g, :L, :] = (col("r0", c) + za[c][:L]).astype(BF16)
            lhs_scr[j, g, L:, :] = za[c][L:].astype(BF16)
            n_scr[j, g] = zv[c][L:] + avk[c][2 * L:]
            wcol = w_end[sl, c // ng:c // ng + 1]
            wl_scr[j, g] = compact(jnp.broadcast_to(wcol, (gw, gw)))

    for j in (reversed(range(nch)) if reverse else range(nch)):
        rows = slice(j * L, (j + 1) * L)
        s_old = [s_scr[g] for g in range(ng)]
        z = [_dot(lhs_scr[j, g], bdiag(s_old[g])) for g in range(ng)]
        for g in range(ng):
            y_ref[rows, g * gw:(g + 1) * gw] += z[g][:L]
            s_scr[g] = wl_scr[j, g] * s_old[g] + z[g][L:] + n_scr[j, g]


def _wkv_scan(r, lw, k, v, a, b, d_idx, t_ctx, reverse):
    bsz, t, d = r.shape
    L = WKV_CHUNK
    tb = _tile(WKV_BLOCK, t_ctx, t - t_ctx)
    nb = t // tb
    n_ctx_blocks = t_ctx // tb
    nch = tb // L
    gw = min(LANE_GROUP, d)
    ng = d // gw
    if reverse:
        cidx = lambda c: jnp.where(c < n_ctx_blocks, n_ctx_blocks - 1 - c, nb + n_ctx_blocks - 1 - c)
    else:
        cidx = lambda c: c
    tok = pl.BlockSpec((None, tb, d), lambda bb, c: (bb, cidx(c), 0))
    tokd = pl.BlockSpec((None, None, tb, d), lambda bb, c: (d_idx, bb, cidx(c), 0))
    return pl.pallas_call(
        functools.partial(_wkv_kernel, reverse=reverse),
        out_shape=jax.ShapeDtypeStruct((bsz, t, d), F32),
        grid=(bsz, nb),
        in_specs=[tok, tokd, tokd, tok, tok, tokd],
        out_specs=tok,
        scratch_shapes=[pltpu.VMEM((ng, HEAD_SIZE, gw), F32),
                        pltpu.VMEM((nch, ng, 2 * L, gw), BF16),
                        pltpu.VMEM((nch, ng, HEAD_SIZE, gw), F32),
                        pltpu.VMEM((nch, ng, HEAD_SIZE, gw), F32)],
        compiler_params=_params("parallel", "arbitrary"),
        name="wkv_rev" if reverse else "wkv_fwd",
    )(r, lw, k, v, a, b)


def _rwkv_out_kernel(y0_ref, y1_ref, r_ref, k_ref, v_ref, g_ref, rk_ref, lnw_ref, lnb_ref,
                     wo_ref, x_ref, gate_ref, o_ref):
    y = y0_ref[...] + y1_ref[...]
    ones = _seg_ones(min(LANE_GROUP, y.shape[1]))
    inv_n = 1.0 / HEAD_SIZE
    mu = _head_sum(y, ones) * inv_n
    yc = y - mu
    var = _head_sum(yc * yc, ones) * inv_n
    yn = yc * lax.rsqrt(var + LNX_EPS) * lnw_ref[...] + lnb_ref[...]
    ksum = k_ref[0].astype(F32) + k_ref[1].astype(F32)
    bonus = _head_sum(r_ref[...].astype(F32) * ksum * rk_ref[...], ones) * v_ref[...]
    out = ((yn + bonus) * g_ref[...].astype(F32)).astype(BF16)
    o_ref[...] = x_ref[...] + gate_ref[...] * _dot(out, wo_ref[...])


def _rwkv_output(y0, y1, r, kd, v, g, rk, lnw, lnb, wo_bf, x, gate, t_off):
    bsz, t, d = x.shape
    tm = _tile(TOKEN_TILE, t, t_off)
    off = t_off // tm
    tok = pl.BlockSpec((None, tm, d), lambda b, i: (b, i + off, 0))
    tok2 = pl.BlockSpec((2, None, tm, d), lambda b, i: (0, b, i + off, 0))
    row = pl.BlockSpec((1, d), lambda b, i: (0, 0))
    return pl.pallas_call(
        _rwkv_out_kernel,
        out_shape=jax.ShapeDtypeStruct((bsz, t, d), F32),
        grid=(bsz, t // tm),
        in_specs=[tok, tok, tok, tok2, tok, tok, row, row, row,
                  pl.BlockSpec((d, d), lambda b, i: (0, 0)),
                  pl.BlockSpec((None, tm, d), lambda b, i: (b, i, 0)),
                  pl.BlockSpec((None, 1, d), lambda b, i: (b, 0, 0))],
        out_specs=pl.BlockSpec((None, tm, d), lambda b, i: (b, i, 0)),
        compiler_params=_params("parallel", "parallel"),
        name="rwkv_out",
    )(y0, y1, r, kd, v, g, rk.reshape(1, d), lnw.reshape(1, d), lnb.reshape(1, d), wo_bf, x, gate)


def _moe_route_kernel(x_ref, g_ref, sc_ref, sh_ref, rt_ref, xs_ref, gs_ref, code_ref,
                      h_scr, aff_scr, *, cap):
    t, d = x_ref.shape
    n_e = rt_ref.shape[0]

    @pl.when(pl.program_id(1) == 0)
    def _():
        g = g_ref[...]
        sc = sc_ref[...]
        sh = sh_ref[...]
        rt_hi, rt_lo = _split2(rt_ref[...])
        tc = _tile(TOKEN_TILE, t)
        for j in range(t // tc):
            h = _norm_mod(x_ref[j * tc:(j + 1) * tc, :], g, sc, sh)
            h_hi, h_lo = _split2(h)
            h_scr[j * tc:(j + 1) * tc, :] = h_hi
            aff_scr[:, j * tc:(j + 1) * tc] = (_dot_nt(rt_hi, h_hi) + _dot_nt(rt_hi, h_lo)
                                               + _dot_nt(rt_lo, h_hi))
        logits = aff_scr[...]
        m = jnp.max(logits, axis=0, keepdims=True)
        ex = jnp.exp(logits - m)
        aff = ex / jnp.sum(ex, axis=0, keepdims=True)
        aff_scr[...] = aff
        bits = pltpu.bitcast(aff, jnp.int32)

        def count_ge(cand):
            return jnp.sum(jnp.where(bits >= cand, 1, 0), axis=1, keepdims=True)

        def search(i, thr):
            lo = 28 - 2 * i
            c1, c2, c3 = [thr | (jnp.int32(m) << lo) for m in (1, 2, 3)]
            thr = jnp.where(count_ge(c1) >= cap, c1, thr)
            thr = jnp.where(count_ge(c2) >= cap, c2, thr)
            return jnp.where(count_ge(c3) >= cap, c3, thr)

        thr = jnp.zeros((n_e, 1), jnp.int32)
        thr = jnp.where(count_ge(thr | (jnp.int32(1) << 30)) >= cap, thr | (jnp.int32(1) << 30), thr)
        thr = lax.fori_loop(0, 15, search, thr)
        gt = bits > thr
        eq = bits == thr
        key = jnp.where(gt, 1, 0) + jnp.where(eq, 1 << TIE_BITS, 0)
        lane = lax.broadcasted_iota(jnp.int32, (n_e, t), 1)
        csum = key
        sh_amt = 1
        while sh_amt < t:
            csum = csum + jnp.where(lane >= sh_amt, pltpu.roll(csum, sh_amt, 1), 0)
            sh_amt *= 2
        before = csum - key
        n_gt = before & ((1 << TIE_BITS) - 1)
        n_eq = before >> TIE_BITS
        need = cap - jnp.sum(jnp.where(gt, 1, 0), axis=1, keepdims=True)
        sel = gt | (eq & (n_eq < need))
        code_ref[...] = jnp.where(sel, n_gt + jnp.minimum(n_eq, need), -1)

    eg = xs_ref.shape[0]
    e0 = pl.program_id(1) * eg
    slot = lax.broadcasted_iota(jnp.int32, (cap, t), 0)
    onehot = []
    for l in range(eg):
        hit = code_ref[pl.ds(e0 + l, 1), :] == slot
        onehot.append(jnp.where(hit, 1.0, 0.0).astype(BF16))
        gs_ref[l] = jnp.sum(jnp.where(hit, aff_scr[pl.ds(e0 + l, 1), :], 0.0), axis=1, keepdims=True)
    gathered = _dot(onehot[0] if eg == 1 else jnp.concatenate(onehot, axis=0), h_scr[...])
    xs_ref[...] = gathered.reshape(eg, cap, d).astype(BF16)


def _moe_route(x, norm_g, sc, sh, router):
    bsz, t, d = x.shape
    n_e = router.shape[1]
    cap = CAPACITY_FACTOR * t // n_e
    assert t < (1 << TIE_BITS)
    eg = _tile(max(1, MOE_GATHER_ROWS // cap), n_e)
    vec = lambda b, e: (b, 0, 0)
    return pl.pallas_call(
        functools.partial(_moe_route_kernel, cap=cap),
        out_shape=[jax.ShapeDtypeStruct((bsz, n_e, cap, d), BF16),
                   jax.ShapeDtypeStruct((bsz, n_e, cap, 1), F32),
                   jax.ShapeDtypeStruct((bsz, n_e, t), jnp.int32)],
        grid=(bsz, n_e // eg),
        in_specs=[pl.BlockSpec((None, t, d), lambda b, e: (b, 0, 0)),
                  pl.BlockSpec((1, d), lambda b, e: (0, 0)),
                  pl.BlockSpec((None, 1, d), vec),
                  pl.BlockSpec((None, 1, d), vec),
                  pl.BlockSpec((n_e, d), lambda b, e: (0, 0))],
        out_specs=[pl.BlockSpec((None, eg, cap, d), lambda b, e: (b, e, 0, 0)),
                   pl.BlockSpec((None, eg, cap, 1), lambda b, e: (b, e, 0, 0)),
                   pl.BlockSpec((None, n_e, t), lambda b, e: (b, 0, 0))],
        scratch_shapes=[pltpu.VMEM((t, d), BF16), pltpu.VMEM((n_e, t), F32)],
        compiler_params=_params("parallel", "arbitrary"),
        name="moe_route",
    )(x, norm_g.reshape(1, d), sc, sh, router.T)


def _moe_ffn_kernel(xs_ref, gs_ref, wg_ref, wu_ref, wd_ref, ys_ref, wg_scr, wu_scr, wd_scr):
    @pl.when(pl.program_id(1) == 0)
    def _():
        wg_scr[...] = wg_ref[...].astype(BF16)
        wu_scr[...] = wu_ref[...].astype(BF16)
        wd_scr[...] = wd_ref[...].astype(BF16)

    tb, cap, d = xs_ref.shape
    xs = xs_ref[...].reshape(tb * cap, d)
    hg = _dot(xs, wg_scr[...])
    hu = _dot(xs, wu_scr[...])
    hid = (hg * _sigmoid(hg) * hu).astype(BF16)
    ys = _dot(hid, wd_scr[...]) * gs_ref[...].reshape(tb * cap, 1)
    ys_ref[...] = ys.reshape(tb, cap, d).astype(BF16)


def _moe_ffn(xs, gs, wg, wu, wd, layer):
    bsz, n_e, cap, d = xs.shape
    f = wg.shape[-1]
    tb = max(1, min(bsz, MOE_FFN_ROWS // cap))
    while bsz % tb:
        tb -= 1
    return pl.pallas_call(
        _moe_ffn_kernel,
        out_shape=jax.ShapeDtypeStruct((bsz, n_e, cap, d), BF16),
        grid=(n_e, bsz // tb),
        in_specs=[pl.BlockSpec((tb, None, cap, d), lambda e, j: (j, e, 0, 0)),
                  pl.BlockSpec((tb, None, cap, 1), lambda e, j: (j, e, 0, 0)),
                  pl.BlockSpec((None, None, d, f), lambda e, j: (layer, e, 0, 0)),
                  pl.BlockSpec((None, None, d, f), lambda e, j: (layer, e, 0, 0)),
                  pl.BlockSpec((None, None, f, d), lambda e, j: (layer, e, 0, 0))],
        out_specs=pl.BlockSpec((tb, None, cap, d), lambda e, j: (j, e, 0, 0)),
        scratch_shapes=[pltpu.VMEM((d, f), BF16), pltpu.VMEM((d, f), BF16), pltpu.VMEM((f, d), BF16)],
        compiler_params=_params("parallel", "arbitrary"),
        name="moe_ffn",
    )(xs, gs, wg, wu, wd)


def _moe_combine_kernel(code_ref, ys_ref, x_ref, gate_ref, fg_ref, o_ref, *, cap, final_norm):
    n_e = code_ref.shape[1]
    slot = lax.broadcasted_iota(jnp.int32, (1, cap), 1)
    pieces = [jnp.where(code_ref[:, e:e + 1] == slot, 1.0, 0.0).astype(BF16) for e in range(n_e)]
    scat = jnp.concatenate(pieces, axis=1)
    x = x_ref[...] + gate_ref[...] * _dot(scat, ys_ref[...])
    if final_norm:
        x = x * lax.rsqrt(jnp.mean(x * x, axis=-1, keepdims=True) + NORM_EPS) * fg_ref[...]
    o_ref[...] = x


def _moe_combine(code_t, ys, x, gate, final_g, final_norm):
    bsz, t, d = x.shape
    n_e = code_t.shape[-1]
    cap = ys.shape[1] // n_e
    tm = _tile(MOE_COMBINE_ROWS, t)
    return pl.pallas_call(
        functools.partial(_moe_combine_kernel, cap=cap, final_norm=final_norm),
        out_shape=jax.ShapeDtypeStruct((bsz, t, d), F32),
        grid=(bsz, t // tm),
        in_specs=[pl.BlockSpec((None, tm, n_e), lambda b, i: (b, i, 0)),
                  pl.BlockSpec((None, n_e * cap, d), lambda b, i: (b, 0, 0)),
                  pl.BlockSpec((None, tm, d), lambda b, i: (b, i, 0)),
                  pl.BlockSpec((None, 1, d), lambda b, i: (b, 0, 0)),
                  pl.BlockSpec((1, d), lambda b, i: (0, 0))],
        out_specs=pl.BlockSpec((None, tm, d), lambda b, i: (b, i, 0)),
        compiler_params=_params("parallel", "parallel"),
        name="moe_combine",
    )(code_t, ys, x, gate, final_g.reshape(1, d))


def _ec_moe_layer(x, norm_g, sc, sh, gate, router, wg, wu, wd, layer, final_g, final_norm):
    bsz, t, d = x.shape
    xs, gs, code = _moe_route(x, norm_g, sc, sh, router)
    ys = _moe_ffn(xs, gs, wg, wu, wd, layer)
    n_e, cap = xs.shape[1], xs.shape[2]
    return _moe_combine(jnp.swapaxes(code, 1, 2), ys.reshape(bsz, n_e * cap, d), x, gate,
                        final_g, final_norm)


def kernel(x, c, ctx, c_ctx, ada_w, ada_b, norm_g, fnet_wo, fnet_bo, rw_mix, rw_wr, rw_wk, rw_wv, rw_wo,
           rw_w0, rw_w1, rw_w2, rw_a0, rw_a1, rw_a2, rw_v0, rw_v1, rw_v2, rw_g1, rw_g2, rw_kk, rw_ka, rw_rk,
           rw_lnx_w, rw_lnx_b, moe_router, moe_wg, moe_wu, moe_wd, final_g):
    bsz, t, d = x.shape
    t_ctx = ctx.shape[1]
    depth = ada_w.shape[0]
    n_mixers = 2

    rows = -(-(bsz + 1) // 8) * 8
    cc = jnp.concatenate([c, c_ctx[None, :], jnp.zeros((rows - bsz - 1, d), F32)], axis=0)
    mods = _adaln(cc, ada_w, ada_b)

    def mod_x(i, j):
        return mods[i, :bsz, j * d:(j + 1) * d].reshape(bsz, 1, d)

    def mod_c(i, j):
        return jnp.broadcast_to(mods[i, bsz, j * d:(j + 1) * d].reshape(1, 1, d), (bsz, 1, d))

    mats_x = _dft_mats(t, d // FNET_GROUPS)
    mats_c = _dft_mats(t_ctx, d // FNET_GROUPS)
    vfirst = None
    for i in range(depth):
        need_ctx = i < depth - 1
        if i % n_mixers == 0:
            fi = i // n_mixers
            wo_bf = fnet_wo[fi].astype(BF16)
            x = _fnet_layer(x, norm_g[i, 0], mod_x(i, 1), mod_x(i, 0), mod_x(i, 2), wo_bf, fnet_bo[fi], mats_x)
            if need_ctx:
                ctx = _fnet_layer(ctx, norm_g[i, 0], mod_c(i, 1), mod_c(i, 0), mod_c(i, 2), wo_bf,
                                  fnet_bo[fi], mats_c)
        else:
            ri = i // n_mixers
            p = dict(mix=rw_mix[ri], wr=rw_wr[ri], wk=rw_wk[ri], wv=rw_wv[ri], w0=rw_w0[ri], w1=rw_w1[ri],
                     w2=rw_w2[ri], a0=rw_a0[ri], a1=rw_a1[ri], a2=rw_a2[ri], g1=rw_g1[ri], g2=rw_g2[ri],
                     kk=rw_kk[ri], ka=rw_ka[ri])
            if ri > 0:
                p.update(v0=rw_v0[ri - 1], v1=rw_v1[ri - 1], v2=rw_v2[ri - 1])
            vf = vfirst if ri > 0 else None
            r, v, g, a, lw, kd, bd = _rwkv_project(ctx, x, norm_g[i, 0], (mod_c(i, 1), mod_c(i, 0)),
                                                   (mod_x(i, 1), mod_x(i, 0)), vf, p)
            if ri == 0:
                vfirst = v
            y0 = _wkv_scan(r, lw, kd, v, a, bd, 0, t_ctx, False)
            y1 = _wkv_scan(r, lw, kd, v, a, bd, 1, t_ctx, True)
            wo_bf = rw_wo[ri].astype(BF16)
            args = (y0, y1, r, kd, v, g, rw_rk[ri], rw_lnx_w[ri], rw_lnx_b[ri], wo_bf)
            x = _rwkv_output(*args, x, mod_x(i, 2), t_ctx)
            if need_ctx:
                ctx = _rwkv_output(*args, ctx, mod_c(i, 2), 0)
        last = i == depth - 1
        x = _ec_moe_layer(x, norm_g[i, 1], mod_x(i, 4), mod_x(i, 3), mod_x(i, 5), moe_router[i],
                          moe_wg, moe_wu, moe_wd, i, final_g, last)
        if need_ctx:
            ctx = _ec_moe_layer(ctx, norm_g[i, 1], mod_c(i, 4), mod_c(i, 3), mod_c(i, 5), moe_router[i],
                                moe_wg, moe_wu, moe_wd, i, final_g, False)
    return x
```

```python
import functools

import jax
import jax.numpy as jnp
from jax import lax
from jax.experimental import pallas as pl
from jax.experimental.pallas import tpu as pltpu

F32 = jnp.float32
BF16 = jnp.bfloat16

HEAD_SIZE = 64
LANES = 128
LANE_GROUP = 256
GRID_W = 64
FNET_GROUPS = 4
CAPACITY_FACTOR = 2
NORM_EPS = 1e-6
LNX_EPS = 64e-5
DECAY_SCALE = 0.6065306597126334
WKV_CHUNK = 64
WKV_BLOCK = 256
WKV_WAVE = 4
MOE_GATHER_ROWS = 1024
MOE_FFN_ROWS = 512
MOE_COMBINE_ROWS = 1024
TOKEN_TILE = 256
CHAN_TILE = 512
ADALN_COLS = 1536
TIE_BITS = 12
VMEM_LIMIT_BYTES = 56 * 1024 * 1024


def _params(*semantics):
    return pltpu.CompilerParams(dimension_semantics=semantics, vmem_limit_bytes=VMEM_LIMIT_BYTES)


def _tile(pref, *extents):
    tm = pref
    while any(n % tm for n in extents if n):
        tm //= 2
    return tm


def _dot(a, b):
    return jnp.dot(a, b, preferred_element_type=F32)


def _dot_nt(a, b):
    return lax.dot_general(a, b, (((1,), (1,)), ((), ())), preferred_element_type=F32)


def _split2(x):
    hi = x.astype(BF16)
    lo = (x - hi.astype(F32)).astype(BF16)
    return hi, lo


def _split3(x):
    hi = x.astype(BF16)
    r1 = x - hi.astype(F32)
    mid = r1.astype(BF16)
    lo = (r1 - mid.astype(F32)).astype(BF16)
    return hi, mid, lo


def _norm_mod(x, g, sc, sh):
    ms = jnp.mean(x * x, axis=-1, keepdims=True)
    return x * lax.rsqrt(ms + NORM_EPS) * g * (1.0 + sc) + sh


def _sigmoid(x):
    return 0.5 * jnp.tanh(0.5 * x) + 0.5


def _seg_ones(n):
    r = lax.broadcasted_iota(jnp.int32, (n, n), 0) // HEAD_SIZE
    c = lax.broadcasted_iota(jnp.int32, (n, n), 1) // HEAD_SIZE
    return jnp.where(r == c, 1.0, 0.0).astype(BF16)


def _head_sum(x, ones):
    d = x.shape[-1]
    w = ones.shape[0]
    parts = []
    for j in range(d // w):
        hi, lo = _split2(x[:, j * w:(j + 1) * w])
        parts.append(_dot(hi, ones) + _dot(lo, ones))
    return parts[0] if len(parts) == 1 else jnp.concatenate(parts, axis=1)


def _adaln_kernel(c_ref, w_ref, b_ref, o_ref):
    c = c_ref[...]
    s = c * _sigmoid(c)
    s_hi, s_lo = _split2(s)
    w_hi, w_lo = _split2(w_ref[...])
    o_ref[...] = _dot(s_hi, w_hi) + _dot(s_lo, w_hi) + _dot(s_hi, w_lo) + b_ref[...]


def _adaln(cc, ada_w, ada_b):
    depth, d, n = ada_w.shape
    rows = cc.shape[0]
    tn = _tile(ADALN_COLS, n)
    return pl.pallas_call(
        _adaln_kernel,
        out_shape=jax.ShapeDtypeStruct((depth, rows, n), F32),
        grid=(depth, n // tn),
        in_specs=[pl.BlockSpec((rows, d), lambda l, j: (0, 0)),
                  pl.BlockSpec((None, d, tn), lambda l, j: (l, 0, j)),
                  pl.BlockSpec((None, 1, tn), lambda l, j: (l, 0, j))],
        out_specs=pl.BlockSpec((None, rows, tn), lambda l, j: (l, 0, j)),
        compiler_params=_params("parallel", "parallel"),
        name="adaln",
    )(cc, ada_w, ada_b.reshape(depth, 1, n))


def _fnet_chan_kernel(x_ref, g_ref, sc_ref, sh_ref, cs_ref, o_ref):
    h = _norm_mod(x_ref[...], g_ref[...], sc_ref[...], sh_ref[...])
    gd = cs_ref.shape[0]
    cs = cs_ref[...]
    for j in range(h.shape[1] // gd):
        z = _dot(h[:, j * gd:(j + 1) * gd].astype(BF16), cs)
        o_ref[0, :, j * gd:(j + 1) * gd] = z[:, :gd].astype(BF16)
        o_ref[1, :, j * gd:(j + 1) * gd] = z[:, gd:].astype(BF16)


def _fnet_time_kernel(f_ref, hcs_ref, wo_ref, bo_ref, x_ref, gate_ref, o_ref):
    f = _dot(f_ref[...], hcs_ref[...])
    y = _dot(f.astype(BF16), wo_ref[...]) + bo_ref[...]
    o_ref[...] = x_ref[...] + gate_ref[...] * y


def _dft_mats(t, gd):
    def cs(n):
        i = jnp.arange(n, dtype=jnp.int32)
        ang = ((i[:, None] * i[None, :]) % n).astype(F32) * (2.0 * jnp.pi / n)
        return jnp.cos(ang), jnp.sin(ang)
    ct, st = cs(t)
    cc, sc = cs(gd)
    scale = 1.0 / jnp.sqrt(jnp.asarray(t * gd, F32))
    return (jnp.concatenate([ct, -st], axis=1).astype(BF16),
            (jnp.concatenate([cc, sc], axis=1) * scale).astype(BF16))


def _fnet_layer(x, norm_g, sc, sh, gate, wo_bf, bo, mats):
    bsz, t, d = x.shape
    f_mat, cs_mat = mats
    gd = d // FNET_GROUPS
    tm = _tile(CHAN_TILE, t)
    vec = lambda b, i: (b, 0, 0)
    hcs = pl.pallas_call(
        _fnet_chan_kernel,
        out_shape=jax.ShapeDtypeStruct((bsz, 2, t, d), BF16),
        grid=(bsz, t // tm),
        in_specs=[pl.BlockSpec((None, tm, d), lambda b, i: (b, i, 0)),
                  pl.BlockSpec((1, d), lambda b, i: (0, 0)),
                  pl.BlockSpec((None, 1, d), vec),
                  pl.BlockSpec((None, 1, d), vec),
                  pl.BlockSpec((gd, 2 * gd), lambda b, i: (0, 0))],
        out_specs=pl.BlockSpec((None, 2, tm, d), lambda b, i: (b, 0, i, 0)),
        compiler_params=_params("parallel", "parallel"),
        name="fnet_chan",
    )(x, norm_g.reshape(1, d), sc, sh, cs_mat)
    hcs = hcs.reshape(bsz, 2 * t, d)
    tm2 = _tile(CHAN_TILE, t)
    return pl.pallas_call(
        _fnet_time_kernel,
        out_shape=jax.ShapeDtypeStruct((bsz, t, d), F32),
        grid=(bsz, t // tm2),
        in_specs=[pl.BlockSpec((tm2, 2 * t), lambda b, i: (i, 0)),
                  pl.BlockSpec((None, 2 * t, d), lambda b, i: (b, 0, 0)),
                  pl.BlockSpec((d, d), lambda b, i: (0, 0)),
                  pl.BlockSpec((1, d), lambda b, i: (0, 0)),
                  pl.BlockSpec((None, tm2, d), lambda b, i: (b, i, 0)),
                  pl.BlockSpec((None, 1, d), vec)],
        out_specs=pl.BlockSpec((None, tm2, d), lambda b, i: (b, i, 0)),
        compiler_params=_params("parallel", "parallel"),
        name="fnet_time",
    )(f_mat, hcs, wo_bf, bo.reshape(1, d), x, gate)


def _shifted_grid(h, h_above, h_below):
    tm, d = h.shape
    q = d // 4
    colw = lax.broadcasted_iota(jnp.int32, (tm, q), 0) % GRID_W
    left = jnp.where(colw != 0, pltpu.roll(h[:, :q], 1, 0), 0.0)
    right = jnp.where(colw != GRID_W - 1, pltpu.roll(h[:, q:2 * q], tm - 1, 0), 0.0)
    up = jnp.concatenate([h_above[:, 2 * q:3 * q], h[:tm - GRID_W, 2 * q:3 * q]], axis=0)
    down = jnp.concatenate([h[GRID_W:, 3 * q:], h_below[:, 3 * q:]], axis=0)
    return jnp.concatenate([left, right, up, down], axis=1)


def _shifted_seq(h):
    tm, d = h.shape
    half = d // 2
    row = lax.broadcasted_iota(jnp.int32, (tm, half), 0)
    prev = jnp.where(row != 0, pltpu.roll(h[:, :half], 1, 0), 0.0)
    nxt = jnp.where(row != tm - 1, pltpu.roll(h[:, half:], tm - 1, 0), 0.0)
    return jnp.concatenate([prev, nxt], axis=1)


def _rwkv_proj_kernel(*refs, has_vres, n_ctx_tiles):
    it = iter(refs)
    c_ref, x_ref, xa_ref, xb_ref = next(it), next(it), next(it), next(it)
    ng_ref, scc_ref, shc_ref, scx_ref, shx_ref = next(it), next(it), next(it), next(it), next(it)
    vf_ref = next(it) if has_vres else None
    mix_ref, wr_ref, wk_ref, wv_ref = next(it), next(it), next(it), next(it)
    w0_ref, w1_ref, w2_ref = next(it), next(it), next(it)
    a0_ref, a1_ref, a2_ref = next(it), next(it), next(it)
    g1_ref, g2_ref, kk_ref, ka_ref = next(it), next(it), next(it), next(it)
    if has_vres:
        v0_ref, v1_ref, v2_ref = next(it), next(it), next(it)
    r_out, v_out, g_out, a_out = next(it), next(it), next(it), next(it)
    lw_out, kd_out, bd_out = next(it), next(it), next(it)

    i = pl.program_id(1)
    is_ctx = i < n_ctx_tiles
    ng = ng_ref[...]
    scx, shx = scx_ref[...], shx_ref[...]
    h = _norm_mod(jnp.where(is_ctx, c_ref[...], x_ref[...]), ng,
                  jnp.where(is_ctx, scc_ref[...], scx), jnp.where(is_ctx, shc_ref[...], shx))
    h_above = jnp.where(i > n_ctx_tiles, _norm_mod(xa_ref[...], ng, scx, shx), 0.0)
    h_below = jnp.where(i < pl.num_programs(1) - 1, _norm_mod(xb_ref[...], ng, scx, shx), 0.0)
    hs = jnp.where(is_ctx, _shifted_seq(h), _shifted_grid(h, h_above, h_below))
    xx = hs - h
    xr, xw, xk, xv, xa, xg = [(h + xx * mix_ref[j:j + 1, :]).astype(BF16) for j in range(6)]
    r = _dot(xr, wr_ref[...])
    k = _dot(xk, wk_ref[...])
    v = _dot(xv, wv_ref[...])
    if has_vres:
        lora = _dot(_dot(xv, v1_ref[...]).astype(BF16), v2_ref[...])
        v = v + (vf_ref[...] - v) * _sigmoid(v0_ref[...] + lora)
    g = _dot(_sigmoid(_dot(xg, g1_ref[...])).astype(BF16), g2_ref[...])
    ones = _seg_ones(min(LANE_GROUP, h.shape[1]))
    kk = k * kk_ref[...]
    kk = kk * lax.rsqrt(_head_sum(kk * kk, ones) + 1e-12)
    r_out[...] = r.astype(BF16)
    v_out[...] = v
    g_out[...] = g.astype(BF16)
    a_out[...] = (-kk).astype(BF16)
    k_in = k * ka_ref[...]
    k_out = k - k_in
    for d in range(2):
        wlog = w0_ref[d:d + 1, :] + _dot(jnp.tanh(_dot(xw, w1_ref[d])).astype(BF16), w2_ref[d])
        lw_out[d] = (-0.5 * DECAY_SCALE) * (jnp.tanh(0.5 * wlog) + 1.0)
        a = _sigmoid(a0_ref[d:d + 1, :] + _dot(_dot(xa, a1_ref[d]).astype(BF16), a2_ref[d]))
        kd_out[d] = (k_out + k_in * a).astype(BF16)
        bd_out[d] = (kk * a).astype(BF16)


def _rwkv_project(ctx, x, norm_g, mod_c, mod_x, vfirst, p):
    bsz, t, d = x.shape
    t_ctx = ctx.shape[1]
    tm = _tile(TOKEN_TILE, t, t_ctx)
    nct = t_ctx // tm
    assert nct == 1, "the sequence shift handles a context that fits one tile"
    ntx = t // tm
    has_vres = vfirst is not None
    tok = pl.BlockSpec((None, tm, d), lambda b, i: (b, i, 0))
    tok2 = pl.BlockSpec((2, None, tm, d), lambda b, i: (0, b, i, 0))
    vec = pl.BlockSpec((None, 1, d), lambda b, i: (b, 0, 0))

    def full(a):
        nd = a.ndim
        return pl.BlockSpec(a.shape, lambda b, i: (0,) * nd)

    bf = lambda a: a.astype(BF16)
    weights = [p["mix"], bf(p["wr"]), bf(p["wk"]), bf(p["wv"]),
               p["w0"], bf(p["w1"]), bf(p["w2"]), p["a0"], bf(p["a1"]), bf(p["a2"]),
               bf(p["g1"]), bf(p["g2"]), p["kk"].reshape(1, d), p["ka"].reshape(1, d)]
    if has_vres:
        weights += [p["v0"].reshape(1, d), bf(p["v1"]), bf(p["v2"])]
    rpt = tm // GRID_W
    last = t // GRID_W - 1
    acts = [ctx, x, x, x, norm_g.reshape(1, d), *mod_c, *mod_x]
    act_specs = [pl.BlockSpec((None, tm, d), lambda b, i: (b, jnp.minimum(i, nct - 1), 0)),
                 pl.BlockSpec((None, tm, d), lambda b, i: (b, jnp.maximum(i - nct, 0), 0)),
                 pl.BlockSpec((None, GRID_W, d), lambda b, i: (b, jnp.maximum((i - nct) * rpt - 1, 0), 0)),
                 pl.BlockSpec((None, GRID_W, d), lambda b, i: (b, jnp.clip((i - nct + 1) * rpt, 0, last), 0)),
                 pl.BlockSpec((1, d), lambda b, i: (0, 0)), vec, vec, vec, vec]
    if has_vres:
        acts.append(vfirst)
        act_specs.append(tok)
    tt = t_ctx + t
    out_shape = [jax.ShapeDtypeStruct((bsz, tt, d), BF16),
                 jax.ShapeDtypeStruct((bsz, tt, d), F32),
                 jax.ShapeDtypeStruct((bsz, tt, d), BF16),
                 jax.ShapeDtypeStruct((bsz, tt, d), BF16),
                 jax.ShapeDtypeStruct((2, bsz, tt, d), F32),
                 jax.ShapeDtypeStruct((2, bsz, tt, d), BF16),
                 jax.ShapeDtypeStruct((2, bsz, tt, d), BF16)]
    return pl.pallas_call(
        functools.partial(_rwkv_proj_kernel, has_vres=has_vres, n_ctx_tiles=nct),
        out_shape=out_shape,
        grid=(bsz, nct + ntx),
        in_specs=act_specs + [full(w) for w in weights],
        out_specs=[tok, tok, tok, tok, tok2, tok2, tok2],
        compiler_params=_params("parallel", "parallel"),
        name="rwkv_proj",
    )(*acts, *weights)


def _wkv_kernel(*refs, reverse, add_prev):
    r_ref, lw_ref, k_ref, v_ref, a_ref, b_ref = refs[:6]
    yp_ref = refs[6] if add_prev else None
    y_ref, s_scr, lhs_scr, n_scr, wl_scr = refs[-5:]
    _wkv_body(r_ref, lw_ref, k_ref, v_ref, a_ref, b_ref, yp_ref, y_ref, s_scr, lhs_scr, n_scr, wl_scr, reverse)


def _wkv_body(r_ref, lw_ref, k_ref, v_ref, a_ref, b_ref, yp_ref, y_ref, s_scr, lhs_scr, n_scr, wl_scr, reverse):
    L = WKV_CHUNK
    tb, d = r_ref.shape
    nch = tb // L
    gw = s_scr.shape[-1]
    hpg = gw // HEAD_SIZE
    ng = d // gw

    @pl.when(pl.program_id(1) == 0)
    def _():
        s_scr[...] = jnp.zeros_like(s_scr)

    ti = lax.broadcasted_iota(jnp.int32, (L, L), 0)
    si = lax.broadcasted_iota(jnp.int32, (L, L), 1)
    tri = jnp.where((si >= ti) if reverse else (si <= ti), 1.0, 0.0).astype(BF16)
    last, mid = (0, L // 2) if reverse else (L - 1, L // 2 - 1)
    trow = lax.broadcasted_iota(jnp.int32, (L, gw), 0)
    scol = lax.broadcasted_iota(jnp.int32, (L, gw), 1) % HEAD_SIZE
    strict = (scol > trow) if reverse else (scol < trow)
    incl = (scol >= trow) if reverse else (scol <= trow)
    eye = jnp.where(scol == trow, 1.0, 0.0)
    bdm = (lax.broadcasted_iota(jnp.int32, (gw, gw), 0) // HEAD_SIZE
           == lax.broadcasted_iota(jnp.int32, (gw, gw), 1) // HEAD_SIZE)

    def bdiag(z):
        return jnp.where(bdm, jnp.concatenate([z] * hpg, axis=0), 0.0).astype(BF16)

    def per_head(f, lo):
        return jnp.concatenate([f[h * HEAD_SIZE:(h + 1) * HEAD_SIZE, lo:lo + L] for h in range(hpg)], axis=1)

    def compact(f):
        fm = jnp.where(bdm, f, 0.0)
        out = fm[:HEAD_SIZE]
        for h in range(1, hpg):
            out = out + fm[h * HEAD_SIZE:(h + 1) * HEAD_SIZE]
        return out

    def prep(j):
        rows = slice(j * L, (j + 1) * L)
        lw = lw_ref[rows, :]
        p1, p2, p3 = _split3(lw)
        cw = _dot(tri, p1) + _dot(tri, p2) + _dot(tri, p3)
        cw_end = cw[last:last + 1, :]
        cw_mid = cw[mid:mid + 1, :]
        r = r_ref[rows, :].astype(F32)
        k = k_ref[rows, :].astype(F32)
        a = a_ref[rows, :].astype(F32)
        b = b_ref[rows, :].astype(F32)
        e_in = jnp.exp(cw)
        e_ex = jnp.exp(cw - lw)
        e_inv = jnp.exp(cw_mid - cw)
        e_mid = jnp.exp(-cw_mid)
        e_end = e_inv * jnp.exp(cw_end - cw_mid)
        r0 = r * e_in
        a0 = a * e_ex
        return dict(rows=rows, cw_end=cw_end, r0=r0, a0=a0, ra=r0 * e_mid, aa=a0 * e_mid, kb=k * e_inv, bb=b * e_inv,
                    ke=k * e_end, be=b * e_end)

    for w0 in range(0, nch, WKV_WAVE):
        pj = {j: prep(j) for j in range(w0, min(nch, w0 + WKV_WAVE))}
        ch = [(j, slice(g * gw, (g + 1) * gw)) for j in pj for g in range(ng)]
        cs = range(len(ch))
        col = lambda name, c: pj[ch[c][0]][name][:, ch[c][1]]
        xq = [jnp.concatenate([col("aa", c), col("ra", c)], axis=0).astype(BF16) for c in cs]
        gb = [_dot_nt(xq[c], bdiag(col("bb", c))) for c in cs]
        gk = [_dot_nt(xq[c], bdiag(col("kb", c))) for c in cs]
        a_ab = [jnp.where(strict, gb[c][:L], 0.0) for c in cs]
        vg = [v_ref[pj[ch[c][0]]["rows"], ch[c][1]] for c in cs]
        ft = [jnp.concatenate([col("be", c), col("ke", c)], axis=0).T for c in cs]
        bet = [per_head(ft[c], 0) for c in cs]
        ket = [per_head(ft[c], L) for c in cs]
        akrk = [jnp.concatenate([jnp.where(strict, gk[c][:L], 0.0), jnp.where(incl, gk[c][L:], 0.0), ket[c]],
                                axis=0).astype(BF16) for c in cs]
        tmat = [eye + a_ab[c] for c in cs]
        pw = [_dot(a_ab[c].astype(BF16), bdiag(a_ab[c])) for c in cs]
        avk = [_dot(akrk[c], bdiag(vg[c])) for c in cs]
        n_lvl = L.bit_length() - 2
        for lvl in range(n_lvl):
            if lvl + 1 < n_lvl:
                z = [_dot(jnp.concatenate([tmat[c], pw[c]], axis=0).astype(BF16), bdiag(pw[c])) for c in cs]
                tmat = [tmat[c] + z[c][:L] for c in cs]
                pw = [z[c][L:] for c in cs]
            else:
                z = [_dot(tmat[c].astype(BF16), bdiag(pw[c])) for c in cs]
                tmat = [tmat[c] + z[c] for c in cs]
        rbe = [jnp.concatenate([jnp.where(incl, gb[c][L:], 0.0), bet[c]], axis=0).astype(BF16) for c in cs]
        rt = [_dot(rbe[c], bdiag(tmat[c])).astype(BF16) for c in cs]
        za = [_dot(rt[c], bdiag(col("a0", c))) for c in cs]
        zv = [_dot(rt[c], bdiag(avk[c][:L])) for c in cs]
        ridx = lax.broadcasted_iota(jnp.int32, (LANES, d), 0)
        ends = jnp.zeros((LANES, d), F32)
        for n, j in enumerate(pj):
            ends = jnp.where(ridx == n, pj[j]["cw_end"], ends)
        w_end = jnp.exp(ends.T)
        for c in cs:
            j, sl = ch[c]
            g = c % ng
            y_part = zv[c][:L] + avk[c][L:2 * L]
            if yp_ref is not None:
                y_part = y_part + yp_ref[pj[j]["rows"], sl]
            y_ref[pj[j]["rows"], sl] = y_part
            lhs_scr[j, g, :L, :] = (col("r0", c) + za[c][:L]).astype(BF16)
            lhs_scr[j, g, L:, :] = za[c][L:].astype(BF16)
            n_scr[j, g] = zv[c][L:] + avk[c][2 * L:]
            wcol = w_end[sl, c // ng:c // ng + 1]
            wl_scr[j, g] = compact(jnp.broadcast_to(wcol, (gw, gw)))

    for j in (reversed(range(nch)) if reverse else range(nch)):
        rows = slice(j * L, (j + 1) * L)
        s_old = [s_scr[g] for g in range(ng)]
        z = [_dot(lhs_scr[j, g], bdiag(s_old[g])) for g in range(ng)]
        for g in range(ng):
            y_ref[rows, g * gw:(g + 1) * gw] += z[g][:L]
            s_scr[g] = wl_scr[j, g] * s_old[g] + z[g][L:] + n_scr[j, g]


def _wkv_scan(r, lw, k, v, a, b, d_idx, t_ctx, reverse, y_prev=None):
    bsz, t, d = r.shape
    L = WKV_CHUNK
    tb = _tile(WKV_BLOCK, t_ctx, t - t_ctx)
    nb = t // tb
    n_ctx_blocks = t_ctx // tb
    nch = tb // L
    gw = min(LANE_GROUP, d)
    ng = d // gw
    if reverse:
        cidx = lambda c: jnp.where(c < n_ctx_blocks, n_ctx_blocks - 1 - c, nb + n_ctx_blocks - 1 - c)
    else:
        cidx = lambda c: c
    tok = pl.BlockSpec((None, tb, d), lambda bb, c: (bb, cidx(c), 0))
    tokd = pl.BlockSpec((None, None, tb, d), lambda bb, c: (d_idx, bb, cidx(c), 0))
    prev = [] if y_prev is None else [y_prev]
    return pl.pallas_call(
        functools.partial(_wkv_kernel, reverse=reverse, add_prev=bool(prev)),
        out_shape=jax.ShapeDtypeStruct((bsz, t, d), F32),
        grid=(bsz, nb),
        in_specs=[tok, tokd, tokd, tok, tok, tokd] + [tok] * len(prev),
        out_specs=tok,
        scratch_shapes=[pltpu.VMEM((ng, HEAD_SIZE, gw), F32),
                        pltpu.VMEM((nch, ng, 2 * L, gw), BF16),
                        pltpu.VMEM((nch, ng, HEAD_SIZE, gw), F32),
                        pltpu.VMEM((nch, ng, HEAD_SIZE, gw), F32)],
        compiler_params=_params("parallel", "arbitrary"),
        name="wkv_rev" if reverse else "wkv_fwd",
    )(r, lw, k, v, a, b, *prev)


def _rwkv_out_kernel(y_ref, r_ref, k_ref, v_ref, g_ref, rk_ref, lnw_ref, lnb_ref,
                     wo_ref, x_ref, gate_ref, o_ref):
    y = y_ref[...]
    ones = _seg_ones(min(LANE_GROUP, y.shape[1]))
    inv_n = 1.0 / HEAD_SIZE
    mu = _head_sum(y, ones) * inv_n
    yc = y - mu
    var = _head_sum(yc * yc, ones) * inv_n
    yn = yc * lax.rsqrt(var + LNX_EPS) * lnw_ref[...] + lnb_ref[...]
    ksum = k_ref[0].astype(F32) + k_ref[1].astype(F32)
    bonus = _head_sum(r_ref[...].astype(F32) * ksum * rk_ref[...], ones) * v_ref[...]
    out = ((yn + bonus) * g_ref[...].astype(F32)).astype(BF16)
    o_ref[...] = x_ref[...] + gate_ref[...] * _dot(out, wo_ref[...])


def _rwkv_output(y, r, kd, v, g, rk, lnw, lnb, wo_bf, x, gate, t_off):
    bsz, t, d = x.shape
    tm = _tile(TOKEN_TILE, t, t_off)
    off = t_off // tm
    tok = pl.BlockSpec((None, tm, d), lambda b, i: (b, i + off, 0))
    tok2 = pl.BlockSpec((2, None, tm, d), lambda b, i: (0, b, i + off, 0))
    row = pl.BlockSpec((1, d), lambda b, i: (0, 0))
    return pl.pallas_call(
        _rwkv_out_kernel,
        out_shape=jax.ShapeDtypeStruct((bsz, t, d), F32),
        grid=(bsz, t // tm),
        in_specs=[tok, tok, tok2, tok, tok, row, row, row,
                  pl.BlockSpec((d, d), lambda b, i: (0, 0)),
                  pl.BlockSpec((None, tm, d), lambda b, i: (b, i, 0)),
                  pl.BlockSpec((None, 1, d), lambda b, i: (b, 0, 0))],
        out_specs=pl.BlockSpec((None, tm, d), lambda b, i: (b, i, 0)),
        compiler_params=_params("parallel", "parallel"),
        name="rwkv_out",
    )(y, r, kd, v, g, rk.reshape(1, d), lnw.reshape(1, d), lnb.reshape(1, d), wo_bf, x, gate)


def _moe_route_kernel(x_ref, g_ref, sc_ref, sh_ref, rt_ref, xs_ref, gs_ref, code_ref,
                      h_scr, aff_scr, *, cap):
    t, d = x_ref.shape
    n_e = rt_ref.shape[0]

    @pl.when(pl.program_id(1) == 0)
    def _():
        g = g_ref[...]
        sc = sc_ref[...]
        sh = sh_ref[...]
        rt_hi, rt_lo = _split2(rt_ref[...])
        tc = _tile(TOKEN_TILE, t)
        for j in range(t // tc):
            h = _norm_mod(x_ref[j * tc:(j + 1) * tc, :], g, sc, sh)
            h_hi, h_lo = _split2(h)
            h_scr[j * tc:(j + 1) * tc, :] = h_hi
            aff_scr[:, j * tc:(j + 1) * tc] = (_dot_nt(rt_hi, h_hi) + _dot_nt(rt_hi, h_lo)
                                               + _dot_nt(rt_lo, h_hi))
        logits = aff_scr[...]
        m = jnp.max(logits, axis=0, keepdims=True)
        ex = jnp.exp(logits - m)
        aff = ex / jnp.sum(ex, axis=0, keepdims=True)
        aff_scr[...] = aff
        bits = pltpu.bitcast(aff, jnp.int32)

        def count_ge(cand):
            return jnp.sum(jnp.where(bits >= cand, 1, 0), axis=1, keepdims=True)

        def search(i, thr):
            lo = 28 - 2 * i
            c1, c2, c3 = [thr | (jnp.int32(m) << lo) for m in (1, 2, 3)]
            thr = jnp.where(count_ge(c1) >= cap, c1, thr)
            thr = jnp.where(count_ge(c2) >= cap, c2, thr)
            return jnp.where(count_ge(c3) >= cap, c3, thr)

        thr = jnp.zeros((n_e, 1), jnp.int32)
        thr = jnp.where(count_ge(thr | (jnp.int32(1) << 30)) >= cap, thr | (jnp.int32(1) << 30), thr)
        thr = lax.fori_loop(0, 15, search, thr)
        gt = bits > thr
        eq = bits == thr
        key = jnp.where(gt, 1, 0) + jnp.where(eq, 1 << TIE_BITS, 0)
        lane = lax.broadcasted_iota(jnp.int32, (n_e, t), 1)
        csum = key
        sh_amt = 1
        while sh_amt < t:
            csum = csum + jnp.where(lane >= sh_amt, pltpu.roll(csum, sh_amt, 1), 0)
            sh_amt *= 2
        before = csum - key
        n_gt = before & ((1 << TIE_BITS) - 1)
        n_eq = before >> TIE_BITS
        need = cap - jnp.sum(jnp.where(gt, 1, 0), axis=1, keepdims=True)
        sel = gt | (eq & (n_eq < need))
        code_ref[...] = jnp.where(sel, n_gt + jnp.minimum(n_eq, need), -1)

    eg = xs_ref.shape[0]
    e0 = pl.program_id(1) * eg
    slot = lax.broadcasted_iota(jnp.int32, (cap, t), 0)
    onehot = []
    for l in range(eg):
        hit = code_ref[pl.ds(e0 + l, 1), :] == slot
        onehot.append(jnp.where(hit, 1.0, 0.0).astype(BF16))
        gs_ref[l] = jnp.sum(jnp.where(hit, aff_scr[pl.ds(e0 + l, 1), :], 0.0), axis=1, keepdims=True)
    gathered = _dot(onehot[0] if eg == 1 else jnp.concatenate(onehot, axis=0), h_scr[...])
    xs_ref[...] = gathered.reshape(eg, cap, d).astype(BF16)


def _moe_route(x, norm_g, sc, sh, router):
    bsz, t, d = x.shape
    n_e = router.shape[1]
    cap = CAPACITY_FACTOR * t // n_e
    assert t < (1 << TIE_BITS)
    eg = _tile(max(1, MOE_GATHER_ROWS // cap), n_e)
    vec = lambda b, e: (b, 0, 0)
    return pl.pallas_call(
        functools.partial(_moe_route_kernel, cap=cap),
        out_shape=[jax.ShapeDtypeStruct((bsz, n_e, cap, d), BF16),
                   jax.ShapeDtypeStruct((bsz, n_e, cap, 1), F32),
                   jax.ShapeDtypeStruct((bsz, n_e, t), jnp.int32)],
        grid=(bsz, n_e // eg),
        in_specs=[pl.BlockSpec((None, t, d), lambda b, e: (b, 0, 0)),
                  pl.BlockSpec((1, d), lambda b, e: (0, 0)),
                  pl.BlockSpec((None, 1, d), vec),
                  pl.BlockSpec((None, 1, d), vec),
                  pl.BlockSpec((n_e, d), lambda b, e: (0, 0))],
        out_specs=[pl.BlockSpec((None, eg, cap, d), lambda b, e: (b, e, 0, 0)),
                   pl.BlockSpec((None, eg, cap, 1), lambda b, e: (b, e, 0, 0)),
                   pl.BlockSpec((None, n_e, t), lambda b, e: (b, 0, 0))],
        scratch_shapes=[pltpu.VMEM((t, d), BF16), pltpu.VMEM((n_e, t), F32)],
        compiler_params=_params("parallel", "arbitrary"),
        name="moe_route",
    )(x, norm_g.reshape(1, d), sc, sh, router.T)


def _moe_ffn_kernel(xs_ref, gs_ref, wg_ref, wu_ref, wd_ref, ys_ref, wg_scr, wu_scr, wd_scr):
    @pl.when(pl.program_id(1) == 0)
    def _():
        wg_scr[...] = wg_ref[...].astype(BF16)
        wu_scr[...] = wu_ref[...].astype(BF16)
        wd_scr[...] = wd_ref[...].astype(BF16)

    tb, cap, d = xs_ref.shape
    xs = xs_ref[...].reshape(tb * cap, d)
    hg = _dot(xs, wg_scr[...])
    hu = _dot(xs, wu_scr[...])
    hid = (hg * _sigmoid(hg) * hu).astype(BF16)
    ys = _dot(hid, wd_scr[...]) * gs_ref[...].reshape(tb * cap, 1)
    ys_ref[...] = ys.reshape(tb, cap, d).astype(BF16)


def _moe_ffn(xs, gs, wg, wu, wd, layer):
    bsz, n_e, cap, d = xs.shape
    f = wg.shape[-1]
    tb = max(1, min(bsz, MOE_FFN_ROWS // cap))
    while bsz % tb:
        tb -= 1
    return pl.pallas_call(
        _moe_ffn_kernel,
        out_shape=jax.ShapeDtypeStruct((bsz, n_e, cap, d), BF16),
        grid=(n_e, bsz // tb),
        in_specs=[pl.BlockSpec((tb, None, cap, d), lambda e, j: (j, e, 0, 0)),
                  pl.BlockSpec((tb, None, cap, 1), lambda e, j: (j, e, 0, 0)),
                  pl.BlockSpec((None, None, d, f), lambda e, j: (layer, e, 0, 0)),
                  pl.BlockSpec((None, None, d, f), lambda e, j: (layer, e, 0, 0)),
                  pl.BlockSpec((None, None, f, d), lambda e, j: (layer, e, 0, 0))],
        out_specs=pl.BlockSpec((tb, None, cap, d), lambda e, j: (j, e, 0, 0)),
        scratch_shapes=[pltpu.VMEM((d, f), BF16), pltpu.VMEM((d, f), BF16), pltpu.VMEM((f, d), BF16)],
        compiler_params=_params("parallel", "arbitrary"),
        name="moe_ffn",
    )(xs, gs, wg, wu, wd)


def _moe_combine_kernel(code_ref, ys_ref, x_ref, gate_ref, fg_ref, o_ref, *, cap, final_norm):
    n_e = code_ref.shape[1]
    slot = lax.broadcasted_iota(jnp.int32, (1, cap), 1)
    pieces = [jnp.where(code_ref[:, e:e + 1] == slot, 1.0, 0.0).astype(BF16) for e in range(n_e)]
    scat = jnp.concatenate(pieces, axis=1)
    x = x_ref[...] + gate_ref[...] * _dot(scat, ys_ref[...])
    if final_norm:
        x = x * lax.rsqrt(jnp.mean(x * x, axis=-1, keepdims=True) + NORM_EPS) * fg_ref[...]
    o_ref[...] = x


def _moe_combine(code_t, ys, x, gate, final_g, final_norm):
    bsz, t, d = x.shape
    n_e = code_t.shape[-1]
    cap = ys.shape[1] // n_e
    tm = _tile(MOE_COMBINE_ROWS, t)
    return pl.pallas_call(
        functools.partial(_moe_combine_kernel, cap=cap, final_norm=final_norm),
        out_shape=jax.ShapeDtypeStruct((bsz, t, d), F32),
        grid=(bsz, t // tm),
        in_specs=[pl.BlockSpec((None, tm, n_e), lambda b, i: (b, i, 0)),
                  pl.BlockSpec((None, n_e * cap, d), lambda b, i: (b, 0, 0)),
                  pl.BlockSpec((None, tm, d), lambda b, i: (b, i, 0)),
                  pl.BlockSpec((None, 1, d), lambda b, i: (b, 0, 0)),
                  pl.BlockSpec((1, d), lambda b, i: (0, 0))],
        out_specs=pl.BlockSpec((None, tm, d), lambda b, i: (b, i, 0)),
        compiler_params=_params("parallel", "parallel"),
        name="moe_combine",
    )(code_t, ys, x, gate, final_g.reshape(1, d))


def _ec_moe_layer(x, norm_g, sc, sh, gate, router, wg, wu, wd, layer, final_g, final_norm):
    bsz, t, d = x.shape
    xs, gs, code = _moe_route(x, norm_g, sc, sh, router)
    ys = _moe_ffn(xs, gs, wg, wu, wd, layer)
    n_e, cap = xs.shape[1], xs.shape[2]
    return _moe_combine(jnp.swapaxes(code, 1, 2), ys.reshape(bsz, n_e * cap, d), x, gate,
                        final_g, final_norm)


def kernel(x, c, ctx, c_ctx, ada_w, ada_b, norm_g, fnet_wo, fnet_bo, rw_mix, rw_wr, rw_wk, rw_wv, rw_wo,
           rw_w0, rw_w1, rw_w2, rw_a0, rw_a1, rw_a2, rw_v0, rw_v1, rw_v2, rw_g1, rw_g2, rw_kk, rw_ka, rw_rk,
           rw_lnx_w, rw_lnx_b, moe_router, moe_wg, moe_wu, moe_wd, final_g):
    bsz, t, d = x.shape
    t_ctx = ctx.shape[1]
    depth = ada_w.shape[0]
    n_mixers = 2

    rows = -(-(bsz + 1) // 8) * 8
    cc = jnp.concatenate([c, c_ctx[None, :], jnp.zeros((rows - bsz - 1, d), F32)], axis=0)
    mods = _adaln(cc, ada_w, ada_b)

    def mod_x(i, j):
        return mods[i, :bsz, j * d:(j + 1) * d].reshape(bsz, 1, d)

    def mod_c(i, j):
        return jnp.broadcast_to(mods[i, bsz, j * d:(j + 1) * d].reshape(1, 1, d), (bsz, 1, d))

    mats_x = _dft_mats(t, d // FNET_GROUPS)
    mats_c = _dft_mats(t_ctx, d // FNET_GROUPS)
    vfirst = None
    for i in range(depth):
        need_ctx = i < depth - 1
        if i % n_mixers == 0:
            fi = i // n_mixers
            wo_bf = fnet_wo[fi].astype(BF16)
            x = _fnet_layer(x, norm_g[i, 0], mod_x(i, 1), mod_x(i, 0), mod_x(i, 2), wo_bf, fnet_bo[fi], mats_x)
            if need_ctx:
                ctx = _fnet_layer(ctx, norm_g[i, 0], mod_c(i, 1), mod_c(i, 0), mod_c(i, 2), wo_bf,
                                  fnet_bo[fi], mats_c)
        else:
            ri = i // n_mixers
            p = dict(mix=rw_mix[ri], wr=rw_wr[ri], wk=rw_wk[ri], wv=rw_wv[ri], w0=rw_w0[ri], w1=rw_w1[ri],
                     w2=rw_w2[ri], a0=rw_a0[ri], a1=rw_a1[ri], a2=rw_a2[ri], g1=rw_g1[ri], g2=rw_g2[ri],
                     kk=rw_kk[ri], ka=rw_ka[ri])
            if ri > 0:
                p.update(v0=rw_v0[ri - 1], v1=rw_v1[ri - 1], v2=rw_v2[ri - 1])
            vf = vfirst if ri > 0 else None
            r, v, g, a, lw, kd, bd = _rwkv_project(ctx, x, norm_g[i, 0], (mod_c(i, 1), mod_c(i, 0)),
                                                   (mod_x(i, 1), mod_x(i, 0)), vf, p)
            if ri == 0:
                vfirst = v
            y = _wkv_scan(r, lw, kd, v, a, bd, 0, t_ctx, False)
            y = _wkv_scan(r, lw, kd, v, a, bd, 1, t_ctx, True, y_prev=y)
            wo_bf = rw_wo[ri].astype(BF16)
            args = (y, r, kd, v, g, rw_rk[ri], rw_lnx_w[ri], rw_lnx_b[ri], wo_bf)
            x = _rwkv_output(*args, x, mod_x(i, 2), t_ctx)
            if need_ctx:
                ctx = _rwkv_output(*args, ctx, mod_c(i, 2), 0)
        last = i == depth - 1
        x = _ec_moe_layer(x, norm_g[i, 1], mod_x(i, 4), mod_x(i, 3), mod_x(i, 5), moe_router[i],
                          moe_wg, moe_wu, moe_wd, i, final_g, last)
        if need_ctx:
            ctx = _ec_moe_layer(ctx, norm_g[i, 1], mod_c(i, 4), mod_c(i, 3), mod_c(i, 5), moe_router[i],
                                moe_wg, moe_wu, moe_wd, i, final_g, False)
    return x
```

```python
import functools

import jax
import jax.numpy as jnp
from jax import lax
from jax.experimental import pallas as pl
from jax.experimental.pallas import tpu as pltpu

F32 = jnp.float32
BF16 = jnp.bfloat16

HEAD_SIZE = 64
LANES = 128
LANE_GROUP = 256
GRID_W = 64
FNET_GROUPS = 4
CAPACITY_FACTOR = 2
NORM_EPS = 1e-6
LNX_EPS = 64e-5
DECAY_SCALE = 0.6065306597126334
WKV_CHUNK = 64
WKV_BLOCK = 256
WKV_WAVE = 4
MOE_GATHER_ROWS = 2048
MOE_FFN_ROWS = 1024
MOE_COMBINE_ROWS = 1024
TOKEN_TILE = 256
CHAN_TILE = 512
ADALN_COLS = 1536
TIE_BITS = 12
VMEM_LIMIT_BYTES = 56 * 1024 * 1024


def _params(*semantics):
    return pltpu.CompilerParams(dimension_semantics=semantics, vmem_limit_bytes=VMEM_LIMIT_BYTES)


def _tile(pref, *extents):
    tm = pref
    while any(n % tm for n in extents if n):
        tm //= 2
    return tm


def _dot(a, b):
    return jnp.dot(a, b, preferred_element_type=F32)


def _dot_nt(a, b):
    return lax.dot_general(a, b, (((1,), (1,)), ((), ())), preferred_element_type=F32)


def _split2(x):
    hi = x.astype(BF16)
    lo = (x - hi.astype(F32)).astype(BF16)
    return hi, lo


def _split3(x):
    hi = x.astype(BF16)
    r1 = x - hi.astype(F32)
    mid = r1.astype(BF16)
    lo = (r1 - mid.astype(F32)).astype(BF16)
    return hi, mid, lo


def _norm_mod(x, g, sc, sh):
    ms = jnp.mean(x * x, axis=-1, keepdims=True)
    return x * lax.rsqrt(ms + NORM_EPS) * g * (1.0 + sc) + sh


def _sigmoid(x):
    return 0.5 * jnp.tanh(0.5 * x) + 0.5


def _seg_ones(n):
    r = lax.broadcasted_iota(jnp.int32, (n, n), 0) // HEAD_SIZE
    c = lax.broadcasted_iota(jnp.int32, (n, n), 1) // HEAD_SIZE
    return jnp.where(r == c, 1.0, 0.0).astype(BF16)


def _head_sum(x, ones):
    d = x.shape[-1]
    w = ones.shape[0]
    parts = []
    for j in range(d // w):
        hi, lo = _split2(x[:, j * w:(j + 1) * w])
        parts.append(_dot(hi, ones) + _dot(lo, ones))
    return parts[0] if len(parts) == 1 else jnp.concatenate(parts, axis=1)


def _adaln_kernel(c_ref, w_ref, b_ref, o_ref):
    c = c_ref[...]
    s = c * _sigmoid(c)
    s_hi, s_lo = _split2(s)
    w_hi, w_lo = _split2(w_ref[...])
    o_ref[...] = _dot(s_hi, w_hi) + _dot(s_lo, w_hi) + _dot(s_hi, w_lo) + b_ref[...]


def _adaln(cc, ada_w, ada_b):
    depth, d, n = ada_w.shape
    rows = cc.shape[0]
    tn = _tile(ADALN_COLS, n)
    return pl.pallas_call(
        _adaln_kernel,
        out_shape=jax.ShapeDtypeStruct((depth, rows, n), F32),
        grid=(depth, n // tn),
        in_specs=[pl.BlockSpec((rows, d), lambda l, j: (0, 0)),
                  pl.BlockSpec((None, d, tn), lambda l, j: (l, 0, j)),
                  pl.BlockSpec((None, 1, tn), lambda l, j: (l, 0, j))],
        out_specs=pl.BlockSpec((None, rows, tn), lambda l, j: (l, 0, j)),
        compiler_params=_params("parallel", "parallel"),
        name="adaln",
    )(cc, ada_w, ada_b.reshape(depth, 1, n))


def _fnet_chan_kernel(x_ref, g_ref, sc_ref, sh_ref, cs_ref, o_ref):
    h = _norm_mod(x_ref[...], g_ref[...], sc_ref[...], sh_ref[...])
    gd = cs_ref.shape[0]
    cs = cs_ref[...]
    for j in range(h.shape[1] // gd):
        z = _dot(h[:, j * gd:(j + 1) * gd].astype(BF16), cs)
        o_ref[0, :, j * gd:(j + 1) * gd] = z[:, :gd].astype(BF16)
        o_ref[1, :, j * gd:(j + 1) * gd] = z[:, gd:].astype(BF16)


def _fnet_time_kernel(f_ref, hcs_ref, wo_ref, bo_ref, x_ref, gate_ref, o_ref):
    f = _dot(f_ref[...], hcs_ref[...])
    y = _dot(f.astype(BF16), wo_ref[...]) + bo_ref[...]
    o_ref[...] = x_ref[...] + gate_ref[...] * y


def _dft_mats(t, gd):
    def cs(n):
        i = jnp.arange(n, dtype=jnp.int32)
        ang = ((i[:, None] * i[None, :]) % n).astype(F32) * (2.0 * jnp.pi / n)
        return jnp.cos(ang), jnp.sin(ang)
    ct, st = cs(t)
    cc, sc = cs(gd)
    scale = 1.0 / jnp.sqrt(jnp.asarray(t * gd, F32))
    return (jnp.concatenate([ct, -st], axis=1).astype(BF16),
            (jnp.concatenate([cc, sc], axis=1) * scale).astype(BF16))


def _fnet_layer(x, norm_g, sc, sh, gate, wo_bf, bo, mats):
    bsz, t, d = x.shape
    f_mat, cs_mat = mats
    gd = d // FNET_GROUPS
    tm = _tile(CHAN_TILE, t)
    vec = lambda b, i: (b, 0, 0)
    hcs = pl.pallas_call(
        _fnet_chan_kernel,
        out_shape=jax.ShapeDtypeStruct((bsz, 2, t, d), BF16),
        grid=(bsz, t // tm),
        in_specs=[pl.BlockSpec((None, tm, d), lambda b, i: (b, i, 0)),
                  pl.BlockSpec((1, d), lambda b, i: (0, 0)),
                  pl.BlockSpec((None, 1, d), vec),
                  pl.BlockSpec((None, 1, d), vec),
                  pl.BlockSpec((gd, 2 * gd), lambda b, i: (0, 0))],
        out_specs=pl.BlockSpec((None, 2, tm, d), lambda b, i: (b, 0, i, 0)),
        compiler_params=_params("parallel", "parallel"),
        name="fnet_chan",
    )(x, norm_g.reshape(1, d), sc, sh, cs_mat)
    hcs = hcs.reshape(bsz, 2 * t, d)
    tm2 = _tile(CHAN_TILE, t)
    return pl.pallas_call(
        _fnet_time_kernel,
        out_shape=jax.ShapeDtypeStruct((bsz, t, d), F32),
        grid=(bsz, t // tm2),
        in_specs=[pl.BlockSpec((tm2, 2 * t), lambda b, i: (i, 0)),
                  pl.BlockSpec((None, 2 * t, d), lambda b, i: (b, 0, 0)),
                  pl.BlockSpec((d, d), lambda b, i: (0, 0)),
                  pl.BlockSpec((1, d), lambda b, i: (0, 0)),
                  pl.BlockSpec((None, tm2, d), lambda b, i: (b, i, 0)),
                  pl.BlockSpec((None, 1, d), vec)],
        out_specs=pl.BlockSpec((None, tm2, d), lambda b, i: (b, i, 0)),
        compiler_params=_params("parallel", "parallel"),
        name="fnet_time",
    )(f_mat, hcs, wo_bf, bo.reshape(1, d), x, gate)


def _shifted_grid(h, h_above, h_below):
    tm, d = h.shape
    q = d // 4
    colw = lax.broadcasted_iota(jnp.int32, (tm, q), 0) % GRID_W
    left = jnp.where(colw != 0, pltpu.roll(h[:, :q], 1, 0), 0.0)
    right = jnp.where(colw != GRID_W - 1, pltpu.roll(h[:, q:2 * q], tm - 1, 0), 0.0)
    up = jnp.concatenate([h_above[:, 2 * q:3 * q], h[:tm - GRID_W, 2 * q:3 * q]], axis=0)
    down = jnp.concatenate([h[GRID_W:, 3 * q:], h_below[:, 3 * q:]], axis=0)
    return jnp.concatenate([left, right, up, down], axis=1)


def _shifted_seq(h):
    tm, d = h.shape
    half = d // 2
    row = lax.broadcasted_iota(jnp.int32, (tm, half), 0)
    prev = jnp.where(row != 0, pltpu.roll(h[:, :half], 1, 0), 0.0)
    nxt = jnp.where(row != tm - 1, pltpu.roll(h[:, half:], tm - 1, 0), 0.0)
    return jnp.concatenate([prev, nxt], axis=1)


def _rwkv_proj_kernel(*refs, has_vres, n_ctx_tiles):
    it = iter(refs)
    c_ref, x_ref, xa_ref, xb_ref = next(it), next(it), next(it), next(it)
    ng_ref, scc_ref, shc_ref, scx_ref, shx_ref = next(it), next(it), next(it), next(it), next(it)
    vf_ref = next(it) if has_vres else None
    mix_ref, wr_ref, wk_ref, wv_ref = next(it), next(it), next(it), next(it)
    w0_ref, w1_ref, w2_ref = next(it), next(it), next(it)
    a0_ref, a1_ref, a2_ref = next(it), next(it), next(it)
    g1_ref, g2_ref, kk_ref, ka_ref = next(it), next(it), next(it), next(it)
    if has_vres:
        v0_ref, v1_ref, v2_ref = next(it), next(it), next(it)
    r_out, v_out, g_out, a_out = next(it), next(it), next(it), next(it)
    lw_out, kd_out, bd_out = next(it), next(it), next(it)

    i = pl.program_id(1)
    is_ctx = i < n_ctx_tiles
    ng = ng_ref[...]
    scx, shx = scx_ref[...], shx_ref[...]
    h = _norm_mod(jnp.where(is_ctx, c_ref[...], x_ref[...]), ng,
                  jnp.where(is_ctx, scc_ref[...], scx), jnp.where(is_ctx, shc_ref[...], shx))
    h_above = jnp.where(i > n_ctx_tiles, _norm_mod(xa_ref[...], ng, scx, shx), 0.0)
    h_below = jnp.where(i < pl.num_programs(1) - 1, _norm_mod(xb_ref[...], ng, scx, shx), 0.0)
    hs = jnp.where(is_ctx, _shifted_seq(h), _shifted_grid(h, h_above, h_below))
    xx = hs - h
    xr, xw, xk, xv, xa, xg = [(h + xx * mix_ref[j:j + 1, :]).astype(BF16) for j in range(6)]
    r = _dot(xr, wr_ref[...])
    k = _dot(xk, wk_ref[...])
    v = _dot(xv, wv_ref[...])
    if has_vres:
        lora = _dot(_dot(xv, v1_ref[...]).astype(BF16), v2_ref[...])
        v = v + (vf_ref[...] - v) * _sigmoid(v0_ref[...] + lora)
    g = _dot(_sigmoid(_dot(xg, g1_ref[...])).astype(BF16), g2_ref[...])
    ones = _seg_ones(min(LANE_GROUP, h.shape[1]))
    kk = k * kk_ref[...]
    kk = kk * lax.rsqrt(_head_sum(kk * kk, ones) + 1e-12)
    r_out[...] = r.astype(BF16)
    v_out[...] = v
    g_out[...] = g.astype(BF16)
    a_out[...] = (-kk).astype(BF16)
    k_in = k * ka_ref[...]
    k_out = k - k_in
    for d in range(2):
        wlog = w0_ref[d:d + 1, :] + _dot(jnp.tanh(_dot(xw, w1_ref[d])).astype(BF16), w2_ref[d])
        lw_out[d] = (-0.5 * DECAY_SCALE) * (jnp.tanh(0.5 * wlog) + 1.0)
        a = _sigmoid(a0_ref[d:d + 1, :] + _dot(_dot(xa, a1_ref[d]).astype(BF16), a2_ref[d]))
        kd_out[d] = (k_out + k_in * a).astype(BF16)
        bd_out[d] = (kk * a).astype(BF16)


def _rwkv_project(ctx, x, norm_g, mod_c, mod_x, vfirst, p):
    bsz, t, d = x.shape
    t_ctx = ctx.shape[1]
    tm = _tile(TOKEN_TILE, t, t_ctx)
    nct = t_ctx // tm
    assert nct == 1, "the sequence shift handles a context that fits one tile"
    ntx = t // tm
    has_vres = vfirst is not None
    tok = pl.BlockSpec((None, tm, d), lambda b, i: (b, i, 0))
    tok2 = pl.BlockSpec((2, None, tm, d), lambda b, i: (0, b, i, 0))
    vec = pl.BlockSpec((None, 1, d), lambda b, i: (b, 0, 0))

    def full(a):
        nd = a.ndim
        return pl.BlockSpec(a.shape, lambda b, i: (0,) * nd)

    bf = lambda a: a.astype(BF16)
    weights = [p["mix"], bf(p["wr"]), bf(p["wk"]), bf(p["wv"]),
               p["w0"], bf(p["w1"]), bf(p["w2"]), p["a0"], bf(p["a1"]), bf(p["a2"]),
               bf(p["g1"]), bf(p["g2"]), p["kk"].reshape(1, d), p["ka"].reshape(1, d)]
    if has_vres:
        weights += [p["v0"].reshape(1, d), bf(p["v1"]), bf(p["v2"])]
    rpt = tm // GRID_W
    last = t // GRID_W - 1
    acts = [ctx, x, x, x, norm_g.reshape(1, d), *mod_c, *mod_x]
    act_specs = [pl.BlockSpec((None, tm, d), lambda b, i: (b, jnp.minimum(i, nct - 1), 0)),
                 pl.BlockSpec((None, tm, d), lambda b, i: (b, jnp.maximum(i - nct, 0), 0)),
                 pl.BlockSpec((None, GRID_W, d), lambda b, i: (b, jnp.maximum((i - nct) * rpt - 1, 0), 0)),
                 pl.BlockSpec((None, GRID_W, d), lambda b, i: (b, jnp.clip((i - nct + 1) * rpt, 0, last), 0)),
                 pl.BlockSpec((1, d), lambda b, i: (0, 0)), vec, vec, vec, vec]
    if has_vres:
        acts.append(vfirst)
        act_specs.append(tok)
    tt = t_ctx + t
    out_shape = [jax.ShapeDtypeStruct((bsz, tt, d), BF16),
                 jax.ShapeDtypeStruct((bsz, tt, d), F32),
                 jax.ShapeDtypeStruct((bsz, tt, d), BF16),
                 jax.ShapeDtypeStruct((bsz, tt, d), BF16),
                 jax.ShapeDtypeStruct((2, bsz, tt, d), F32),
                 jax.ShapeDtypeStruct((2, bsz, tt, d), BF16),
                 jax.ShapeDtypeStruct((2, bsz, tt, d), BF16)]
    return pl.pallas_call(
        functools.partial(_rwkv_proj_kernel, has_vres=has_vres, n_ctx_tiles=nct),
        out_shape=out_shape,
        grid=(bsz, nct + ntx),
        in_specs=act_specs + [full(w) for w in weights],
        out_specs=[tok, tok, tok, tok, tok2, tok2, tok2],
        compiler_params=_params("parallel", "parallel"),
        name="rwkv_proj",
    )(*acts, *weights)


def _wkv_kernel(*refs, reverse, add_prev):
    r_ref, lw_ref, k_ref, v_ref, a_ref, b_ref = refs[:6]
    yp_ref = refs[6] if add_prev else None
    y_ref, s_scr, lhs_scr, n_scr, wl_scr = refs[-5:]
    _wkv_body(r_ref, lw_ref, k_ref, v_ref, a_ref, b_ref, yp_ref, y_ref, s_scr, lhs_scr, n_scr, wl_scr, reverse)


def _wkv_body(r_ref, lw_ref, k_ref, v_ref, a_ref, b_ref, yp_ref, y_ref, s_scr, lhs_scr, n_scr, wl_scr, reverse):
    L = WKV_CHUNK
    tb, d = r_ref.shape
    nch = tb // L
    gw = s_scr.shape[-1]
    hpg = gw // HEAD_SIZE
    ng = d // gw

    @pl.when(pl.program_id(1) == 0)
    def _():
        s_scr[...] = jnp.zeros_like(s_scr)

    ti = lax.broadcasted_iota(jnp.int32, (L, L), 0)
    si = lax.broadcasted_iota(jnp.int32, (L, L), 1)
    tri = jnp.where((si >= ti) if reverse else (si <= ti), 1.0, 0.0).astype(BF16)
    last, mid = (0, L // 2) if reverse else (L - 1, L // 2 - 1)
    trow = lax.broadcasted_iota(jnp.int32, (L, gw), 0)
    scol = lax.broadcasted_iota(jnp.int32, (L, gw), 1) % HEAD_SIZE
    strict = (scol > trow) if reverse else (scol < trow)
    incl = (scol >= trow) if reverse else (scol <= trow)
    eye = jnp.where(scol == trow, 1.0, 0.0)
    bdm = (lax.broadcasted_iota(jnp.int32, (gw, gw), 0) // HEAD_SIZE
           == lax.broadcasted_iota(jnp.int32, (gw, gw), 1) // HEAD_SIZE)

    def bdiag(z):
        return jnp.where(bdm, jnp.concatenate([z] * hpg, axis=0), 0.0).astype(BF16)

    def per_head(f, lo):
        return jnp.concatenate([f[h * HEAD_SIZE:(h + 1) * HEAD_SIZE, lo:lo + L] for h in range(hpg)], axis=1)

    def compact(f):
        fm = jnp.where(bdm, f, 0.0)
        out = fm[:HEAD_SIZE]
        for h in range(1, hpg):
            out = out + fm[h * HEAD_SIZE:(h + 1) * HEAD_SIZE]
        return out

    def prep(j):
        rows = slice(j * L, (j + 1) * L)
        lw = lw_ref[rows, :]
        p1, p2, p3 = _split3(lw)
        cw = _dot(tri, p1) + _dot(tri, p2) + _dot(tri, p3)
        cw_end = cw[last:last + 1, :]
        cw_mid = cw[mid:mid + 1, :]
        r = r_ref[rows, :].astype(F32)
        k = k_ref[rows, :].astype(F32)
        a = a_ref[rows, :].astype(F32)
        b = b_ref[rows, :].astype(F32)
        e_in = jnp.exp(cw)
        e_ex = jnp.exp(cw - lw)
        e_inv = jnp.exp(cw_mid - cw)
        e_mid = jnp.exp(-cw_mid)
        e_end = e_inv * jnp.exp(cw_end - cw_mid)
        r0 = r * e_in
        a0 = a * e_ex
        return dict(rows=rows, cw_end=cw_end, r0=r0, a0=a0, ra=r0 * e_mid, aa=a0 * e_mid, kb=k * e_inv, bb=b * e_inv,
                    ke=k * e_end, be=b * e_end)

    for w0 in range(0, nch, WKV_WAVE):
        pj = {j: prep(j) for j in range(w0, min(nch, w0 + WKV_WAVE))}
        ch = [(j, slice(g * gw, (g + 1) * gw)) for j in pj for g in range(ng)]
        cs = range(len(ch))
        col = lambda name, c: pj[ch[c][0]][name][:, ch[c][1]]
        xq = [jnp.concatenate([col("aa", c), col("ra", c)], axis=0).astype(BF16) for c in cs]
        gb = [_dot_nt(xq[c], bdiag(col("bb", c))) for c in cs]
        gk = [_dot_nt(xq[c], bdiag(col("kb", c))) for c in cs]
        a_ab = [jnp.where(strict, gb[c][:L], 0.0) for c in cs]
        vg = [v_ref[pj[ch[c][0]]["rows"], ch[c][1]] for c in cs]
        ft = [jnp.concatenate([col("be", c), col("ke", c)], axis=0).T for c in cs]
        bet = [per_head(ft[c], 0) for c in cs]
        ket = [per_head(ft[c], L) for c in cs]
        akrk = [jnp.concatenate([jnp.where(strict, gk[c][:L], 0.0), jnp.where(incl, gk[c][L:], 0.0), ket[c]],
                                axis=0).astype(BF16) for c in cs]
        tmat = [eye + a_ab[c] for c in cs]
        pw = [_dot(a_ab[c].astype(BF16), bdiag(a_ab[c])) for c in cs]
        avk = [_dot(akrk[c], bdiag(vg[c])) for c in cs]
        n_lvl = L.bit_length() - 2
        for lvl in range(n_lvl):
            if lvl + 1 < n_lvl:
                z = [_dot(jnp.concatenate([tmat[c], pw[c]], axis=0).astype(BF16), bdiag(pw[c])) for c in cs]
                tmat = [tmat[c] + z[c][:L] for c in cs]
                pw = [z[c][L:] for c in cs]
            else:
                z = [_dot(tmat[c].astype(BF16), bdiag(pw[c])) for c in cs]
                tmat = [tmat[c] + z[c] for c in cs]
        rbe = [jnp.concatenate([jnp.where(incl, gb[c][L:], 0.0), bet[c]], axis=0).astype(BF16) for c in cs]
        rt = [_dot(rbe[c], bdiag(tmat[c])).astype(BF16) for c in cs]
        za = [_dot(rt[c], bdiag(col("a0", c))) for c in cs]
        zv = [_dot(rt[c], bdiag(avk[c][:L])) for c in cs]
        ridx = lax.broadcasted_iota(jnp.int32, (LANES, d), 0)
        ends = jnp.zeros((LANES, d), F32)
        for n, j in enumerate(pj):
            ends = jnp.where(ridx == n, pj[j]["cw_end"], ends)
        w_end = jnp.exp(ends.T)
        for c in cs:
            j, sl = ch[c]
            g = c % ng
            y_part = zv[c][:L] + avk[c][L:2 * L]
            if yp_ref is not None:
                y_part = y_part + yp_ref[pj[j]["rows"], sl]
            y_ref[pj[j]["rows"], sl] = y_part
            lhs_scr[j, g, :L, :] = (col("r0", c) + za[c][:L]).astype(BF16)
            lhs_scr[j, g, L:, :] = za[c][L:].astype(BF16)
            n_scr[j, g] = zv[c][L:] + avk[c][2 * L:]
            wcol = w_end[sl, c // ng:c // ng + 1]
            wl_scr[j, g] = compact(jnp.broadcast_to(wcol, (gw, gw)))

    for j in (reversed(range(nch)) if reverse else range(nch)):
        rows = slice(j * L, (j + 1) * L)
        s_old = [s_scr[g] for g in range(ng)]
        z = [_dot(lhs_scr[j, g], bdiag(s_old[g])) for g in range(ng)]
        for g in range(ng):
            y_ref[rows, g * gw:(g + 1) * gw] += z[g][:L]
            s_scr[g] = wl_scr[j, g] * s_old[g] + z[g][L:] + n_scr[j, g]


def _wkv_scan(r, lw, k, v, a, b, d_idx, t_ctx, reverse, y_prev=None):
    bsz, t, d = r.shape
    L = WKV_CHUNK
    tb = _tile(WKV_BLOCK, t_ctx, t - t_ctx)
    nb = t // tb
    n_ctx_blocks = t_ctx // tb
    nch = tb // L
    gw = min(LANE_GROUP, d)
    ng = d // gw
    if reverse:
        cidx = lambda c: jnp.where(c < n_ctx_blocks, n_ctx_blocks - 1 - c, nb + n_ctx_blocks - 1 - c)
    else:
        cidx = lambda c: c
    tok = pl.BlockSpec((None, tb, d), lambda bb, c: (bb, cidx(c), 0))
    tokd = pl.BlockSpec((None, None, tb, d), lambda bb, c: (d_idx, bb, cidx(c), 0))
    prev = [] if y_prev is None else [y_prev]
    return pl.pallas_call(
        functools.partial(_wkv_kernel, reverse=reverse, add_prev=bool(prev)),
        out_shape=jax.ShapeDtypeStruct((bsz, t, d), F32),
        grid=(bsz, nb),
        in_specs=[tok, tokd, tokd, tok, tok, tokd] + [tok] * len(prev),
        out_specs=tok,
        scratch_shapes=[pltpu.VMEM((ng, HEAD_SIZE, gw), F32),
                        pltpu.VMEM((nch, ng, 2 * L, gw), BF16),
                        pltpu.VMEM((nch, ng, HEAD_SIZE, gw), F32),
                        pltpu.VMEM((nch, ng, HEAD_SIZE, gw), F32)],
        compiler_params=_params("parallel", "arbitrary"),
        name="wkv_rev" if reverse else "wkv_fwd",
    )(r, lw, k, v, a, b, *prev)


def _rwkv_out_kernel(y_ref, r_ref, k_ref, v_ref, g_ref, rk_ref, lnw_ref, lnb_ref,
                     wo_ref, x_ref, gate_ref, o_ref):
    y = y_ref[...]
    ones = _seg_ones(min(LANE_GROUP, y.shape[1]))
    inv_n = 1.0 / HEAD_SIZE
    mu = _head_sum(y, ones) * inv_n
    yc = y - mu
    var = _head_sum(yc * yc, ones) * inv_n
    yn = yc * lax.rsqrt(var + LNX_EPS) * lnw_ref[...] + lnb_ref[...]
    ksum = k_ref[0].astype(F32) + k_ref[1].astype(F32)
    bonus = _head_sum(r_ref[...].astype(F32) * ksum * rk_ref[...], ones) * v_ref[...]
    out = ((yn + bonus) * g_ref[...].astype(F32)).astype(BF16)
    o_ref[...] = x_ref[...] + gate_ref[...] * _dot(out, wo_ref[...])


def _rwkv_output(y, r, kd, v, g, rk, lnw, lnb, wo_bf, x, gate, t_off):
    bsz, t, d = x.shape
    tm = _tile(TOKEN_TILE, t, t_off)
    off = t_off // tm
    tok = pl.BlockSpec((None, tm, d), lambda b, i: (b, i + off, 0))
    tok2 = pl.BlockSpec((2, None, tm, d), lambda b, i: (0, b, i + off, 0))
    row = pl.BlockSpec((1, d), lambda b, i: (0, 0))
    return pl.pallas_call(
        _rwkv_out_kernel,
        out_shape=jax.ShapeDtypeStruct((bsz, t, d), F32),
        grid=(bsz, t // tm),
        in_specs=[tok, tok, tok2, tok, tok, row, row, row,
                  pl.BlockSpec((d, d), lambda b, i: (0, 0)),
                  pl.BlockSpec((None, tm, d), lambda b, i: (b, i, 0)),
                  pl.BlockSpec((None, 1, d), lambda b, i: (b, 0, 0))],
        out_specs=pl.BlockSpec((None, tm, d), lambda b, i: (b, i, 0)),
        compiler_params=_params("parallel", "parallel"),
        name="rwkv_out",
    )(y, r, kd, v, g, rk.reshape(1, d), lnw.reshape(1, d), lnb.reshape(1, d), wo_bf, x, gate)


def _moe_route_kernel(x_ref, g_ref, sc_ref, sh_ref, rt_ref, xs_ref, gs_ref, code_ref,
                      h_scr, aff_scr, *, cap):
    t, d = x_ref.shape
    n_e = rt_ref.shape[0]

    @pl.when(pl.program_id(1) == 0)
    def _():
        g = g_ref[...]
        sc = sc_ref[...]
        sh = sh_ref[...]
        rt_hi, rt_lo = _split2(rt_ref[...])
        tc = _tile(TOKEN_TILE, t)
        for j in range(t // tc):
            h = _norm_mod(x_ref[j * tc:(j + 1) * tc, :], g, sc, sh)
            h_hi, h_lo = _split2(h)
            h_scr[j * tc:(j + 1) * tc, :] = h_hi
            aff_scr[:, j * tc:(j + 1) * tc] = (_dot_nt(rt_hi, h_hi) + _dot_nt(rt_hi, h_lo)
                                               + _dot_nt(rt_lo, h_hi))
        logits = aff_scr[...]
        m = jnp.max(logits, axis=0, keepdims=True)
        ex = jnp.exp(logits - m)
        aff = ex / jnp.sum(ex, axis=0, keepdims=True)
        aff_scr[...] = aff
        bits = pltpu.bitcast(aff, jnp.int32)

        def count_ge(cand):
            return jnp.sum(jnp.where(bits >= cand, 1, 0), axis=1, keepdims=True)

        def search(i, thr):
            lo = 28 - 2 * i
            c1, c2, c3 = [thr | (jnp.int32(m) << lo) for m in (1, 2, 3)]
            thr = jnp.where(count_ge(c1) >= cap, c1, thr)
            thr = jnp.where(count_ge(c2) >= cap, c2, thr)
            return jnp.where(count_ge(c3) >= cap, c3, thr)

        thr = jnp.zeros((n_e, 1), jnp.int32)
        thr = jnp.where(count_ge(thr | (jnp.int32(1) << 30)) >= cap, thr | (jnp.int32(1) << 30), thr)
        thr = lax.fori_loop(0, 15, search, thr)
        gt = bits > thr
        eq = bits == thr
        key = jnp.where(gt, 1, 0) + jnp.where(eq, 1 << TIE_BITS, 0)
        lane = lax.broadcasted_iota(jnp.int32, (n_e, t), 1)
        csum = key
        sh_amt = 1
        while sh_amt < t:
            csum = csum + jnp.where(lane >= sh_amt, pltpu.roll(csum, sh_amt, 1), 0)
            sh_amt *= 2
        before = csum - key
        n_gt = before & ((1 << TIE_BITS) - 1)
        n_eq = before >> TIE_BITS
        need = cap - jnp.sum(jnp.where(gt, 1, 0), axis=1, keepdims=True)
        sel = gt | (eq & (n_eq < need))
        code_ref[...] = jnp.where(sel, n_gt + jnp.minimum(n_eq, need), -1)

    eg = xs_ref.shape[0]
    e0 = pl.program_id(1) * eg
    slot = lax.broadcasted_iota(jnp.int32, (cap, t), 0)
    onehot = []
    for l in range(eg):
        hit = code_ref[pl.ds(e0 + l, 1), :] == slot
        onehot.append(jnp.where(hit, 1.0, 0.0).astype(BF16))
        gs_ref[l] = jnp.sum(jnp.where(hit, aff_scr[pl.ds(e0 + l, 1), :], 0.0), axis=1, keepdims=True)
    gathered = _dot(onehot[0] if eg == 1 else jnp.concatenate(onehot, axis=0), h_scr[...])
    xs_ref[...] = gathered.reshape(eg, cap, d).astype(BF16)


def _moe_route(x, norm_g, sc, sh, router):
    bsz, t, d = x.shape
    n_e = router.shape[1]
    cap = CAPACITY_FACTOR * t // n_e
    assert t < (1 << TIE_BITS)
    eg = _tile(max(1, MOE_GATHER_ROWS // cap), n_e)
    vec = lambda b, e: (b, 0, 0)
    return pl.pallas_call(
        functools.partial(_moe_route_kernel, cap=cap),
        out_shape=[jax.ShapeDtypeStruct((bsz, n_e, cap, d), BF16),
                   jax.ShapeDtypeStruct((bsz, n_e, cap, 1), F32),
                   jax.ShapeDtypeStruct((bsz, n_e, t), jnp.int32)],
        grid=(bsz, n_e // eg),
        in_specs=[pl.BlockSpec((None, t, d), lambda b, e: (b, 0, 0)),
                  pl.BlockSpec((1, d), lambda b, e: (0, 0)),
                  pl.BlockSpec((None, 1, d), vec),
                  pl.BlockSpec((None, 1, d), vec),
                  pl.BlockSpec((n_e, d), lambda b, e: (0, 0))],
        out_specs=[pl.BlockSpec((None, eg, cap, d), lambda b, e: (b, e, 0, 0)),
                   pl.BlockSpec((None, eg, cap, 1), lambda b, e: (b, e, 0, 0)),
                   pl.BlockSpec((None, n_e, t), lambda b, e: (b, 0, 0))],
        scratch_shapes=[pltpu.VMEM((t, d), BF16), pltpu.VMEM((n_e, t), F32)],
        compiler_params=_params("parallel", "arbitrary"),
        name="moe_route",
    )(x, norm_g.reshape(1, d), sc, sh, router.T)


def _moe_ffn_kernel(*refs, n_main):
    two = len(refs) == 12
    xs_ref, gs_ref = refs[0], refs[1]
    xs2_ref, gs2_ref = (refs[2], refs[3]) if two else (None, None)
    wg_ref, wu_ref, wd_ref = refs[-7 - two:-4 - two]
    ys_ref = refs[-4 - two]
    ys2_ref = refs[-4] if two else None
    wg_scr, wu_scr, wd_scr = refs[-3:]
    j = pl.program_id(1)

    @pl.when(j == 0)
    def _():
        wg_scr[...] = wg_ref[...].astype(BF16)
        wu_scr[...] = wu_ref[...].astype(BF16)
        wd_scr[...] = wd_ref[...].astype(BF16)

    def ffn(x_ref, g_ref, o_ref):
        tb, cap, d = x_ref.shape
        xs = x_ref[...].reshape(tb * cap, d)
        hg = _dot(xs, wg_scr[...])
        hu = _dot(xs, wu_scr[...])
        hid = (hg * _sigmoid(hg) * hu).astype(BF16)
        ys = _dot(hid, wd_scr[...]) * g_ref[...].reshape(tb * cap, 1)
        o_ref[...] = ys.reshape(tb, cap, d).astype(BF16)

    if two:
        pl.when(j < n_main)(lambda: ffn(xs_ref, gs_ref, ys_ref))
        pl.when(j == n_main)(lambda: ffn(xs2_ref, gs2_ref, ys2_ref))
    else:
        ffn(xs_ref, gs_ref, ys_ref)


def _moe_ffn(xs, gs, wg, wu, wd, layer, second=None):
    bsz, n_e, cap, d = xs.shape
    f = wg.shape[-1]
    tb = max(1, min(bsz, MOE_FFN_ROWS // cap))
    while bsz % tb:
        tb -= 1
    n_main = bsz // tb
    main = lambda e, j: (jnp.minimum(j, n_main - 1), e, 0, 0)
    wspec = lambda shape: pl.BlockSpec((None, None) + shape, lambda e, j: (layer, e, 0, 0))
    acts, act_specs = [xs, gs], [pl.BlockSpec((tb, None, cap, d), main), pl.BlockSpec((tb, None, cap, 1), main)]
    out_shape = [jax.ShapeDtypeStruct((bsz, n_e, cap, d), BF16)]
    out_specs = [pl.BlockSpec((tb, None, cap, d), main)]
    if second is not None:
        xs2, gs2 = second
        b2, _, cap2, _ = xs2.shape
        whole = lambda e, j: (0, e, 0, 0)
        acts += [xs2, gs2]
        act_specs += [pl.BlockSpec((b2, None, cap2, d), whole), pl.BlockSpec((b2, None, cap2, 1), whole)]
        out_shape.append(jax.ShapeDtypeStruct((b2, n_e, cap2, d), BF16))
        out_specs.append(pl.BlockSpec((b2, None, cap2, d), whole))
    outs = pl.pallas_call(
        functools.partial(_moe_ffn_kernel, n_main=n_main),
        out_shape=out_shape,
        grid=(n_e, n_main + (second is not None)),
        in_specs=act_specs + [wspec((d, f)), wspec((d, f)), wspec((f, d))],
        out_specs=out_specs,
        scratch_shapes=[pltpu.VMEM((d, f), BF16), pltpu.VMEM((d, f), BF16), pltpu.VMEM((f, d), BF16)],
        compiler_params=_params("parallel", "arbitrary"),
        name="moe_ffn",
    )(*acts, wg, wu, wd)
    return outs if second is not None else outs[0]


def _moe_combine_kernel(code_ref, ys_ref, x_ref, gate_ref, fg_ref, o_ref, *, cap, final_norm):
    n_e = code_ref.shape[1]
    slot = lax.broadcasted_iota(jnp.int32, (1, cap), 1)
    pieces = [jnp.where(code_ref[:, e:e + 1] == slot, 1.0, 0.0).astype(BF16) for e in range(n_e)]
    scat = jnp.concatenate(pieces, axis=1)
    x = x_ref[...] + gate_ref[...] * _dot(scat, ys_ref[...])
    if final_norm:
        x = x * lax.rsqrt(jnp.mean(x * x, axis=-1, keepdims=True) + NORM_EPS) * fg_ref[...]
    o_ref[...] = x


def _moe_combine(code_t, ys, x, gate, final_g, final_norm):
    bsz, t, d = x.shape
    n_e = code_t.shape[-1]
    cap = ys.shape[1] // n_e
    tm = _tile(MOE_COMBINE_ROWS, t)
    return pl.pallas_call(
        functools.partial(_moe_combine_kernel, cap=cap, final_norm=final_norm),
        out_shape=jax.ShapeDtypeStruct((bsz, t, d), F32),
        grid=(bsz, t // tm),
        in_specs=[pl.BlockSpec((None, tm, n_e), lambda b, i: (b, i, 0)),
                  pl.BlockSpec((None, n_e * cap, d), lambda b, i: (b, 0, 0)),
                  pl.BlockSpec((None, tm, d), lambda b, i: (b, i, 0)),
                  pl.BlockSpec((None, 1, d), lambda b, i: (b, 0, 0)),
                  pl.BlockSpec((1, d), lambda b, i: (0, 0))],
        out_specs=pl.BlockSpec((None, tm, d), lambda b, i: (b, i, 0)),
        compiler_params=_params("parallel", "parallel"),
        name="moe_combine",
    )(code_t, ys, x, gate, final_g.reshape(1, d))


def _ec_moe_layer(streams, norm_g, router, wg, wu, wd, layer, final_g, final_norm):
    routed = [_moe_route(x, norm_g, sc, sh, router) for x, sc, sh, _ in streams]
    xs, gs, _ = routed[0]
    second = routed[1][:2] if len(routed) == 2 else None
    ys = _moe_ffn(xs, gs, wg, wu, wd, layer, second)
    ys = ys if second is not None else [ys]
    outs = []
    for n, ((x, _, _, gate), (xs_n, _, code)) in enumerate(zip(streams, routed)):
        bsz, n_e, cap, d = xs_n.shape
        outs.append(_moe_combine(jnp.swapaxes(code, 1, 2), ys[n].reshape(bsz, n_e * cap, d), x, gate,
                                 final_g, final_norm and n == 0))
    return outs


def kernel(x, c, ctx, c_ctx, ada_w, ada_b, norm_g, fnet_wo, fnet_bo, rw_mix, rw_wr, rw_wk, rw_wv, rw_wo,
           rw_w0, rw_w1, rw_w2, rw_a0, rw_a1, rw_a2, rw_v0, rw_v1, rw_v2, rw_g1, rw_g2, rw_kk, rw_ka, rw_rk,
           rw_lnx_w, rw_lnx_b, moe_router, moe_wg, moe_wu, moe_wd, final_g):
    bsz, t, d = x.shape
    t_ctx = ctx.shape[1]
    depth = ada_w.shape[0]
    n_mixers = 2

    rows = -(-(bsz + 1) // 8) * 8
    cc = jnp.concatenate([c, c_ctx[None, :], jnp.zeros((rows - bsz - 1, d), F32)], axis=0)
    mods = _adaln(cc, ada_w, ada_b)

    def mod_x(i, j):
        return mods[i, :bsz, j * d:(j + 1) * d].reshape(bsz, 1, d)

    def mod_c(i, j):
        return jnp.broadcast_to(mods[i, bsz, j * d:(j + 1) * d].reshape(1, 1, d), (bsz, 1, d))

    mats_x = _dft_mats(t, d // FNET_GROUPS)
    mats_c = _dft_mats(t_ctx, d // FNET_GROUPS)
    vfirst = None
    for i in range(depth):
        need_ctx = i < depth - 1
        if i % n_mixers == 0:
            fi = i // n_mixers
            wo_bf = fnet_wo[fi].astype(BF16)
            x = _fnet_layer(x, norm_g[i, 0], mod_x(i, 1), mod_x(i, 0), mod_x(i, 2), wo_bf, fnet_bo[fi], mats_x)
            if need_ctx:
                ctx = _fnet_layer(ctx, norm_g[i, 0], mod_c(i, 1), mod_c(i, 0), mod_c(i, 2), wo_bf,
                                  fnet_bo[fi], mats_c)
        else:
            ri = i // n_mixers
            p = dict(mix=rw_mix[ri], wr=rw_wr[ri], wk=rw_wk[ri], wv=rw_wv[ri], w0=rw_w0[ri], w1=rw_w1[ri],
                     w2=rw_w2[ri], a0=rw_a0[ri], a1=rw_a1[ri], a2=rw_a2[ri], g1=rw_g1[ri], g2=rw_g2[ri],
                     kk=rw_kk[ri], ka=rw_ka[ri])
            if ri > 0:
                p.update(v0=rw_v0[ri - 1], v1=rw_v1[ri - 1], v2=rw_v2[ri - 1])
            vf = vfirst if ri > 0 else None
            r, v, g, a, lw, kd, bd = _rwkv_project(ctx, x, norm_g[i, 0], (mod_c(i, 1), mod_c(i, 0)),
                                                   (mod_x(i, 1), mod_x(i, 0)), vf, p)
            if ri == 0:
                vfirst = v
            y = _wkv_scan(r, lw, kd, v, a, bd, 0, t_ctx, False)
            y = _wkv_scan(r, lw, kd, v, a, bd, 1, t_ctx, True, y_prev=y)
            wo_bf = rw_wo[ri].astype(BF16)
            args = (y, r, kd, v, g, rw_rk[ri], rw_lnx_w[ri], rw_lnx_b[ri], wo_bf)
            x = _rwkv_output(*args, x, mod_x(i, 2), t_ctx)
            if need_ctx:
                ctx = _rwkv_output(*args, ctx, mod_c(i, 2), 0)
        last = i == depth - 1
        streams = [(x, mod_x(i, 4), mod_x(i, 3), mod_x(i, 5))]
        if need_ctx:
            streams.append((ctx, mod_c(i, 4), mod_c(i, 3), mod_c(i, 5)))
        outs = _ec_moe_layer(streams, norm_g[i, 1], moe_router[i], moe_wg, moe_wu, moe_wd, i, final_g, last)
        x = outs[0]
        if need_ctx:
            ctx = outs[1]
    return x
```

```python
import functools

import jax
import jax.numpy as jnp
from jax import lax
from jax.experimental import pallas as pl
from jax.experimental.pallas import tpu as pltpu

F32 = jnp.float32
BF16 = jnp.bfloat16

HEAD_SIZE = 64
LANES = 128
LANE_GROUP = 256
GRID_W = 64
FNET_GROUPS = 4
CAPACITY_FACTOR = 2
NORM_EPS = 1e-6
LNX_EPS = 64e-5
DECAY_SCALE = 0.6065306597126334
WKV_CHUNK = 64
WKV_BLOCK = 256
WKV_WAVE = 2
MOE_GATHER_ROWS = 2048
MOE_FFN_ROWS = 1024
MOE_COMBINE_ROWS = 1024
TOKEN_TILE = 256
CHAN_TILE = 512
ADALN_COLS = 1536
TIE_BITS = 12
VMEM_LIMIT_BYTES = 56 * 1024 * 1024


def _params(*semantics):
    return pltpu.CompilerParams(dimension_semantics=semantics, vmem_limit_bytes=VMEM_LIMIT_BYTES)


def _tile(pref, *extents):
    tm = pref
    while any(n % tm for n in extents if n):
        tm //= 2
    return tm


def _dot(a, b):
    return jnp.dot(a, b, preferred_element_type=F32)


def _dot_nt(a, b):
    return lax.dot_general(a, b, (((1,), (1,)), ((), ())), preferred_element_type=F32)


def _split2(x):
    hi = x.astype(BF16)
    lo = (x - hi.astype(F32)).astype(BF16)
    return hi, lo


def _split3(x):
    hi = x.astype(BF16)
    r1 = x - hi.astype(F32)
    mid = r1.astype(BF16)
    lo = (r1 - mid.astype(F32)).astype(BF16)
    return hi, mid, lo


def _norm_mod(x, g, sc, sh):
    ms = jnp.mean(x * x, axis=-1, keepdims=True)
    return x * lax.rsqrt(ms + NORM_EPS) * g * (1.0 + sc) + sh


def _sigmoid(x):
    return 0.5 * jnp.tanh(0.5 * x) + 0.5


def _seg_ones(n):
    r = lax.broadcasted_iota(jnp.int32, (n, n), 0) // HEAD_SIZE
    c = lax.broadcasted_iota(jnp.int32, (n, n), 1) // HEAD_SIZE
    return jnp.where(r == c, 1.0, 0.0).astype(BF16)


def _head_sum(x, ones):
    d = x.shape[-1]
    w = ones.shape[0]
    parts = []
    for j in range(d // w):
        hi, lo = _split2(x[:, j * w:(j + 1) * w])
        parts.append(_dot(hi, ones) + _dot(lo, ones))
    return parts[0] if len(parts) == 1 else jnp.concatenate(parts, axis=1)


def _adaln_kernel(c_ref, w_ref, b_ref, o_ref):
    c = c_ref[...]
    s = c * _sigmoid(c)
    s_hi, s_lo = _split2(s)
    w_hi, w_lo = _split2(w_ref[...])
    o_ref[...] = _dot(s_hi, w_hi) + _dot(s_lo, w_hi) + _dot(s_hi, w_lo) + b_ref[...]


def _adaln(cc, ada_w, ada_b):
    depth, d, n = ada_w.shape
    rows = cc.shape[0]
    tn = _tile(ADALN_COLS, n)
    return pl.pallas_call(
        _adaln_kernel,
        out_shape=jax.ShapeDtypeStruct((depth, rows, n), F32),
        grid=(depth, n // tn),
        in_specs=[pl.BlockSpec((rows, d), lambda l, j: (0, 0)),
                  pl.BlockSpec((None, d, tn), lambda l, j: (l, 0, j)),
                  pl.BlockSpec((None, 1, tn), lambda l, j: (l, 0, j))],
        out_specs=pl.BlockSpec((None, rows, tn), lambda l, j: (l, 0, j)),
        compiler_params=_params("parallel", "parallel"),
        name="adaln",
    )(cc, ada_w, ada_b.reshape(depth, 1, n))


def _fnet_chan_kernel(x_ref, g_ref, sc_ref, sh_ref, cs_ref, o_ref):
    h = _norm_mod(x_ref[...], g_ref[...], sc_ref[...], sh_ref[...])
    gd = cs_ref.shape[0]
    cs = cs_ref[...]
    for j in range(h.shape[1] // gd):
        z = _dot(h[:, j * gd:(j + 1) * gd].astype(BF16), cs)
        o_ref[0, :, j * gd:(j + 1) * gd] = z[:, :gd].astype(BF16)
        o_ref[1, :, j * gd:(j + 1) * gd] = z[:, gd:].astype(BF16)


def _fnet_time_kernel(f_ref, hcs_ref, wo_ref, bo_ref, x_ref, gate_ref, o_ref):
    f = _dot(f_ref[...], hcs_ref[...])
    y = _dot(f.astype(BF16), wo_ref[...]) + bo_ref[...]
    o_ref[...] = x_ref[...] + gate_ref[...] * y


def _dft_mats(t, gd):
    def cs(n):
        i = jnp.arange(n, dtype=jnp.int32)
        ang = ((i[:, None] * i[None, :]) % n).astype(F32) * (2.0 * jnp.pi / n)
        return jnp.cos(ang), jnp.sin(ang)
    ct, st = cs(t)
    cc, sc = cs(gd)
    scale = 1.0 / jnp.sqrt(jnp.asarray(t * gd, F32))
    return (jnp.concatenate([ct, -st], axis=1).astype(BF16),
            (jnp.concatenate([cc, sc], axis=1) * scale).astype(BF16))


def _fnet_layer(x, norm_g, sc, sh, gate, wo_bf, bo, mats):
    bsz, t, d = x.shape
    f_mat, cs_mat = mats
    gd = d // FNET_GROUPS
    tm = _tile(CHAN_TILE, t)
    vec = lambda b, i: (b, 0, 0)
    hcs = pl.pallas_call(
        _fnet_chan_kernel,
        out_shape=jax.ShapeDtypeStruct((bsz, 2, t, d), BF16),
        grid=(bsz, t // tm),
        in_specs=[pl.BlockSpec((None, tm, d), lambda b, i: (b, i, 0)),
                  pl.BlockSpec((1, d), lambda b, i: (0, 0)),
                  pl.BlockSpec((None, 1, d), vec),
                  pl.BlockSpec((None, 1, d), vec),
                  pl.BlockSpec((gd, 2 * gd), lambda b, i: (0, 0))],
        out_specs=pl.BlockSpec((None, 2, tm, d), lambda b, i: (b, 0, i, 0)),
        compiler_params=_params("parallel", "parallel"),
        name="fnet_chan",
    )(x, norm_g.reshape(1, d), sc, sh, cs_mat)
    hcs = hcs.reshape(bsz, 2 * t, d)
    tm2 = _tile(CHAN_TILE, t)
    return pl.pallas_call(
        _fnet_time_kernel,
        out_shape=jax.ShapeDtypeStruct((bsz, t, d), F32),
        grid=(bsz, t // tm2),
        in_specs=[pl.BlockSpec((tm2, 2 * t), lambda b, i: (i, 0)),
                  pl.BlockSpec((None, 2 * t, d), lambda b, i: (b, 0, 0)),
                  pl.BlockSpec((d, d), lambda b, i: (0, 0)),
                  pl.BlockSpec((1, d), lambda b, i: (0, 0)),
                  pl.BlockSpec((None, tm2, d), lambda b, i: (b, i, 0)),
                  pl.BlockSpec((None, 1, d), vec)],
        out_specs=pl.BlockSpec((None, tm2, d), lambda b, i: (b, i, 0)),
        compiler_params=_params("parallel", "parallel"),
        name="fnet_time",
    )(f_mat, hcs, wo_bf, bo.reshape(1, d), x, gate)


def _shifted_grid(h, h_above, h_below):
    tm, d = h.shape
    q = d // 4
    colw = lax.broadcasted_iota(jnp.int32, (tm, q), 0) % GRID_W
    left = jnp.where(colw != 0, pltpu.roll(h[:, :q], 1, 0), 0.0)
    right = jnp.where(colw != GRID_W - 1, pltpu.roll(h[:, q:2 * q], tm - 1, 0), 0.0)
    up = jnp.concatenate([h_above[:, 2 * q:3 * q], h[:tm - GRID_W, 2 * q:3 * q]], axis=0)
    down = jnp.concatenate([h[GRID_W:, 3 * q:], h_below[:, 3 * q:]], axis=0)
    return jnp.concatenate([left, right, up, down], axis=1)


def _shifted_seq(h):
    tm, d = h.shape
    half = d // 2
    row = lax.broadcasted_iota(jnp.int32, (tm, half), 0)
    prev = jnp.where(row != 0, pltpu.roll(h[:, :half], 1, 0), 0.0)
    nxt = jnp.where(row != tm - 1, pltpu.roll(h[:, half:], tm - 1, 0), 0.0)
    return jnp.concatenate([prev, nxt], axis=1)


def _rwkv_proj_kernel(*refs, has_vres, n_ctx_tiles):
    it = iter(refs)
    c_ref, x_ref, xa_ref, xb_ref = next(it), next(it), next(it), next(it)
    ng_ref, scc_ref, shc_ref, scx_ref, shx_ref = next(it), next(it), next(it), next(it), next(it)
    vf_ref = next(it) if has_vres else None
    mix_ref, wr_ref, wk_ref, wv_ref = next(it), next(it), next(it), next(it)
    w0_ref, w1_ref, w2_ref = next(it), next(it), next(it)
    a0_ref, a1_ref, a2_ref = next(it), next(it), next(it)
    g1_ref, g2_ref, kk_ref, ka_ref = next(it), next(it), next(it), next(it)
    if has_vres:
        v0_ref, v1_ref, v2_ref = next(it), next(it), next(it)
    r_out, v_out, g_out, a_out = next(it), next(it), next(it), next(it)
    lw_out, kd_out, bd_out = next(it), next(it), next(it)

    i = pl.program_id(1)
    is_ctx = i < n_ctx_tiles
    ng = ng_ref[...]
    scx, shx = scx_ref[...], shx_ref[...]
    h = _norm_mod(jnp.where(is_ctx, c_ref[...], x_ref[...]), ng,
                  jnp.where(is_ctx, scc_ref[...], scx), jnp.where(is_ctx, shc_ref[...], shx))
    h_above = jnp.where(i > n_ctx_tiles, _norm_mod(xa_ref[...], ng, scx, shx), 0.0)
    h_below = jnp.where(i < pl.num_programs(1) - 1, _norm_mod(xb_ref[...], ng, scx, shx), 0.0)
    hs = jnp.where(is_ctx, _shifted_seq(h), _shifted_grid(h, h_above, h_below))
    xx = hs - h
    xr, xw, xk, xv, xa, xg = [(h + xx * mix_ref[j:j + 1, :]).astype(BF16) for j in range(6)]
    r = _dot(xr, wr_ref[...])
    k = _dot(xk, wk_ref[...])
    v = _dot(xv, wv_ref[...])
    if has_vres:
        lora = _dot(_dot(xv, v1_ref[...]).astype(BF16), v2_ref[...])
        v = v + (vf_ref[...] - v) * _sigmoid(v0_ref[...] + lora)
    g = _dot(_sigmoid(_dot(xg, g1_ref[...])).astype(BF16), g2_ref[...])
    ones = _seg_ones(min(LANE_GROUP, h.shape[1]))
    kk = k * kk_ref[...]
    kk = kk * lax.rsqrt(_head_sum(kk * kk, ones) + 1e-12)
    r_out[...] = r.astype(BF16)
    v_out[...] = v
    g_out[...] = g.astype(BF16)
    a_out[...] = (-kk).astype(BF16)
    k_in = k * ka_ref[...]
    k_out = k - k_in
    for d in range(2):
        wlog = w0_ref[d:d + 1, :] + _dot(jnp.tanh(_dot(xw, w1_ref[d])).astype(BF16), w2_ref[d])
        lw_out[d] = (-0.5 * DECAY_SCALE) * (jnp.tanh(0.5 * wlog) + 1.0)
        a = _sigmoid(a0_ref[d:d + 1, :] + _dot(_dot(xa, a1_ref[d]).astype(BF16), a2_ref[d]))
        kd_out[d] = (k_out + k_in * a).astype(BF16)
        bd_out[d] = (kk * a).astype(BF16)


def _rwkv_project(ctx, x, norm_g, mod_c, mod_x, vfirst, p):
    bsz, t, d = x.shape
    t_ctx = ctx.shape[1]
    tm = _tile(TOKEN_TILE, t, t_ctx)
    nct = t_ctx // tm
    assert nct == 1, "the sequence shift handles a context that fits one tile"
    ntx = t // tm
    has_vres = vfirst is not None
    tok = pl.BlockSpec((None, tm, d), lambda b, i: (b, i, 0))
    tok2 = pl.BlockSpec((2, None, tm, d), lambda b, i: (0, b, i, 0))
    vec = pl.BlockSpec((None, 1, d), lambda b, i: (b, 0, 0))

    def full(a):
        nd = a.ndim
        return pl.BlockSpec(a.shape, lambda b, i: (0,) * nd)

    bf = lambda a: a.astype(BF16)
    weights = [p["mix"], bf(p["wr"]), bf(p["wk"]), bf(p["wv"]),
               p["w0"], bf(p["w1"]), bf(p["w2"]), p["a0"], bf(p["a1"]), bf(p["a2"]),
               bf(p["g1"]), bf(p["g2"]), p["kk"].reshape(1, d), p["ka"].reshape(1, d)]
    if has_vres:
        weights += [p["v0"].reshape(1, d), bf(p["v1"]), bf(p["v2"])]
    rpt = tm // GRID_W
    last = t // GRID_W - 1
    acts = [ctx, x, x, x, norm_g.reshape(1, d), *mod_c, *mod_x]
    act_specs = [pl.BlockSpec((None, tm, d), lambda b, i: (b, jnp.minimum(i, nct - 1), 0)),
                 pl.BlockSpec((None, tm, d), lambda b, i: (b, jnp.maximum(i - nct, 0), 0)),
                 pl.BlockSpec((None, GRID_W, d), lambda b, i: (b, jnp.maximum((i - nct) * rpt - 1, 0), 0)),
                 pl.BlockSpec((None, GRID_W, d), lambda b, i: (b, jnp.clip((i - nct + 1) * rpt, 0, last), 0)),
                 pl.BlockSpec((1, d), lambda b, i: (0, 0)), vec, vec, vec, vec]
    if has_vres:
        acts.append(vfirst)
        act_specs.append(tok)
    tt = t_ctx + t
    out_shape = [jax.ShapeDtypeStruct((bsz, tt, d), BF16),
                 jax.ShapeDtypeStruct((bsz, tt, d), F32),
                 jax.ShapeDtypeStruct((bsz, tt, d), BF16),
                 jax.ShapeDtypeStruct((bsz, tt, d), BF16),
                 jax.ShapeDtypeStruct((2, bsz, tt, d), F32),
                 jax.ShapeDtypeStruct((2, bsz, tt, d), BF16),
                 jax.ShapeDtypeStruct((2, bsz, tt, d), BF16)]
    return pl.pallas_call(
        functools.partial(_rwkv_proj_kernel, has_vres=has_vres, n_ctx_tiles=nct),
        out_shape=out_shape,
        grid=(bsz, nct + ntx),
        in_specs=act_specs + [full(w) for w in weights],
        out_specs=[tok, tok, tok, tok, tok2, tok2, tok2],
        compiler_params=_params("parallel", "parallel"),
        name="rwkv_proj",
    )(*acts, *weights)


def _wkv_kernel(*refs, reverse, add_prev):
    r_ref, lw_ref, k_ref, v_ref, a_ref, b_ref = refs[:6]
    yp_ref = refs[6] if add_prev else None
    y_ref, s_scr, lhs_scr, n_scr, wl_scr = refs[-5:]
    _wkv_body(r_ref, lw_ref, k_ref, v_ref, a_ref, b_ref, yp_ref, y_ref, s_scr, lhs_scr, n_scr, wl_scr, reverse)


def _wkv_body(r_ref, lw_ref, k_ref, v_ref, a_ref, b_ref, yp_ref, y_ref, s_scr, lhs_scr, n_scr, wl_scr, reverse):
    L = WKV_CHUNK
    tb, d = r_ref.shape
    nch = tb // L
    gw = s_scr.shape[-1]
    hpg = gw // HEAD_SIZE
    ng = d // gw

    @pl.when(pl.program_id(1) == 0)
    def _():
        s_scr[...] = jnp.zeros_like(s_scr)

    ti = lax.broadcasted_iota(jnp.int32, (L, L), 0)
    si = lax.broadcasted_iota(jnp.int32, (L, L), 1)
    tri = jnp.where((si >= ti) if reverse else (si <= ti), 1.0, 0.0).astype(BF16)
    last, mid = (0, L // 2) if reverse else (L - 1, L // 2 - 1)
    trow = lax.broadcasted_iota(jnp.int32, (L, gw), 0)
    scol = lax.broadcasted_iota(jnp.int32, (L, gw), 1) % HEAD_SIZE
    strict = (scol > trow) if reverse else (scol < trow)
    incl = (scol >= trow) if reverse else (scol <= trow)
    eye = jnp.where(scol == trow, 1.0, 0.0)
    bdm = (lax.broadcasted_iota(jnp.int32, (gw, gw), 0) // HEAD_SIZE
           == lax.broadcasted_iota(jnp.int32, (gw, gw), 1) // HEAD_SIZE)

    def bdiag(z):
        return jnp.where(bdm, jnp.concatenate([z] * hpg, axis=0), 0.0).astype(BF16)

    def per_head(f, lo):
        return jnp.concatenate([f[h * HEAD_SIZE:(h + 1) * HEAD_SIZE, lo:lo + L] for h in range(hpg)], axis=1)

    def compact(f):
        fm = jnp.where(bdm, f, 0.0)
        out = fm[:HEAD_SIZE]
        for h in range(1, hpg):
            out = out + fm[h * HEAD_SIZE:(h + 1) * HEAD_SIZE]
        return out

    def prep(j):
        rows = slice(j * L, (j + 1) * L)
        lw = lw_ref[rows, :]
        p1, p2, p3 = _split3(lw)
        cw = _dot(tri, p1) + _dot(tri, p2) + _dot(tri, p3)
        cw_end = cw[last:last + 1, :]
        cw_mid = cw[mid:mid + 1, :]
        r = r_ref[rows, :].astype(F32)
        k = k_ref[rows, :].astype(F32)
        a = a_ref[rows, :].astype(F32)
        b = b_ref[rows, :].astype(F32)
        e_in = jnp.exp(cw)
        e_ex = jnp.exp(cw - lw)
        e_inv = jnp.exp(cw_mid - cw)
        e_mid = jnp.exp(-cw_mid)
        e_end = e_inv * jnp.exp(cw_end - cw_mid)
        r0 = r * e_in
        a0 = a * e_ex
        return dict(rows=rows, cw_end=cw_end, r0=r0, a0=a0, ra=r0 * e_mid, aa=a0 * e_mid, kb=k * e_inv, bb=b * e_inv,
                    ke=k * e_end, be=b * e_end)

    for w0 in range(0, nch, WKV_WAVE):
        pj = {j: prep(j) for j in range(w0, min(nch, w0 + WKV_WAVE))}
        ch = [(j, slice(g * gw, (g + 1) * gw)) for j in pj for g in range(ng)]
        cs = range(len(ch))
        col = lambda name, c: pj[ch[c][0]][name][:, ch[c][1]]
        xq = [jnp.concatenate([col("aa", c), col("ra", c)], axis=0).astype(BF16) for c in cs]
        gb = [_dot_nt(xq[c], bdiag(col("bb", c))) for c in cs]
        gk = [_dot_nt(xq[c], bdiag(col("kb", c))) for c in cs]
        a_ab = [jnp.where(strict, gb[c][:L], 0.0) for c in cs]
        vg = [v_ref[pj[ch[c][0]]["rows"], ch[c][1]] for c in cs]
        ft = [jnp.concatenate([col("be", c), col("ke", c)], axis=0).T for c in cs]
        bet = [per_head(ft[c], 0) for c in cs]
        ket = [per_head(ft[c], L) for c in cs]
        akrk = [jnp.concatenate([jnp.where(strict, gk[c][:L], 0.0), jnp.where(incl, gk[c][L:], 0.0), ket[c]],
                                axis=0).astype(BF16) for c in cs]
        tmat = [eye + a_ab[c] for c in cs]
        pw = [_dot(a_ab[c].astype(BF16), bdiag(a_ab[c])) for c in cs]
        avk = [_dot(akrk[c], bdiag(vg[c])) for c in cs]
        n_lvl = L.bit_length() - 2
        for lvl in range(n_lvl):
            if lvl + 1 < n_lvl:
                z = [_dot(jnp.concatenate([tmat[c], pw[c]], axis=0).astype(BF16), bdiag(pw[c])) for c in cs]
                tmat = [tmat[c] + z[c][:L] for c in cs]
                pw = [z[c][L:] for c in cs]
            else:
                z = [_dot(tmat[c].astype(BF16), bdiag(pw[c])) for c in cs]
                tmat = [tmat[c] + z[c] for c in cs]
        rbe = [jnp.concatenate([jnp.where(incl, gb[c][L:], 0.0), bet[c]], axis=0).astype(BF16) for c in cs]
        rt = [_dot(rbe[c], bdiag(tmat[c])).astype(BF16) for c in cs]
        za = [_dot(rt[c], bdiag(col("a0", c))) for c in cs]
        zv = [_dot(rt[c], bdiag(avk[c][:L])) for c in cs]
        ridx = lax.broadcasted_iota(jnp.int32, (LANES, d), 0)
        ends = jnp.zeros((LANES, d), F32)
        for n, j in enumerate(pj):
            ends = jnp.where(ridx == n, pj[j]["cw_end"], ends)
        w_end = jnp.exp(ends.T)
        for c in cs:
            j, sl = ch[c]
            g = c % ng
            y_part = zv[c][:L] + avk[c][L:2 * L]
            if yp_ref is not None:
                y_part = y_part + yp_ref[pj[j]["rows"], sl]
            y_ref[pj[j]["rows"], sl] = y_part
            lhs_scr[j, g, :L, :] = (col("r0", c) + za[c][:L]).astype(BF16)
            lhs_scr[j, g, L:, :] = za[c][L:].astype(BF16)
            n_scr[j, g] = zv[c][L:] + avk[c][2 * L:]
            wcol = w_end[sl, c // ng:c // ng + 1]
            wl_scr[j, g] = compact(jnp.broadcast_to(wcol, (gw, gw)))

    for j in (reversed(range(nch)) if reverse else range(nch)):
        rows = slice(j * L, (j + 1) * L)
        s_old = [s_scr[g] for g in range(ng)]
        z = [_dot(lhs_scr[j, g], bdiag(s_old[g])) for g in range(ng)]
        for g in range(ng):
            y_ref[rows, g * gw:(g + 1) * gw] += z[g][:L]
            s_scr[g] = wl_scr[j, g] * s_old[g] + z[g][L:] + n_scr[j, g]


def _wkv_scan(r, lw, k, v, a, b, d_idx, t_ctx, reverse, y_prev=None):
    bsz, t, d = r.shape
    L = WKV_CHUNK
    tb = _tile(WKV_BLOCK, t_ctx, t - t_ctx)
    nb = t // tb
    n_ctx_blocks = t_ctx // tb
    nch = tb // L
    gw = min(LANE_GROUP, d)
    ng = d // gw
    if reverse:
        cidx = lambda c: jnp.where(c < n_ctx_blocks, n_ctx_blocks - 1 - c, nb + n_ctx_blocks - 1 - c)
    else:
        cidx = lambda c: c
    tok = pl.BlockSpec((None, tb, d), lambda bb, c: (bb, cidx(c), 0))
    tokd = pl.BlockSpec((None, None, tb, d), lambda bb, c: (d_idx, bb, cidx(c), 0))
    prev = [] if y_prev is None else [y_prev]
    return pl.pallas_call(
        functools.partial(_wkv_kernel, reverse=reverse, add_prev=bool(prev)),
        out_shape=jax.ShapeDtypeStruct((bsz, t, d), F32),
        grid=(bsz, nb),
        in_specs=[tok, tokd, tokd, tok, tok, tokd] + [tok] * len(prev),
        out_specs=tok,
        scratch_shapes=[pltpu.VMEM((ng, HEAD_SIZE, gw), F32),
                        pltpu.VMEM((nch, ng, 2 * L, gw), BF16),
                        pltpu.VMEM((nch, ng, HEAD_SIZE, gw), F32),
                        pltpu.VMEM((nch, ng, HEAD_SIZE, gw), F32)],
        compiler_params=_params("parallel", "arbitrary"),
        name="wkv_rev" if reverse else "wkv_fwd",
    )(r, lw, k, v, a, b, *prev)


def _rwkv_out_kernel(y_ref, r_ref, k_ref, v_ref, g_ref, rk_ref, lnw_ref, lnb_ref,
                     wo_ref, x_ref, gate_ref, o_ref):
    y = y_ref[...]
    ones = _seg_ones(min(LANE_GROUP, y.shape[1]))
    inv_n = 1.0 / HEAD_SIZE
    mu = _head_sum(y, ones) * inv_n
    yc = y - mu
    var = _head_sum(yc * yc, ones) * inv_n
    yn = yc * lax.rsqrt(var + LNX_EPS) * lnw_ref[...] + lnb_ref[...]
    ksum = k_ref[0].astype(F32) + k_ref[1].astype(F32)
    bonus = _head_sum(r_ref[...].astype(F32) * ksum * rk_ref[...], ones) * v_ref[...]
    out = ((yn + bonus) * g_ref[...].astype(F32)).astype(BF16)
    o_ref[...] = x_ref[...] + gate_ref[...] * _dot(out, wo_ref[...])


def _rwkv_output(y, r, kd, v, g, rk, lnw, lnb, wo_bf, x, gate, t_off):
    bsz, t, d = x.shape
    tm = _tile(TOKEN_TILE, t, t_off)
    off = t_off // tm
    tok = pl.BlockSpec((None, tm, d), lambda b, i: (b, i + off, 0))
    tok2 = pl.BlockSpec((2, None, tm, d), lambda b, i: (0, b, i + off, 0))
    row = pl.BlockSpec((1, d), lambda b, i: (0, 0))
    return pl.pallas_call(
        _rwkv_out_kernel,
        out_shape=jax.ShapeDtypeStruct((bsz, t, d), F32),
        grid=(bsz, t // tm),
        in_specs=[tok, tok, tok2, tok, tok, row, row, row,
                  pl.BlockSpec((d, d), lambda b, i: (0, 0)),
                  pl.BlockSpec((None, tm, d), lambda b, i: (b, i, 0)),
                  pl.BlockSpec((None, 1, d), lambda b, i: (b, 0, 0))],
        out_specs=pl.BlockSpec((None, tm, d), lambda b, i: (b, i, 0)),
        compiler_params=_params("parallel", "parallel"),
        name="rwkv_out",
    )(y, r, kd, v, g, rk.reshape(1, d), lnw.reshape(1, d), lnb.reshape(1, d), wo_bf, x, gate)


def _moe_route_kernel(x_ref, g_ref, sc_ref, sh_ref, rt_ref, xs_ref, gs_ref, code_ref,
                      h_scr, aff_scr, *, cap):
    t, d = x_ref.shape
    n_e = rt_ref.shape[0]

    @pl.when(pl.program_id(1) == 0)
    def _():
        g = g_ref[...]
        sc = sc_ref[...]
        sh = sh_ref[...]
        rt_hi, rt_lo = _split2(rt_ref[...])
        tc = _tile(TOKEN_TILE, t)
        for j in range(t // tc):
            h = _norm_mod(x_ref[j * tc:(j + 1) * tc, :], g, sc, sh)
            h_hi, h_lo = _split2(h)
            h_scr[j * tc:(j + 1) * tc, :] = h_hi
            aff_scr[:, j * tc:(j + 1) * tc] = (_dot_nt(rt_hi, h_hi) + _dot_nt(rt_hi, h_lo)
                                               + _dot_nt(rt_lo, h_hi))
        logits = aff_scr[...]
        m = jnp.max(logits, axis=0, keepdims=True)
        ex = jnp.exp(logits - m)
        aff = ex / jnp.sum(ex, axis=0, keepdims=True)
        aff_scr[...] = aff
        bits = pltpu.bitcast(aff, jnp.int32)

        def count_ge(cand):
            return jnp.sum(jnp.where(bits >= cand, 1, 0), axis=1, keepdims=True)

        def search(i, thr):
            lo = 28 - 2 * i
            c1, c2, c3 = [thr | (jnp.int32(m) << lo) for m in (1, 2, 3)]
            thr = jnp.where(count_ge(c1) >= cap, c1, thr)
            thr = jnp.where(count_ge(c2) >= cap, c2, thr)
            return jnp.where(count_ge(c3) >= cap, c3, thr)

        thr = jnp.zeros((n_e, 1), jnp.int32)
        thr = jnp.where(count_ge(thr | (jnp.int32(1) << 30)) >= cap, thr | (jnp.int32(1) << 30), thr)
        thr = lax.fori_loop(0, 15, search, thr)
        gt = bits > thr
        eq = bits == thr
        key = jnp.where(gt, 1, 0) + jnp.where(eq, 1 << TIE_BITS, 0)
        lane = lax.broadcasted_iota(jnp.int32, (n_e, t), 1)
        csum = key
        sh_amt = 1
        while sh_amt < t:
            csum = csum + jnp.where(lane >= sh_amt, pltpu.roll(csum, sh_amt, 1), 0)
            sh_amt *= 2
        before = csum - key
        n_gt = before & ((1 << TIE_BITS) - 1)
        n_eq = before >> TIE_BITS
        need = cap - jnp.sum(jnp.where(gt, 1, 0), axis=1, keepdims=True)
        sel = gt | (eq & (n_eq < need))
        code_ref[...] = jnp.where(sel, n_gt + jnp.minimum(n_eq, need), -1)

    eg = xs_ref.shape[0]
    e0 = pl.program_id(1) * eg
    slot = lax.broadcasted_iota(jnp.int32, (cap, t), 0)
    onehot = []
    for l in range(eg):
        hit = code_ref[pl.ds(e0 + l, 1), :] == slot
        onehot.append(jnp.where(hit, 1.0, 0.0).astype(BF16))
        gs_ref[l] = jnp.sum(jnp.where(hit, aff_scr[pl.ds(e0 + l, 1), :], 0.0), axis=1, keepdims=True)
    gathered = _dot(onehot[0] if eg == 1 else jnp.concatenate(onehot, axis=0), h_scr[...])
    xs_ref[...] = gathered.reshape(eg, cap, d).astype(BF16)


def _moe_route(x, norm_g, sc, sh, router):
    bsz, t, d = x.shape
    n_e = router.shape[1]
    cap = CAPACITY_FACTOR * t // n_e
    assert t < (1 << TIE_BITS)
    eg = _tile(max(1, MOE_GATHER_ROWS // cap), n_e)
    vec = lambda b, e: (b, 0, 0)
    return pl.pallas_call(
        functools.partial(_moe_route_kernel, cap=cap),
        out_shape=[jax.ShapeDtypeStruct((bsz, n_e, cap, d), BF16),
                   jax.ShapeDtypeStruct((bsz, n_e, cap, 1), F32),
                   jax.ShapeDtypeStruct((bsz, n_e, t), jnp.int32)],
        grid=(bsz, n_e // eg),
        in_specs=[pl.BlockSpec((None, t, d), lambda b, e: (b, 0, 0)),
                  pl.BlockSpec((1, d), lambda b, e: (0, 0)),
                  pl.BlockSpec((None, 1, d), vec),
                  pl.BlockSpec((None, 1, d), vec),
                  pl.BlockSpec((n_e, d), lambda b, e: (0, 0))],
        out_specs=[pl.BlockSpec((None, eg, cap, d), lambda b, e: (b, e, 0, 0)),
                   pl.BlockSpec((None, eg, cap, 1), lambda b, e: (b, e, 0, 0)),
                   pl.BlockSpec((None, n_e, t), lambda b, e: (b, 0, 0))],
        scratch_shapes=[pltpu.VMEM((t, d), BF16), pltpu.VMEM((n_e, t), F32)],
        compiler_params=_params("parallel", "arbitrary"),
        name="moe_route",
    )(x, norm_g.reshape(1, d), sc, sh, router.T)


def _moe_ffn_kernel(xs_ref, gs_ref, wg_ref, wu_ref, wd_ref, ys_ref, wg_scr, wu_scr, wd_scr):
    @pl.when(pl.program_id(1) == 0)
    def _():
        wg_scr[...] = wg_ref[...].astype(BF16)
        wu_scr[...] = wu_ref[...].astype(BF16)
        wd_scr[...] = wd_ref[...].astype(BF16)

    tb, cap, d = xs_ref.shape
    xs = xs_ref[...].reshape(tb * cap, d)
    hg = _dot(xs, wg_scr[...])
    hu = _dot(xs, wu_scr[...])
    hid = (hg * _sigmoid(hg) * hu).astype(BF16)
    ys = _dot(hid, wd_scr[...]) * gs_ref[...].reshape(tb * cap, 1)
    ys_ref[...] = ys.reshape(tb, cap, d).astype(BF16)


def _moe_ffn(xs, gs, wg, wu, wd, layer):
    bsz, n_e, cap, d = xs.shape
    f = wg.shape[-1]
    tb = max(1, min(bsz, MOE_FFN_ROWS // cap))
    while bsz % tb:
        tb -= 1
    return pl.pallas_call(
        _moe_ffn_kernel,
        out_shape=jax.ShapeDtypeStruct((bsz, n_e, cap, d), BF16),
        grid=(n_e, bsz // tb),
        in_specs=[pl.BlockSpec((tb, None, cap, d), lambda e, j: (j, e, 0, 0)),
                  pl.BlockSpec((tb, None, cap, 1), lambda e, j: (j, e, 0, 0)),
                  pl.BlockSpec((None, None, d, f), lambda e, j: (layer, e, 0, 0)),
                  pl.BlockSpec((None, None, d, f), lambda e, j: (layer, e, 0, 0)),
                  pl.BlockSpec((None, None, f, d), lambda e, j: (layer, e, 0, 0))],
        out_specs=pl.BlockSpec((tb, None, cap, d), lambda e, j: (j, e, 0, 0)),
        scratch_shapes=[pltpu.VMEM((d, f), BF16), pltpu.VMEM((d, f), BF16), pltpu.VMEM((f, d), BF16)],
        compiler_params=_params("parallel", "arbitrary"),
        name="moe_ffn",
    )(xs, gs, wg, wu, wd)


def _moe_combine_kernel(code_ref, ys_ref, x_ref, gate_ref, fg_ref, o_ref, *, cap, final_norm):
    n_e = code_ref.shape[1]
    slot = lax.broadcasted_iota(jnp.int32, (1, cap), 1)
    pieces = [jnp.where(code_ref[:, e:e + 1] == slot, 1.0, 0.0).astype(BF16) for e in range(n_e)]
    scat = jnp.concatenate(pieces, axis=1)
    x = x_ref[...] + gate_ref[...] * _dot(scat, ys_ref[...])
    if final_norm:
        x = x * lax.rsqrt(jnp.mean(x * x, axis=-1, keepdims=True) + NORM_EPS) * fg_ref[...]
    o_ref[...] = x


def _moe_combine(code_t, ys, x, gate, final_g, final_norm):
    bsz, t, d = x.shape
    n_e = code_t.shape[-1]
    cap = ys.shape[1] // n_e
    tm = _tile(MOE_COMBINE_ROWS, t)
    return pl.pallas_call(
        functools.partial(_moe_combine_kernel, cap=cap, final_norm=final_norm),
        out_shape=jax.ShapeDtypeStruct((bsz, t, d), F32),
        grid=(bsz, t // tm),
        in_specs=[pl.BlockSpec((None, tm, n_e), lambda b, i: (b, i, 0)),
                  pl.BlockSpec((None, n_e * cap, d), lambda b, i: (b, 0, 0)),
                  pl.BlockSpec((None, tm, d), lambda b, i: (b, i, 0)),
                  pl.BlockSpec((None, 1, d), lambda b, i: (b, 0, 0)),
                  pl.BlockSpec((1, d), lambda b, i: (0, 0))],
        out_specs=pl.BlockSpec((None, tm, d), lambda b, i: (b, i, 0)),
        compiler_params=_params("parallel", "parallel"),
        name="moe_combine",
    )(code_t, ys, x, gate, final_g.reshape(1, d))


def _ec_moe_layer(x, norm_g, sc, sh, gate, router, wg, wu, wd, layer, final_g, final_norm):
    bsz, t, d = x.shape
    xs, gs, code = _moe_route(x, norm_g, sc, sh, router)
    ys = _moe_ffn(xs, gs, wg, wu, wd, layer)
    n_e, cap = xs.shape[1], xs.shape[2]
    return _moe_combine(jnp.swapaxes(code, 1, 2), ys.reshape(bsz, n_e * cap, d), x, gate,
                        final_g, final_norm)


def kernel(x, c, ctx, c_ctx, ada_w, ada_b, norm_g, fnet_wo, fnet_bo, rw_mix, rw_wr, rw_wk, rw_wv, rw_wo,
           rw_w0, rw_w1, rw_w2, rw_a0, rw_a1, rw_a2, rw_v0, rw_v1, rw_v2, rw_g1, rw_g2, rw_kk, rw_ka, rw_rk,
           rw_lnx_w, rw_lnx_b, moe_router, moe_wg, moe_wu, moe_wd, final_g):
    bsz, t, d = x.shape
    t_ctx = ctx.shape[1]
    depth = ada_w.shape[0]
    n_mixers = 2

    rows = -(-(bsz + 1) // 8) * 8
    cc = jnp.concatenate([c, c_ctx[None, :], jnp.zeros((rows - bsz - 1, d), F32)], axis=0)
    mods = _adaln(cc, ada_w, ada_b)

    def mod_x(i, j):
        return mods[i, :bsz, j * d:(j + 1) * d].reshape(bsz, 1, d)

    def mod_c(i, j):
        return jnp.broadcast_to(mods[i, bsz, j * d:(j + 1) * d].reshape(1, 1, d), (bsz, 1, d))

    mats_x = _dft_mats(t, d // FNET_GROUPS)
    mats_c = _dft_mats(t_ctx, d // FNET_GROUPS)
    vfirst = None
    for i in range(depth):
        need_ctx = i < depth - 1
        if i % n_mixers == 0:
            fi = i // n_mixers
            wo_bf = fnet_wo[fi].astype(BF16)
            x = _fnet_layer(x, norm_g[i, 0], mod_x(i, 1), mod_x(i, 0), mod_x(i, 2), wo_bf, fnet_bo[fi], mats_x)
            if need_ctx:
                ctx = _fnet_layer(ctx, norm_g[i, 0], mod_c(i, 1), mod_c(i, 0), mod_c(i, 2), wo_bf,
                                  fnet_bo[fi], mats_c)
        else:
            ri = i // n_mixers
            p = dict(mix=rw_mix[ri], wr=rw_wr[ri], wk=rw_wk[ri], wv=rw_wv[ri], w0=rw_w0[ri], w1=rw_w1[ri],
                     w2=rw_w2[ri], a0=rw_a0[ri], a1=rw_a1[ri], a2=rw_a2[ri], g1=rw_g1[ri], g2=rw_g2[ri],
                     kk=rw_kk[ri], ka=rw_ka[ri])
            if ri > 0:
                p.update(v0=rw_v0[ri - 1], v1=rw_v1[ri - 1], v2=rw_v2[ri - 1])
            vf = vfirst if ri > 0 else None
            r, v, g, a, lw, kd, bd = _rwkv_project(ctx, x, norm_g[i, 0], (mod_c(i, 1), mod_c(i, 0)),
                                                   (mod_x(i, 1), mod_x(i, 0)), vf, p)
            if ri == 0:
                vfirst = v
            y = _wkv_scan(r, lw, kd, v, a, bd, 0, t_ctx, False)
            y = _wkv_scan(r, lw, kd, v, a, bd, 1, t_ctx, True, y_prev=y)
            wo_bf = rw_wo[ri].astype(BF16)
            args = (y, r, kd, v, g, rw_rk[ri], rw_lnx_w[ri], rw_lnx_b[ri], wo_bf)
            x = _rwkv_output(*args, x, mod_x(i, 2), t_ctx)
            if need_ctx:
                ctx = _rwkv_output(*args, ctx, mod_c(i, 2), 0)
        last = i == depth - 1
        x = _ec_moe_layer(x, norm_g[i, 1], mod_x(i, 4), mod_x(i, 3), mod_x(i, 5), moe_router[i],
                          moe_wg, moe_wu, moe_wd, i, final_g, last)
        if need_ctx:
            ctx = _ec_moe_layer(ctx, norm_g[i, 1], mod_c(i, 4), mod_c(i, 3), mod_c(i, 5), moe_router[i],
                                moe_wg, moe_wu, moe_wd, i, final_g, False)
    return x
```
